```python
import math
import jax
import jax.numpy as jnp
from jax import lax
import numpy as np

D_MODEL = 1024
BATCH = 32
SEQ = 2048
DEPTH = 4

CTX_LEN = 256
GRID_W = 64
N_GROUPS = 4
GROUP_W = D_MODEL // N_GROUPS
MIX_W = N_GROUPS * GROUP_W
HEAD_DIM = 64
ROPE_DIM = 32
ROPE_THETA = 10000.0
QBLOCK = 128
NORM_EPS = 1e-6
NEG_INF = -1e30

NA_HEADS = GROUP_W // HEAD_DIM
NA_KH = 8
NA_KW = 16
MLA_HEADS = GROUP_W // HEAD_DIM
MLA_Q_RANK = GROUP_W
MLA_KV_RANK = D_MODEL // 8
MLA_NOPE = HEAD_DIM
MLA_V = HEAD_DIM
RWKV_HEADS = GROUP_W // HEAD_DIM
RWKV_N = HEAD_DIM
RWKV_W_RANK = 32
RWKV_A_RANK = 32
RWKV_G_RANK = 64
RWKV_LN_EPS = 64e-5
DIFF_HEADS = GROUP_W // HEAD_DIM
DIFF_QK = HEAD_DIM // 2
DIFF_V = HEAD_DIM
DIFF_LN_EPS = 1e-5
D_FF = ((8 * D_MODEL // 3 + 127) // 128) * 128

NA_COLS = 3 * GROUP_W
MLA_COLS = MLA_Q_RANK + MLA_KV_RANK + ROPE_DIM
RWKV_COLS = 3 * GROUP_W + RWKV_W_RANK + RWKV_A_RANK + RWKV_G_RANK
DIFF_COLS = 3 * GROUP_W
IN_COLS = NA_COLS + MLA_COLS + RWKV_COLS + DIFF_COLS
IN_SPLITS = (NA_COLS, NA_COLS + MLA_COLS, NA_COLS + MLA_COLS + RWKV_COLS)
RWKV_SPLITS = (GROUP_W, 2 * GROUP_W, 3 * GROUP_W, 3 * GROUP_W + RWKV_W_RANK,
               3 * GROUP_W + RWKV_W_RANK + RWKV_A_RANK)

kernel_name = 'hybrid_parallel_group_dit'


def rmsnorm(x, g, eps=NORM_EPS):
    xf = x.astype(jnp.float32)
    y = xf * lax.rsqrt(jnp.mean(xf * xf, axis=-1, keepdims=True) + eps)
    return (y * g.astype(jnp.float32)).astype(x.dtype)


def shift_prev(x):
    return jnp.pad(x, ((0, 0), (1, 0), (0, 0)))[:, :-1]


def shift_next(x):
    return jnp.pad(x, ((0, 0), (0, 1), (0, 0)))[:, 1:]


def axial_rope_tables(n_tokens, dtype):
    t = jnp.arange(n_tokens)
    row = (t // GRID_W).astype(jnp.float32)
    col = (t % GRID_W).astype(jnp.float32)
    half = ROPE_DIM // 2
    freqs = ROPE_THETA ** (-jnp.arange(0, half, 2, dtype=jnp.float32) / half)
    ar = row[:, None] * freqs[None, :]
    ac = col[:, None] * freqs[None, :]
    ang = jnp.concatenate([ar, ar, ac, ac], axis=-1)
    return jnp.cos(ang).astype(dtype), jnp.sin(ang).astype(dtype)


def apply_axial_rope(x, cos, sin):
    r1, r2, c1, c2 = jnp.split(x, 4, axis=-1)
    rot = jnp.concatenate([-r2, r1, -c2, c1], axis=-1)
    shape = (1, cos.shape[0]) + (1,) * (x.ndim - 3) + (cos.shape[1],)
    return x * cos.reshape(shape) + rot * sin.reshape(shape)


def map_query_blocks(fn, q):
    B, T = q.shape[0], q.shape[1]
    nb = T // QBLOCK
    qb = jnp.moveaxis(q.reshape((B, nb, QBLOCK) + q.shape[2:]), 1, 0)
    o = jnp.moveaxis(lax.map(fn, qb), 0, 1)
    return o.reshape((B, T) + o.shape[3:])


def softmax_attention(q, k, v, scale):
    s = jnp.einsum('bqhd,bkhd->bhqk', q, k).astype(jnp.float32) * scale
    p = jax.nn.softmax(s, axis=-1).astype(v.dtype)
    return jnp.einsum('bhqk,bkhd->bqhd', p, v)


def neighbourhood_attention(q, k, v, kc, vc, rpb):
    B, T, H, d = q.shape
    rows = T // GRID_W
    kh = min(NA_KH, rows)
    scale = d ** -0.5
    qg = q.reshape(B, rows, GRID_W, H, d)
    kg = k.reshape(B, rows, GRID_W, H, d)
    vg = v.reshape(B, rows, GRID_W, H, d)
    cpos = np.arange(GRID_W)
    cstart = np.clip(cpos - NA_KW // 2, 0, GRID_W - NA_KW)
    col_mask = (cpos[None, :] >= cstart[:, None]) & (cpos[None, :] < cstart[:, None] + NA_KW)
    col_idx = np.clip(cpos[None, :] - cpos[:, None] + NA_KW - 1, 0, 2 * NA_KW - 2)
    mask = jnp.asarray(np.tile(col_mask, (1, kh)))
    n_lat = kh * GRID_W

    def row_block(r):
        rs = jnp.clip(r - kh // 2, 0, rows - kh)
        q_r = lax.dynamic_index_in_dim(qg, r, axis=1, keepdims=False)
        k_r = lax.dynamic_slice_in_dim(kg, rs, kh, axis=1).reshape(B, n_lat, H, d)
        v_r = lax.dynamic_slice_in_dim(vg, rs, kh, axis=1).reshape(B, n_lat, H, d)
        row_off = rs + jnp.arange(kh) - r + NA_KH - 1
        bias = rpb[:, row_off][:, :, col_idx]
        bias = bias.transpose(0, 2, 1, 3).reshape(H, GRID_W, n_lat)
        s_lat = jnp.einsum('bqhd,bkhd->bhqk', q_r, k_r).astype(jnp.float32) * scale + bias.astype(jnp.float32)
        s_lat = jnp.where(mask, s_lat, NEG_INF)
        s_ctx = jnp.einsum('bqhd,bkhd->bhqk', q_r, kc).astype(jnp.float32) * scale
        p = jax.nn.softmax(jnp.concatenate([s_lat, s_ctx], axis=-1), axis=-1).astype(v.dtype)
        return (jnp.einsum('bhqk,bkhd->bqhd', p[..., :n_lat], v_r)
                + jnp.einsum('bhqk,bkhd->bqhd', p[..., n_lat:], vc))

    out = lax.map(row_block, jnp.arange(rows))
    return out.transpose(1, 0, 2, 3, 4).reshape(B, T, H * d)


def na_mixer(z_lat, z_ctx, rpb, need_ctx):
    B = z_lat.shape[0]
    ql, kl, vl = [t.reshape(B, -1, NA_HEADS, HEAD_DIM) for t in jnp.split(z_lat, 3, axis=-1)]
    qc, kc, vc = [t.reshape(B, -1, NA_HEADS, HEAD_DIM) for t in jnp.split(z_ctx, 3, axis=-1)]
    y_lat = neighbourhood_attention(ql, kl, vl, kc, vc, rpb)
    y_ctx = softmax_attention(qc, kc, vc, HEAD_DIM ** -0.5).reshape(B, -1, GROUP_W) if need_ctx else None
    return y_lat, y_ctx


def mla_project(z, q_norm, kv_norm, w_uq, w_ukv, cos, sin, rotary):
    B, T, _ = z.shape
    cq, ckv, k_rope = jnp.split(z, (MLA_Q_RANK, MLA_Q_RANK + MLA_KV_RANK), axis=-1)
    q = (rmsnorm(cq, q_norm) @ w_uq).reshape(B, T, MLA_HEADS, MLA_NOPE + ROPE_DIM)
    kv = (rmsnorm(ckv, kv_norm) @ w_ukv).reshape(B, T, MLA_HEADS, MLA_NOPE + MLA_V)
    q_nope, q_rope = jnp.split(q, (MLA_NOPE,), axis=-1)
    k_nope, v = jnp.split(kv, (MLA_NOPE,), axis=-1)
    if rotary:
        q_rope = apply_axial_rope(q_rope, cos, sin)
        k_rope = apply_axial_rope(k_rope, cos, sin)
    k_rope = jnp.broadcast_to(k_rope[:, :, None, :], (B, T, MLA_HEADS, ROPE_DIM))
    return (jnp.concatenate([q_nope, q_rope], axis=-1),
            jnp.concatenate([k_nope, k_rope], axis=-1), v)


def mla_mixer(z_lat, z_ctx, cos, sin, q_norm, kv_norm, w_uq, w_ukv, need_ctx):
    B, T, _ = z_lat.shape
    ql, kl, vl = mla_project(z_lat, q_norm, kv_norm, w_uq, w_ukv, cos, sin, True)
    qc, kc, vc = mla_project(z_ctx, q_norm, kv_norm, w_uq, w_ukv, cos, sin, False)
    scale = (MLA_NOPE + ROPE_DIM) ** -0.5
    k_all = jnp.concatenate([kl, kc], axis=1)
    v_all = jnp.concatenate([vl, vc], axis=1)
    y_lat = map_query_blocks(lambda qi: softmax_attention(qi, k_all, v_all, scale), ql).reshape(B, T, GROUP_W)
    y_ctx = softmax_attention(qc, kc, vc, scale).reshape(B, -1, GROUP_W) if need_ctx else None
    return y_lat, y_ctx


def rwkv7_prepare(z, mu, w0, w_up, a0, a_up, g_up, k_k, k_a):
    B, T, _ = z.shape
    zs = z + mu[0] * (shift_prev(z) - z) + mu[1] * (shift_next(z) - z)
    r, k, v, wd, ad, gd = jnp.split(zs, RWKV_SPLITS, axis=-1)

    def heads(t):
        return t.reshape(B, T, RWKV_HEADS, RWKV_N)

    kk = heads(k * k_k).astype(jnp.float32)
    kk = (kk * lax.rsqrt(jnp.maximum(jnp.sum(kk * kk, axis=-1, keepdims=True), 1e-24))).astype(z.dtype)
    g = jax.nn.sigmoid(gd) @ g_up
    per_dir = []
    for d in range(2):
        w = -jax.nn.softplus(-(w0[d] + jnp.tanh(wd) @ w_up[d])) - 0.5
        decay = jnp.exp(-jnp.exp(w.astype(jnp.float32)))
        a = jax.nn.sigmoid(a0[d] + ad @ a_up[d])
        kd = k * (1.0 + (a - 1.0) * k_a)
        per_dir.append((heads(decay), heads(kd), -kk, kk * heads(a)))
    return heads(r), heads(k), heads(v), g, per_dir


def rwkv7_scan(r, decay, k, v, a, b, s0, reverse):
    def step(S, inp):
        r_t, w_t, k_t, v_t, a_t, b_t = inp
        sa = jnp.einsum('bhij,bhj->bhi', S, a_t)
        S = S * w_t[:, :, None, :] + sa[..., None] * b_t[:, :, None, :] + v_t[..., None] * k_t[:, :, None, :]
        return S, jnp.einsum('bhij,bhj->bhi', S, r_t)

    xs = tuple(jnp.moveaxis(t.astype(jnp.float32), 1, 0) for t in (r, decay, k, v, a, b))
    s_final, y = lax.scan(step, s0, xs, reverse=reverse)
    return jnp.moveaxis(y, 0, 1), s_final


def rwkv7_output(y, r, k, v, g, r_k, ln_w, ln_b):
    B, T = y.shape[0], y.shape[1]
    mean = jnp.mean(y, axis=-1, keepdims=True)
    var = jnp.mean(jnp.square(y - mean), axis=-1, keepdims=True)
    yn = ((y - mean) * lax.rsqrt(var + RWKV_LN_EPS)).reshape(B, T, GROUP_W).astype(v.dtype) * ln_w + ln_b
    bonus = (jnp.sum(r * k * r_k, axis=-1, keepdims=True) * v).reshape(B, T, GROUP_W)
    return (yn + bonus) * g


def rwkv7_mixer(z_lat, z_ctx, mu, w0, w_up, a0, a_up, g_up, k_k, k_a, r_k, ln_w, ln_b, need_ctx):
    B = z_lat.shape[0]
    rl, kl, vl, gl, dirs_l = rwkv7_prepare(z_lat, mu, w0, w_up, a0, a_up, g_up, k_k, k_a)
    rc, kc, vc, gc, dirs_c = rwkv7_prepare(z_ctx, mu, w0, w_up, a0, a_up, g_up, k_k, k_a)
    s0 = jnp.zeros((B, RWKV_HEADS, RWKV_N, RWKV_N), jnp.float32)
    ys_lat, ys_ctx = [], []
    for d in range(2):
        reverse = d == 1
        dec_c, kd_c, a_c, b_c = dirs_c[d]
        dec_l, kd_l, a_l, b_l = dirs_l[d]
        y_c, s_ctx = rwkv7_scan(rc, dec_c, kd_c, vc, a_c, b_c, s0, reverse)
        y_l, _ = rwkv7_scan(rl, dec_l, kd_l, vl, a_l, b_l, s_ctx, reverse)
        ys_lat.append(y_l)
        ys_ctx.append(y_c)
    y_lat = rwkv7_output(ys_lat[0] + ys_lat[1], rl, kl, vl, gl, r_k, ln_w, ln_b)
    y_ctx = rwkv7_output(ys_ctx[0] + ys_ctx[1], rc, kc, vc, gc, r_k, ln_w, ln_b) if need_ctx else None
    return y_lat, y_ctx


def diff_attention(q, k, v, lam, scale):
    s = jnp.einsum('bqhnd,bkhnd->bhnqk', q, k).astype(jnp.float32) * scale
    p = jax.nn.softmax(s, axis=-1)
    p = p[:, :, 0] - lam * p[:, :, 1]
    return jnp.einsum('bhqk,bkhd->bqhd', p.astype(v.dtype), v)


def diff_mixer(z_lat, z_ctx, cos, sin, lam_p, subln, lam_init, need_ctx):
    B = z_lat.shape[0]

    def split_heads(z):
        T = z.shape[1]
        q, k, v = jnp.split(z, 3, axis=-1)
        return (q.reshape(B, T, DIFF_HEADS, 2, DIFF_QK), k.reshape(B, T, DIFF_HEADS, 2, DIFF_QK),
                v.reshape(B, T, DIFF_HEADS, DIFF_V))

    ql, kl, vl = split_heads(z_lat)
    qc, kc, vc = split_heads(z_ctx)
    ql = apply_axial_rope(ql, cos, sin)
    kl = apply_axial_rope(kl, cos, sin)
    lp = lam_p.astype(jnp.float32)
    lam = jnp.exp(jnp.sum(lp[0] * lp[1])) - jnp.exp(jnp.sum(lp[2] * lp[3])) + lam_init
    scale = DIFF_QK ** -0.5
    k_all = jnp.concatenate([kl, kc], axis=1)
    v_all = jnp.concatenate([vl, vc], axis=1)
    o_lat = map_query_blocks(lambda qi: diff_attention(qi, k_all, v_all, lam, scale), ql)

    def finish(o):
        return (rmsnorm(o, subln, DIFF_LN_EPS) * (1.0 - lam_init)).reshape(B, o.shape[1], GROUP_W)

    y_ctx = finish(diff_attention(qc, kc, vc, lam, scale)) if need_ctx else None
    return finish(o_lat), y_ctx


def conv_glu(h, w_up, conv_w, conv_b, w_down):
    a, b = jnp.split(h @ w_up, 2, axis=-1)
    a = conv_w[0] * shift_prev(a) + conv_w[1] * a + conv_w[2] * shift_next(a) + conv_b
    return (jax.nn.silu(a) * b) @ w_down


def setup_inputs(seed: int = 0) -> dict:
    key = jax.random.key(seed)
    keys = jax.random.split(key, 40)
    L, D = DEPTH, D_MODEL

    def nrm(i, shape, scale):
        return jax.random.normal(keys[i], shape, jnp.float32) * scale

    return {
        'x': nrm(0, (BATCH, SEQ, D), 1.0),
        'c': nrm(1, (BATCH, D), 1.0),
        'ctx': nrm(2, (BATCH, CTX_LEN, D), 1.0),
        'c_ctx': nrm(3, (D,), 1.0),
        'norm1_g': 1.0 + nrm(4, (L, D), 0.05),
        'norm2_g': 1.0 + nrm(5, (L, D), 0.05),
        'ada_w': nrm(6, (L, D, 6 * D), 0.3 * D ** -0.5),
        'ada_b': nrm(7, (L, 6 * D), 0.02),
        'w_in': nrm(8, (L, D, IN_COLS), D ** -0.5),
        'w_out': nrm(9, (L, MIX_W, D), MIX_W ** -0.5),
        'na_rpb': nrm(10, (L, NA_HEADS, 2 * NA_KH - 1, 2 * NA_KW - 1), 0.5),
        'mla_q_norm': 1.0 + nrm(11, (L, MLA_Q_RANK), 0.05),
        'mla_kv_norm': 1.0 + nrm(12, (L, MLA_KV_RANK), 0.05),
        'mla_w_uq': nrm(13, (L, MLA_Q_RANK, MLA_HEADS * (MLA_NOPE + ROPE_DIM)), MLA_Q_RANK ** -0.5),
        'mla_w_ukv': nrm(14, (L, MLA_KV_RANK, MLA_HEADS * (MLA_NOPE + MLA_V)), MLA_KV_RANK ** -0.5),
        'rwkv_mu': jax.random.uniform(keys[15], (L, 2, RWKV_COLS), jnp.float32, 0.0, 0.5),
        'rwkv_w0': -2.0 + nrm(16, (L, 2, GROUP_W), 0.5),
        'rwkv_w_up': nrm(17, (L, 2, RWKV_W_RANK, GROUP_W), RWKV_W_RANK ** -0.5),
        'rwkv_a0': nrm(18, (L, 2, GROUP_W), 0.5),
        'rwkv_a_up': nrm(19, (L, 2, RWKV_A_RANK, GROUP_W), RWKV_A_RANK ** -0.5),
        'rwkv_g_up': nrm(20, (L, RWKV_G_RANK, GROUP_W), RWKV_G_RANK ** -0.5),
        'rwkv_k_k': 0.85 + nrm(21, (L, GROUP_W), 0.05),
        'rwkv_k_a': 1.0 + nrm(22, (L, GROUP_W), 0.05),
        'rwkv_r_k': nrm(23, (L, RWKV_HEADS, RWKV_N), 0.1),
        'rwkv_ln_w': 1.0 + nrm(24, (L, GROUP_W), 0.05),
        'rwkv_ln_b': nrm(25, (L, GROUP_W), 0.02),
        'diff_lambda': nrm(26, (L, 4, DIFF_QK), 0.1),
        'diff_subln': 1.0 + nrm(27, (L, DIFF_V), 0.05),
        'mlp_w_up': nrm(28, (L, D, 2 * D_FF), D ** -0.5),
        'mlp_conv_w': nrm(29, (L, 3, D_FF), 3 ** -0.5),
        'mlp_conv_b': nrm(30, (L, D_FF), 0.02),
        'mlp_w_down': nrm(31, (L, D_FF, D), D_FF ** -0.5),
        'final_norm_g': 1.0 + nrm(32, (D,), 0.05),
    }


def reference(x, c, ctx, c_ctx, norm1_g, norm2_g, ada_w, ada_b, w_in, w_out, na_rpb,
              mla_q_norm, mla_kv_norm, mla_w_uq, mla_w_ukv,
              rwkv_mu, rwkv_w0, rwkv_w_up, rwkv_a0, rwkv_a_up, rwkv_g_up, rwkv_k_k, rwkv_k_a,
              rwkv_r_k, rwkv_ln_w, rwkv_ln_b, diff_lambda, diff_subln,
              mlp_w_up, mlp_conv_w, mlp_conv_b, mlp_w_down, final_norm_g):
    n_lat = x.shape[1]
    cos, sin = axial_rope_tables(n_lat, x.dtype)
    s_lat = jax.nn.silu(c)
    s_ctx = jax.nn.silu(c_ctx)[None]
    xl, xc = x, ctx
    for l in range(DEPTH):
        need_ctx = l < DEPTH - 1
        sh1, sc1, g1, sh2, sc2, g2 = jnp.split((s_lat @ ada_w[l] + ada_b[l])[:, None, :], 6, axis=-1)
        csh1, csc1, cg1, csh2, csc2, cg2 = jnp.split((s_ctx @ ada_w[l] + ada_b[l])[:, None, :], 6, axis=-1)
        h_lat = rmsnorm(xl, norm1_g[l]) * (1.0 + sc1) + sh1
        h_ctx = rmsnorm(xc, norm1_g[l]) * (1.0 + csc1) + csh1
        z_lat = jnp.split(h_lat @ w_in[l], IN_SPLITS, axis=-1)
        z_ctx = jnp.split(h_ctx @ w_in[l], IN_SPLITS, axis=-1)
        na_l, na_c = na_mixer(z_lat[0], z_ctx[0], na_rpb[l], need_ctx)
        mla_l, mla_c = mla_mixer(z_lat[1], z_ctx[1], cos, sin, mla_q_norm[l], mla_kv_norm[l],
                                 mla_w_uq[l], mla_w_ukv[l], need_ctx)
        rk_l, rk_c = rwkv7_mixer(z_lat[2], z_ctx[2], rwkv_mu[l], rwkv_w0[l], rwkv_w_up[l], rwkv_a0[l],
                                 rwkv_a_up[l], rwkv_g_up[l], rwkv_k_k[l], rwkv_k_a[l], rwkv_r_k[l],
                                 rwkv_ln_w[l], rwkv_ln_b[l], need_ctx)
        lam_init = 0.8 - 0.6 * math.exp(-0.3 * l)
        df_l, df_c = diff_mixer(z_lat[3], z_ctx[3], cos, sin, diff_lambda[l], diff_subln[l], lam_init, need_ctx)
        xl = xl + g1 * (jnp.concatenate([na_l, mla_l, rk_l, df_l], axis=-1) @ w_out[l])
        h2 = rmsnorm(xl, norm2_g[l]) * (1.0 + sc2) + sh2
        xl = xl + g2 * conv_glu(h2, mlp_w_up[l], mlp_conv_w[l], mlp_conv_b[l], mlp_w_down[l])
        if need_ctx:
            xc = xc + cg1 * (jnp.concatenate([na_c, mla_c, rk_c, df_c], axis=-1) @ w_out[l])
            hc2 = rmsnorm(xc, norm2_g[l]) * (1.0 + csc2) + csh2
            xc = xc + cg2 * conv_glu(hc2, mlp_w_up[l], mlp_conv_w[l], mlp_conv_b[l], mlp_w_down[l])
    return rmsnorm(xl, final_norm_g)
```

```python
import functools
import math

import jax
import jax.numpy as jnp
import numpy as np
from jax import lax
from jax.experimental import pallas as pl
from jax.experimental.pallas import tpu as pltpu

F32 = jnp.float32
BF16 = jnp.bfloat16

GRID_W = 64
GROUP_W = 256
HEAD_DIM = 64
N_HEADS = 4
ROPE_DIM = 32
ROPE_THETA = 10000.0
NORM_EPS = 1e-6
NEG_INF = -1e30
NA_KH = 8
NA_KW = 16
MLA_Q_RANK = 256
MLA_KV_RANK = 128
RWKV_COLS = 896
RWKV_LN_EPS = 64e-5
DIFF_QK = 32
DIFF_LN_EPS = 1e-5
IN_SPLITS = (768, 1184, 2080, 2848)

W_NA = 0
W_DIFF = 768
W_RWKV = 2048
W_MLA = 2944
W_TOT = 3456

TOKEN_TILE = 256
CHUNK = 64
HALO = 8
VMEM_LIMIT = 56 * 1024 * 1024


def _cparams(sem):
    return pltpu.CompilerParams(dimension_semantics=sem, vmem_limit_bytes=VMEM_LIMIT)


def _dot(a, b):
    return jnp.dot(a.astype(BF16), b.astype(BF16), preferred_element_type=F32)


def _dot_nt(a, b):
    return lax.dot_general(a.astype(BF16), b.astype(BF16), (((1,), (1,)), ((), ())),
                           preferred_element_type=F32)


def _split2(x):
    hi = x.astype(BF16)
    lo = (x - hi.astype(F32)).astype(BF16)
    return hi, lo


def _split3(x):
    h1 = x.astype(BF16)
    r1 = x - h1.astype(F32)
    h2 = r1.astype(BF16)
    h3 = (r1 - h2.astype(F32)).astype(BF16)
    return h1, h2, h3


def _dot3(a, b):
    ah, al = _split2(a)
    bh, bl = _split2(b)
    d = functools.partial(jnp.dot, preferred_element_type=F32)
    return d(ah, bh) + (d(ah, bl) + d(al, bh))


def _sigmoid(x):
    return 1.0 / (1.0 + jnp.exp(-x))


def _softplus(x):
    return jnp.maximum(x, 0.0) + jnp.log(1.0 + jnp.exp(-jnp.abs(x)))


def _rms(x, g, eps):
    return x * lax.rsqrt(jnp.mean(x * x, axis=-1, keepdims=True) + eps) * g


def _head_ones(n=GROUP_W):
    r = lax.broadcasted_iota(jnp.int32, (n, n), 0) >> 6
    c = lax.broadcasted_iota(jnp.int32, (n, n), 1) >> 6
    return jnp.where(r == c, 1.0, 0.0).astype(BF16)


def _head_sum(x, ones):
    hi, lo = _split2(x)
    d = functools.partial(jnp.dot, preferred_element_type=F32)
    return d(hi, ones) + d(lo, ones)


def _bd_mask():
    r = lax.broadcasted_iota(jnp.int32, (GROUP_W, GROUP_W), 0) >> 6
    c = lax.broadcasted_iota(jnp.int32, (GROUP_W, GROUP_W), 1) >> 6
    return r == c


def _bd(x, mask):
    xb = x.astype(BF16)
    return jnp.where(mask, jnp.concatenate([xb, xb, xb, xb], axis=0), jnp.zeros((), BF16))


def _mod_kernel(s_ref, w_ref, b_ref, o_ref):
    s = s_ref[...]
    s = s * _sigmoid(s)
    o_ref[0] = _dot3(s, w_ref[0]) + b_ref[0]


def _modulation(cc, ada_w, ada_b):
    L, D, N = ada_w.shape
    R = cc.shape[0]
    tn = 1536
    return pl.pallas_call(
        _mod_kernel,
        grid=(L, N // tn),
        in_specs=[pl.BlockSpec((R, D), lambda l, n: (0, 0)),
                  pl.BlockSpec((1, D, tn), lambda l, n: (l, 0, n)),
                  pl.BlockSpec((1, 1, tn), lambda l, n: (l, 0, n))],
        out_specs=pl.BlockSpec((1, R, tn), lambda l, n: (l, 0, n)),
        out_shape=jax.ShapeDtypeStruct((L, R, N), F32),
        compiler_params=_cparams(("parallel", "parallel")),
        name="adaln_mod",
    )(cc, ada_w, ada_b.reshape(L, 1, N))


def _in_kernel(x_ref, mod_ref, g_ref, w_ref, cos_ref, sin_ref,
               na_ref, dq_ref, dk_ref, dv_ref, rw_ref, mla_ref):
    x = x_ref[0]
    D = x.shape[-1]
    m = mod_ref[0]
    h = _rms(x, g_ref[...], NORM_EPS) * (1.0 + m[:, D:2 * D]) + m[:, 0:D]
    hb = h.astype(BF16)
    dot = functools.partial(jnp.dot, preferred_element_type=F32)
    na_ref[0] = dot(hb, w_ref[:, W_NA:W_DIFF]).astype(BF16)
    d = dot(hb, w_ref[:, W_DIFF:W_RWKV])
    cos = cos_ref[...]
    sin = sin_ref[...]
    scale = DIFF_QK ** -0.5
    dq_ref[0] = ((d[:, 0:256] * cos + d[:, 256:512] * sin) * scale).astype(BF16)
    dk_ref[0] = (d[:, 512:768] * cos + d[:, 768:1024] * sin).astype(BF16)
    dv_ref[0] = d[:, 1024:1280].astype(BF16)
    rw_ref[0] = dot(hb, w_ref[:, W_RWKV:W_MLA])
    mla_ref[0] = dot(hb, w_ref[:, W_MLA:W_TOT])


def _in_proj(X, mod2, g, w, cos256, sin256, tm, nct):
    B, Ta, D = X.shape
    nt = Ta // tm
    tok = lambda n: pl.BlockSpec((1, tm, n), lambda j, b: (b, j, 0))
    return pl.pallas_call(
        _in_kernel,
        grid=(nt, B),
        in_specs=[tok(D),
                  pl.BlockSpec((1, 1, 6 * D), lambda j, b: (2 * b + jnp.where(j >= nct, 1, 0), 0, 0)),
                  pl.BlockSpec((1, D), lambda j, b: (0, 0)),
                  pl.BlockSpec((D, W_TOT), lambda j, b: (0, 0)),
                  pl.BlockSpec((tm, 256), lambda j, b: (j, 0)),
                  pl.BlockSpec((tm, 256), lambda j, b: (j, 0))],
        out_specs=[tok(768), tok(256), tok(256), tok(256), tok(RWKV_COLS), tok(512)],
        out_shape=[jax.ShapeDtypeStruct((B, Ta, 768), BF16),
                   jax.ShapeDtypeStruct((B, Ta, 256), BF16),
                   jax.ShapeDtypeStruct((B, Ta, 256), BF16),
                   jax.ShapeDtypeStruct((B, Ta, 256), BF16),
                   jax.ShapeDtypeStruct((B, Ta, RWKV_COLS), F32),
                   jax.ShapeDtypeStruct((B, Ta, 512), F32)],
        compiler_params=_cparams(("parallel", "parallel")),
        name="in_proj",
    )(X, mod2, g, w, cos256, sin256)


def _mla_prep_kernel(z_ref, cos_ref, sin_ref, qn_ref, kvn_ref, wqm_ref, wqr_ref, wk_ref, wv_ref, p_ref,
                     q_ref, k_ref, v_ref):
    z = z_ref[0]
    cos = cos_ref[...]
    sin = sin_ref[...]
    dot = functools.partial(jnp.dot, preferred_element_type=F32)
    nq = _rms(z[:, 0:256], qn_ref[...], NORM_EPS).astype(BF16)
    scale = (HEAD_DIM + ROPE_DIM) ** -0.5
    q = (dot(nq, wqm_ref[...]) * cos + dot(nq, wqr_ref[...]) * sin) * scale
    q_ref[0] = q.astype(BF16)
    nkv = _rms(z[:, 256:384], kvn_ref[...], NORM_EPS).astype(BF16)
    kr = dot(z[:, 384:512].astype(BF16), p_ref[...])
    k = dot(nkv, wk_ref[...]) + kr[:, 0:512] * cos + kr[:, 512:1024] * sin
    k_ref[0] = k.astype(BF16)
    v_ref[0] = dot(nkv, wv_ref[...]).astype(BF16)


def _mla_prep(z, cos512, sin512, qn, kvn, wqm, wqr, wk, wv, pmat, tm):
    B, Ta, _ = z.shape
    nt = Ta // tm
    tok = lambda n: pl.BlockSpec((1, tm, n), lambda j, b: (b, j, 0))
    full = lambda a: pl.BlockSpec(a.shape, lambda j, b: (0,) * a.ndim)
    return pl.pallas_call(
        _mla_prep_kernel,
        grid=(nt, B),
        in_specs=[tok(512),
                  pl.BlockSpec((tm, 512), lambda j, b: (j, 0)),
                  pl.BlockSpec((tm, 512), lambda j, b: (j, 0)),
                  full(qn), full(kvn), full(wqm), full(wqr), full(wk), full(wv), full(pmat)],
        out_specs=[tok(512), tok(512), tok(256)],
        out_shape=[jax.ShapeDtypeStruct((B, Ta, 512), BF16),
                   jax.ShapeDtypeStruct((B, Ta, 512), BF16),
                   jax.ShapeDtypeStruct((B, Ta, 256), BF16)],
        compiler_params=_cparams(("parallel", "parallel")),
        name="mla_prep",
    )(z, cos512, sin512, qn, kvn, wqm, wqr, wk, wv, pmat)


def _softmax_parts(s):
    m = jnp.max(s, axis=-1, keepdims=True)
    e = jnp.exp(s - m)
    return e, jnp.sum(e, axis=-1, keepdims=True)


def _mla_attn_body(q, k_ref, v_ref, nk):
    tq = q.shape[0]
    lane = lax.broadcasted_iota(jnp.int32, (tq, 128), 1)
    outs = []
    for pair in range(2):
        vp = v_ref[0, 0:nk, pair * 128:(pair + 1) * 128]
        o2 = []
        for hh in range(2):
            h = 2 * pair + hh
            s = _dot_nt(q[:, h * 128:(h + 1) * 128], k_ref[0, 0:nk, h * 128:(h + 1) * 128])
            e, l = _softmax_parts(s)
            o2.append(jnp.dot(e.astype(BF16), vp, preferred_element_type=F32) * (1.0 / l))
        outs.append(jnp.where(lane < 64, o2[0], o2[1]))
    return jnp.concatenate(outs, axis=1)


def _mla_attn_kernel(q_ref, k_ref, v_ref, o_ref, *, j0, nct, C, Ta):
    j = pl.program_id(1) + j0
    q = q_ref[0]
    if j0 < nct:
        @pl.when(j < nct)
        def _():
            o_ref[0] = _mla_attn_body(q, k_ref, v_ref, C).astype(BF16)

    @pl.when(j >= nct)
    def _():
        o_ref[0] = _mla_attn_body(q, k_ref, v_ref, Ta).astype(BF16)


def _attn_call(kern, name, q, k, v, extra, extra_specs, tq, j0, out_w=256):
    B, Ta, _ = q.shape
    nq = Ta // tq - j0
    return pl.pallas_call(
        kern,
        grid=(B, nq),
        in_specs=[pl.BlockSpec((1, tq, q.shape[-1]), lambda b, j: (b, j + j0, 0)),
                  pl.BlockSpec((1, Ta, k.shape[-1]), lambda b, j: (b, 0, 0)),
                  pl.BlockSpec((1, Ta, v.shape[-1]), lambda b, j: (b, 0, 0))] + extra_specs,
        out_specs=pl.BlockSpec((1, tq, out_w), lambda b, j: (b, j + j0, 0)),
        out_shape=jax.ShapeDtypeStruct((B, Ta, out_w), BF16),
        compiler_params=_cparams(("parallel", "arbitrary")),
        name=name,
    )(q, k, v, *extra)


def _diff_body(q, k_ref, v_ref, nk, lam):
    tq = q.shape[0]
    k = k_ref[0, 0:nk, :]
    sub = lax.broadcasted_iota(jnp.int32, (tq, GROUP_W), 1) >> 5
    lane = lax.broadcasted_iota(jnp.int32, (tq, 128), 1)
    zero = jnp.zeros((), BF16)
    outs = []
    for pair in range(2):
        vp = v_ref[0, 0:nk, pair * 128:(pair + 1) * 128]
        o2 = []
        for hh in range(2):
            h = 2 * pair + hh
            e1, l1 = _softmax_parts(_dot_nt(jnp.where(sub == 2 * h, q, zero), k))
            e2, l2 = _softmax_parts(_dot_nt(jnp.where(sub == 2 * h + 1, q, zero), k))
            p = e1 * (1.0 / l1) - e2 * (lam / l2)
            o2.append(jnp.dot(p.astype(BF16), vp, preferred_element_type=F32))
        outs.append(jnp.where(lane < 64, o2[0], o2[1]))
    return jnp.concatenate(outs, axis=1)


def _diff_attn_kernel(q_ref, k_ref, v_ref, lam_ref, sub_ref, o_ref, *, j0, nct, C, Ta, lam_init):
    j = pl.program_id(1) + j0
    q = q_ref[0]
    lp = lam_ref[...]
    lam = (jnp.exp(jnp.sum(lp[0:1] * lp[1:2], axis=-1, keepdims=True))
           - jnp.exp(jnp.sum(lp[2:3] * lp[3:4], axis=-1, keepdims=True)) + lam_init)
    ones = _head_ones()

    def finish(o):
        ms = _head_sum(o * o, ones) * (1.0 / HEAD_DIM)
        return (o * lax.rsqrt(ms + DIFF_LN_EPS) * sub_ref[...]) * (1.0 - lam_init)

    if j0 < nct:
        @pl.when(j < nct)
        def _():
            o_ref[0] = finish(_diff_body(q, k_ref, v_ref, C, lam)).astype(BF16)

    @pl.when(j >= nct)
    def _():
        o_ref[0] = finish(_diff_body(q, k_ref, v_ref, Ta, lam)).astype(BF16)


def _na_kernel(qt_ref, kv_ref, bias_ref, o_ref, *, j0, nct, C, rows, R):
    j = pl.program_id(1) + j0
    tq = R * GRID_W
    zero = jnp.zeros((), BF16)

    if j0 < nct:
        @pl.when(j < nct)
        def _():
            lane = lax.broadcasted_iota(jnp.int32, (tq, 128), 1)
            outs = []
            for pair in range(2):
                q2 = qt_ref[0, :, pair * 128:(pair + 1) * 128]
                kc = kv_ref[0, 0:C, 256 + pair * 128:256 + (pair + 1) * 128]
                vc = kv_ref[0, 0:C, 512 + pair * 128:512 + (pair + 1) * 128]
                o2 = []
                for hh in range(2):
                    qm = jnp.where((lane < 64) if hh == 0 else (lane >= 64), q2, zero)
                    e, l = _softmax_parts(_dot_nt(qm, kc))
                    o2.append(jnp.dot(e.astype(BF16), vc, preferred_element_type=F32) * (1.0 / l))
                outs.append(jnp.where(lane < 64, o2[0], o2[1]))
            o_ref[0] = jnp.concatenate(outs, axis=1).astype(BF16)

    @pl.when(j >= nct)
    def _():
        lane = lax.broadcasted_iota(jnp.int32, (GRID_W, 128), 1)
        for rr in range(R):
            r = (j - nct) * R + rr
            rs = jnp.clip(r - NA_KH // 2, 0, rows - NA_KH)
            delta = r - rs
            start = pl.multiple_of(C + rs * GRID_W, GRID_W)
            outs = []
            for pair in range(2):
                q2 = qt_ref[0, rr * GRID_W:(rr + 1) * GRID_W, pair * 128:(pair + 1) * 128]
                kw = kv_ref[0, pl.ds(start, NA_KH * GRID_W), 256 + pair * 128:256 + (pair + 1) * 128]
                vw = kv_ref[0, pl.ds(start, NA_KH * GRID_W), 512 + pair * 128:512 + (pair + 1) * 128]
                kc = kv_ref[0, 0:C, 256 + pair * 128:256 + (pair + 1) * 128]
                vc = kv_ref[0, 0:C, 512 + pair * 128:512 + (pair + 1) * 128]
                o2 = []
                for hh in range(2):
                    qm = jnp.where((lane < 64) if hh == 0 else (lane >= 64), q2, zero)
                    s_lat = _dot_nt(qm, kw) + bias_ref[delta, 2 * pair + hh]
                    s_ctx = _dot_nt(qm, kc)
                    m = jnp.maximum(jnp.max(s_lat, axis=-1, keepdims=True),
                                    jnp.max(s_ctx, axis=-1, keepdims=True))
                    e_lat = jnp.exp(s_lat - m)
                    e_ctx = jnp.exp(s_ctx - m)
                    l = jnp.sum(e_lat, axis=-1, keepdims=True) + jnp.sum(e_ctx, axis=-1, keepdims=True)
                    o = (jnp.dot(e_lat.astype(BF16), vw, preferred_element_type=F32)
                         + jnp.dot(e_ctx.astype(BF16), vc, preferred_element_type=F32))
                    o2.append(o * (1.0 / l))
                outs.append(jnp.where(lane < 64, o2[0], o2[1]))
            o_ref[0, rr * GRID_W:(rr + 1) * GRID_W, :] = jnp.concatenate(outs, axis=1).astype(BF16)


def _na_attn(qkv, bias, tq, j0, nct, C, rows):
    B, Ta, _ = qkv.shape
    nq = Ta // tq - j0
    kern = functools.partial(_na_kernel, j0=j0, nct=nct, C=C, rows=rows, R=tq // GRID_W)
    return pl.pallas_call(
        kern,
        grid=(B, nq),
        in_specs=[pl.BlockSpec((1, tq, 768), lambda b, j: (b, j + j0, 0)),
                  pl.BlockSpec((1, Ta, 768), lambda b, j: (b, 0, 0)),
                  pl.BlockSpec(bias.shape, lambda b, j: (0, 0, 0, 0))],
        out_specs=pl.BlockSpec((1, tq, 256), lambda b, j: (b, j + j0, 0)),
        out_shape=jax.ShapeDtypeStruct((B, Ta, 256), BF16),
        compiler_params=_cparams(("parallel", "arbitrary")),
        name="na_attn",
    )(qkv, qkv, bias)


def _rwkv_prep_kernel(z_ref, zp_ref, zn_ref, mu_ref, w0_ref, wup_ref, a0_ref, aup_ref, gup_ref,
                      kk_ref, ka_ref, rk_ref, rvk_ref, gb_ref, dir_ref, *, nct, nt):
    j = pl.program_id(0)
    tm = z_ref.shape[1]
    firstf = jnp.where((j == 0) | (j == nct), 0.0, 1.0)
    lastf = jnp.where((j == nct - 1) | (j == nt - 1), 0.0, 1.0)
    z = z_ref[0]
    row = lax.broadcasted_iota(jnp.int32, (tm, 1), 0)
    prev = jnp.where(row == 0, zp_ref[0, HALO - 1:HALO, :] * firstf, pltpu.roll(z, 1, axis=0))
    nxt = jnp.where(row == tm - 1, zn_ref[0, 0:1, :] * lastf, pltpu.roll(z, tm - 1, axis=0))
    zs = z + mu_ref[0:1, :] * (prev - z) + mu_ref[1:2, :] * (nxt - z)
    r = zs[:, 0:256]
    k = zs[:, 256:512]
    v = zs[:, 512:768]
    low = zs[:, 768:896]
    ones = _head_ones()
    kk = k * kk_ref[...]
    kk = kk * lax.rsqrt(jnp.maximum(_head_sum(kk * kk, ones), 1e-24))
    g = _dot(_sigmoid(low), gup_ref[...])
    bonus = _head_sum(r * k * rk_ref[...], ones) * v
    rvk_ref[0, :, 0:256] = r
    rvk_ref[0, :, 256:512] = v
    rvk_ref[0, :, 512:768] = kk
    gb_ref[0, :, 0:256] = g
    gb_ref[0, :, 256:512] = bonus
    tl = jnp.tanh(low).astype(BF16)
    lb = low.astype(BF16)
    for d in range(2):
        w = -_softplus(-(w0_ref[d:d + 1, :] + jnp.dot(tl, wup_ref[d], preferred_element_type=F32))) - 0.5
        a = _sigmoid(a0_ref[d:d + 1, :] + jnp.dot(lb, aup_ref[d], preferred_element_type=F32))
        dir_ref[d, 0, :, 0:256] = -jnp.exp(w)
        dir_ref[d, 0, :, 256:512] = k * (1.0 + (a - 1.0) * ka_ref[...])
        dir_ref[d, 0, :, 512:768] = kk * a


def _rwkv_prep(z, mu, w0, wup_p, a0, aup_p, gup_p, k_k, k_a, r_k, tm, nct):
    B, Ta, _ = z.shape
    nt = Ta // tm
    hb = tm // HALO
    nh = Ta // HALO
    tok = lambda n: pl.BlockSpec((1, tm, n), lambda j, b: (b, j, 0))
    full = lambda a: pl.BlockSpec(a.shape, lambda j, b: (0,) * a.ndim)
    kern = functools.partial(_rwkv_prep_kernel, nct=nct, nt=nt)
    return pl.pallas_call(
        kern,
        grid=(nt, B),
        in_specs=[tok(RWKV_COLS),
                  pl.BlockSpec((1, HALO, RWKV_COLS), lambda j, b: (b, jnp.maximum(j * hb - 1, 0), 0)),
                  pl.BlockSpec((1, HALO, RWKV_COLS), lambda j, b: (b, jnp.minimum((j + 1) * hb, nh - 1), 0)),
                  full(mu), full(w0), full(wup_p), full(a0), full(aup_p), full(gup_p),
                  full(k_k), full(k_a), full(r_k)],
        out_specs=[tok(768), tok(512),
                   pl.BlockSpec((2, 1, tm, 768), lambda j, b: (0, b, j, 0))],
        out_shape=[jax.ShapeDtypeStruct((B, Ta, 768), F32),
                   jax.ShapeDtypeStruct((B, Ta, 512), F32),
                   jax.ShapeDtypeStruct((2, B, Ta, 768), F32)],
        compiler_params=_cparams(("parallel", "parallel")),
        name="rwkv_prep",
    )(z, z, z, mu, w0, wup_p, a0, aup_p, gup_p, k_k, k_a, r_k)


def _rwkv_chunk_kernel(rvk_ref, dir_ref, o_ref):
    d = pl.program_id(0)
    nchunk = rvk_ref.shape[1] // CHUNK
    bdm = _bd_mask()
    row = lax.broadcasted_iota(jnp.int32, (CHUNK, GROUP_W), 0)
    col = lax.broadcasted_iota(jnp.int32, (CHUNK, GROUP_W), 1) & (CHUNK - 1)
    tdiff = jnp.where(d == 0, row - col, col - row)
    incl = tdiff >= 0
    strict = tdiff > 0
    eye = tdiff == 0
    r2 = lax.broadcasted_iota(jnp.int32, (CHUNK, CHUNK), 0)
    c2 = lax.broadcasted_iota(jnp.int32, (CHUNK, CHUNK), 1)
    tri = jnp.where(jnp.where(d == 0, r2 - c2, c2 - r2) >= 0, 1.0, 0.0).astype(BF16)
    dot = functools.partial(jnp.dot, preferred_element_type=F32)

    def hmul(x, y):
        return dot(x.astype(BF16), _bd(y, bdm))

    for c in range(nchunk):
        sl = slice(c * CHUNK, (c + 1) * CHUNK)
        r = rvk_ref[0, sl, 0:256]
        v = rvk_ref[0, sl, 256:512]
        kk = rvk_ref[0, sl, 512:768]
        lw = dir_ref[0, 0, sl, 0:256]
        kd = dir_ref[0, 0, sl, 256:512]
        b = dir_ref[0, 0, sl, 512:768]
        l1, l2, l3 = _split3(lw)
        cum = dot(tri, l1) + (dot(tri, l2) + dot(tri, l3))
        tot = jnp.sum(lw, axis=0, keepdims=True)
        rt = r * jnp.exp(cum)
        at = -kk * jnp.exp(cum - lw)
        einv = jnp.exp(-cum)
        bt = b * einv
        kt = kd * einv
        eend = jnp.exp(tot - cum)
        bh = b * eend
        kh = kd * eend
        ar = jnp.concatenate([at, rt], axis=0).astype(BF16)
        xb = lax.dot_general(ar, _bd(bt, bdm), (((1,), (1,)), ((), ())), preferred_element_type=F32)
        xk = lax.dot_general(ar, _bd(kt, bdm), (((1,), (1,)), ((), ())), preferred_element_type=F32)
        n = jnp.where(strict, xb[0:CHUNK], 0.0)
        a_ak = jnp.where(strict, xk[0:CHUNK], 0.0)
        lrb = jnp.where(incl, xb[CHUNK:], 0.0)
        lrk = jnp.where(incl, xk[CHUNK:], 0.0)
        p = jnp.where(eye, 1.0, 0.0) + n
        npow = hmul(n, n)
        for _ in range(4):
            sq = hmul(jnp.concatenate([npow, p], axis=0), npow)
            npow = sq[0:CHUNK]
            p = p + sq[CHUNK:]
        t = p + hmul(p, npow)
        akv_lrkv = hmul(jnp.concatenate([a_ak, lrk], axis=0), v)
        w = hmul(t, at)
        u0 = hmul(t, akv_lrkv[0:CHUNK])
        qh = rt + hmul(lrb, w)
        y0 = hmul(lrb, u0) + akv_lrkv[CHUNK:]
        bht = bh.T.astype(BF16)
        kht = kh.T.astype(BF16)
        gfull = dot(bht, w.astype(BF16))
        hfull = dot(bht, u0.astype(BF16)) + dot(kht, v.astype(BF16))
        gfull = jnp.where(bdm, gfull, 0.0)
        hfull = jnp.where(bdm, hfull, 0.0)
        g = (gfull[0:64] + gfull[64:128]) + (gfull[128:192] + gfull[192:256])
        hm = (hfull[0:64] + hfull[64:128]) + (hfull[128:192] + hfull[192:256])
        g = g + jnp.where(eye, jnp.exp(tot), 0.0)
        o_ref[0, 0, sl, 0:256] = qh
        o_ref[0, 0, sl, 256:512] = y0
        o_ref[0, 0, sl, 512:768] = g
        o_ref[0, 0, sl, 768:1024] = hm


def _rwkv_chunks(rvk, dirp, tm):
    B, Ta, _ = rvk.shape
    nt = Ta // tm
    return pl.pallas_call(
        _rwkv_chunk_kernel,
        grid=(2, B, nt),
        in_specs=[pl.BlockSpec((1, tm, 768), lambda d, b, j: (b, j, 0)),
                  pl.BlockSpec((1, 1, tm, 768), lambda d, b, j: (d, b, j, 0))],
        out_specs=pl.BlockSpec((1, 1, tm, 1024), lambda d, b, j: (d, b, j, 0)),
        out_shape=jax.ShapeDtypeStruct((2, B, Ta, 1024), F32),
        compiler_params=_cparams(("parallel", "parallel", "parallel")),
        name="rwkv_chunks",
    )(rvk, dirp)


def _rwkv_scan_kernel(f_ref, b_ref, yf_ref, yb_ref, s_ref):
    i = pl.program_id(1)
    gb = f_ref.shape[1]
    bdm = _bd_mask()

    @pl.when(i == 0)
    def _():
        s_ref[...] = jnp.zeros(s_ref.shape, F32)

    for d, (c_ref, y_ref) in enumerate(((f_ref, yf_ref), (b_ref, yb_ref))):
        for bb in range(gb):
            s = s_ref[d, bb]
            sh, sl = _split2(s)
            lhs = jnp.concatenate([c_ref[0, bb, :, 0:256], c_ref[0, bb, :, 512:768]], axis=0)
            lh, ll = _split2(lhs)
            sbh = jnp.where(bdm, jnp.concatenate([sh] * 4, axis=0), jnp.zeros((), BF16))
            sbl = jnp.where(bdm, jnp.concatenate([sl] * 4, axis=0), jnp.zeros((), BF16))
            dot = functools.partial(jnp.dot, preferred_element_type=F32)
            res = dot(lh, sbh) + (dot(lh, sbl) + dot(ll, sbh))
            y_ref[bb] = res[0:CHUNK] + c_ref[0, bb, :, 256:512]
            s_ref[d, bb] = res[CHUNK:] + c_ref[0, bb, :, 768:1024]


def _rwkv_scan(chk, ncc, gb):
    _, B, Ta, _ = chk.shape
    nc = Ta // CHUNK

    def rev_chunk(i):
        return jnp.where(i < ncc, ncc - 1 - i, nc - 1 - (i - ncc))

    return pl.pallas_call(
        _rwkv_scan_kernel,
        grid=(B // gb, nc),
        in_specs=[pl.BlockSpec((1, gb, CHUNK, 1024), lambda b, i: (0, b, i, 0)),
                  pl.BlockSpec((1, gb, CHUNK, 1024), lambda b, i: (1, b, rev_chunk(i), 0))],
        out_specs=[pl.BlockSpec((gb, CHUNK, 256), lambda b, i: (b, i, 0)),
                   pl.BlockSpec((gb, CHUNK, 256), lambda b, i: (b, rev_chunk(i), 0))],
        out_shape=[jax.ShapeDtypeStruct((B, Ta, 256), F32),
                   jax.ShapeDtypeStruct((B, Ta, 256), F32)],
        scratch_shapes=[pltpu.VMEM((2, gb, CHUNK, GROUP_W), F32)],
        compiler_params=_cparams(("parallel", "arbitrary")),
        name="rwkv_scan",
    )(chk, chk)


def _rwkv_out_kernel(yf_ref, yb_ref, gb_ref, lnw_ref, lnb_ref, o_ref):
    ones = _head_ones()
    y = yf_ref[0] + yb_ref[0]
    mean = _head_sum(y, ones) * (1.0 / HEAD_DIM)
    yc = y - mean
    var = _head_sum(yc * yc, ones) * (1.0 / HEAD_DIM)
    yn = yc * lax.rsqrt(var + RWKV_LN_EPS) * lnw_ref[...] + lnb_ref[...]
    o_ref[0] = ((yn + gb_ref[0, :, 256:512]) * gb_ref[0, :, 0:256]).astype(BF16)


def _rwkv_out(yf, yb, gbn, ln_w, ln_b, tm, j0):
    B, Ta, _ = yf.shape
    nt = Ta // tm - j0
    tok = lambda n: pl.BlockSpec((1, tm, n), lambda j, b: (b, j + j0, 0))
    full = lambda a: pl.BlockSpec(a.shape, lambda j, b: (0,) * a.ndim)
    return pl.pallas_call(
        _rwkv_out_kernel,
        grid=(nt, B),
        in_specs=[tok(256), tok(256), tok(512), full(ln_w), full(ln_b)],
        out_specs=tok(256),
        out_shape=jax.ShapeDtypeStruct((B, Ta, 256), BF16),
        compiler_params=_cparams(("parallel", "parallel")),
        name="rwkv_out",
    )(yf, yb, gbn, ln_w, ln_b)


def _out_kernel(x_ref, mod_ref, a_ref, b_ref, c_ref, d_ref, w_ref, o_ref):
    x = x_ref[0]
    D = x.shape[-1]
    dot = functools.partial(jnp.dot, preferred_element_type=F32)
    y = (dot(a_ref[0], w_ref[0:256, :]) + dot(b_ref[0], w_ref[256:512, :])) + \
        (dot(c_ref[0], w_ref[512:768, :]) + dot(d_ref[0], w_ref[768:1024, :]))
    o_ref[0] = x + mod_ref[0][:, 2 * D:3 * D] * y


def _out_proj(X, mod2, mixes, w, tm, j0, nct):
    B, Ta, D = X.shape
    nt = Ta // tm - j0
    tok = lambda n: pl.BlockSpec((1, tm, n), lambda j, b: (b, j + j0, 0))
    return pl.pallas_call(
        _out_kernel,
        grid=(nt, B),
        in_specs=[tok(D),
                  pl.BlockSpec((1, 1, 6 * D), lambda j, b: (2 * b + jnp.where(j + j0 >= nct, 1, 0), 0, 0)),
                  tok(256), tok(256), tok(256), tok(256),
                  pl.BlockSpec(w.shape, lambda j, b: (0, 0))],
        out_specs=tok(D),
        out_shape=jax.ShapeDtypeStruct((B, Ta, D), F32),
        compiler_params=_cparams(("parallel", "parallel")),
        name="out_proj",
    )(X, mod2, *mixes, w)


def _mlp_kernel(x_ref, xp_ref, xn_ref, mod_ref, g_ref, wa_ref, wb_ref, cw_ref, cb_ref, wd_ref, o_ref,
                *, j0, nct, nt, fc):
    j = pl.program_id(0) + j0
    tm = x_ref.shape[1]
    D = x_ref.shape[2]
    dff = wa_ref.shape[1]
    firstf = jnp.where((j == 0) | (j == nct), 0.0, 1.0)
    lastf = jnp.where((j == nct - 1) | (j == nt - 1), 0.0, 1.0)
    x = x_ref[0]
    xe = jnp.concatenate([xp_ref[0], x, xn_ref[0]], axis=0)
    m = mod_ref[0]
    h = (_rms(xe, g_ref[...], NORM_EPS) * (1.0 + m[:, 4 * D:5 * D]) + m[:, 3 * D:4 * D]).astype(BF16)
    row = lax.broadcasted_iota(jnp.int32, (tm, 1), 0)
    pmask = jnp.where(row == 0, firstf, 1.0)
    nmask = jnp.where(row == tm - 1, lastf, 1.0)
    dot = functools.partial(jnp.dot, preferred_element_type=F32)
    acc = jnp.zeros((tm, D), F32)
    for c in range(dff // fc):
        cs = slice(c * fc, (c + 1) * fc)
        a = dot(h, wa_ref[:, cs])
        b = dot(h[HALO:HALO + tm], wb_ref[:, cs])
        cv = (cw_ref[0:1, cs] * (a[HALO - 1:HALO - 1 + tm] * pmask) + cw_ref[1:2, cs] * a[HALO:HALO + tm]
              + cw_ref[2:3, cs] * (a[HALO + 1:HALO + 1 + tm] * nmask) + cb_ref[:, cs])
        u = cv * _sigmoid(cv) * b
        acc = acc + dot(u.astype(BF16), wd_ref[cs, :])
    o_ref[0] = x + m[:, 5 * D:6 * D] * acc


def _mlp(X, mod2, g, wa, wb, cw, cb, wd, tm, j0, nct):
    B, Ta, D = X.shape
    ntot = Ta // tm
    nt = ntot - j0
    hb = tm // HALO
    nh = Ta // HALO
    tok = lambda n: pl.BlockSpec((1, tm, n), lambda j, b: (b, j + j0, 0))
    full = lambda a: pl.BlockSpec(a.shape, lambda j, b: (0,) * a.ndim)
    kern = functools.partial(_mlp_kernel, j0=j0, nct=nct, nt=ntot, fc=256)
    return pl.pallas_call(
        kern,
        grid=(nt, B),
        in_specs=[tok(D),
                  pl.BlockSpec((1, HALO, D), lambda j, b: (b, jnp.maximum((j + j0) * hb - 1, j0 * hb), 0)),
                  pl.BlockSpec((1, HALO, D), lambda j, b: (b, jnp.minimum((j + j0 + 1) * hb, nh - 1), 0)),
                  pl.BlockSpec((1, 1, 6 * D), lambda j, b: (2 * b + jnp.where(j + j0 >= nct, 1, 0), 0, 0)),
                  full(g), full(wa), full(wb), full(cw), full(cb), full(wd)],
        out_specs=tok(D),
        out_shape=jax.ShapeDtypeStruct((B, Ta, D), F32),
        compiler_params=_cparams(("parallel", "parallel")),
        name="conv_glu",
    )(X, X, X, mod2, g, wa, wb, cw, cb, wd)


def _final_kernel(x_ref, g_ref, o_ref):
    o_ref[0] = _rms(x_ref[0], g_ref[...], NORM_EPS)


def _final_norm(X, g, tm, j0, T):
    B, Ta, D = X.shape
    return pl.pallas_call(
        _final_kernel,
        grid=(B, T // tm),
        in_specs=[pl.BlockSpec((1, tm, D), lambda b, j: (b, j + j0, 0)),
                  pl.BlockSpec((1, D), lambda b, j: (0, 0))],
        out_specs=pl.BlockSpec((1, tm, D), lambda b, j: (b, j, 0)),
        out_shape=jax.ShapeDtypeStruct((B, T, D), F32),
        compiler_params=_cparams(("parallel", "parallel")),
        name="final_norm",
    )(X, g)


def _rot_cols(w):
    s = w.shape
    x = w.reshape(s[:-1] + (s[-1] // ROPE_DIM, 4, ROPE_DIM // 4))
    r1, r2, c1, c2 = x[..., 0, :], x[..., 1, :], x[..., 2, :], x[..., 3, :]
    return jnp.stack([-r2, r1, -c2, c1], axis=-2).reshape(s)


def _rope_tables(T, C):
    t = np.arange(T)
    rowp = (t // GRID_W).astype(np.float32)
    colp = (t % GRID_W).astype(np.float32)
    half = ROPE_DIM // 2
    freqs = jnp.asarray(ROPE_THETA, F32) ** (-jnp.arange(0, half, 2, dtype=F32) / half)
    ar = jnp.asarray(rowp)[:, None] * freqs[None, :]
    ac = jnp.asarray(colp)[:, None] * freqs[None, :]
    ang = jnp.concatenate([ar, ar, ac, ac], axis=-1)
    cos = jnp.concatenate([jnp.ones((C, ROPE_DIM), F32), jnp.cos(ang)], axis=0)
    sin = jnp.concatenate([jnp.zeros((C, ROPE_DIM), F32), jnp.sin(ang)], axis=0)
    Ta = T + C
    cos256 = jnp.tile(cos, (1, 8))
    sin256 = jnp.tile(sin, (1, 8))
    one = jnp.ones((Ta, HEAD_DIM), F32)
    zero = jnp.zeros((Ta, HEAD_DIM), F32)
    cos512 = jnp.tile(jnp.concatenate([one, cos, one[:, :32]], axis=1), (1, 4))
    sin512 = jnp.tile(jnp.concatenate([zero, sin, zero[:, :32]], axis=1), (1, 4))
    return cos256, sin256, cos512, sin512


def _na_bias_tables(rpb):
    cpos = np.arange(GRID_W)
    cstart = np.clip(cpos - NA_KW // 2, 0, GRID_W - NA_KW)
    col_mask = (cpos[None, :] >= cstart[:, None]) & (cpos[None, :] < cstart[:, None] + NA_KW)
    col_idx = np.clip(cpos[None, :] - cpos[:, None] + NA_KW - 1, 0, 2 * NA_KW - 2)
    tabs = []
    for delta in range(NA_KH):
        row_off = np.arange(NA_KH) - delta + NA_KH - 1
        bias = rpb[:, :, row_off][:, :, :, col_idx]
        bias = jnp.where(jnp.asarray(col_mask)[None, None, None], bias, NEG_INF)
        tabs.append(bias.transpose(0, 1, 3, 2, 4).reshape(rpb.shape[0], N_HEADS, GRID_W, NA_KH * GRID_W))
    return jnp.stack(tabs, axis=1)


def kernel(x, c, ctx, c_ctx, norm1_g, norm2_g, ada_w, ada_b, w_in, w_out, na_rpb, mla_q_norm, mla_kv_norm,
           mla_w_uq, mla_w_ukv, rwkv_mu, rwkv_w0, rwkv_w_up, rwkv_a0, rwkv_a_up, rwkv_g_up, rwkv_k_k,
           rwkv_k_a, rwkv_r_k, rwkv_ln_w, rwkv_ln_b, diff_lambda, diff_subln, mlp_w_up, mlp_conv_w,
           mlp_conv_b, mlp_w_down, final_norm_g):
    B, T, D = x.shape
    C = ctx.shape[1]
    L = ada_w.shape[0]
    Ta = T + C
    tm = min(TOKEN_TILE, C)
    rows = T // GRID_W
    assert C % tm == 0 and T % tm == 0 and tm % CHUNK == 0 and rows >= NA_KH and D == 1024
    nct = C // tm
    dff = mlp_w_down.shape[1]

    wi = w_in
    na_w = wi[:, :, 0:768].at[:, :, 0:256].multiply(HEAD_DIM ** -0.5)
    cq_w, ckv_w, kr_w = wi[:, :, 768:1024], wi[:, :, 1024:1152], wi[:, :, 1152:1184]
    rw_w = wi[:, :, 1184:2080]
    dq_w, dk_w, dv_w = wi[:, :, 2080:2336], wi[:, :, 2336:2592], wi[:, :, 2592:2848]
    w_all = jnp.concatenate([na_w, dq_w, _rot_cols(dq_w), dk_w, _rot_cols(dk_w), dv_w, rw_w,
                             cq_w, ckv_w, kr_w, _rot_cols(kr_w), jnp.zeros((L, D, 64), F32)],
                            axis=-1).astype(BF16)
    w_out_b = w_out.astype(BF16)
    wa_b = mlp_w_up[:, :, :dff].astype(BF16)
    wb_b = mlp_w_up[:, :, dff:].astype(BF16)
    wd_b = mlp_w_down.astype(BF16)

    uq = mla_w_uq.reshape(L, MLA_Q_RANK, N_HEADS, HEAD_DIM + ROPE_DIM)
    pad32 = jnp.zeros((L, MLA_Q_RANK, N_HEADS, 32), F32)
    wqm = jnp.concatenate([uq, pad32], axis=-1).reshape(L, MLA_Q_RANK, 512).astype(BF16)
    wqr = jnp.concatenate([jnp.zeros_like(uq[..., :HEAD_DIM]), _rot_cols(uq[..., HEAD_DIM:]), pad32],
                          axis=-1).reshape(L, MLA_Q_RANK, 512).astype(BF16)
    ukv = mla_w_ukv.reshape(L, MLA_KV_RANK, N_HEADS, 2 * HEAD_DIM)
    wk = jnp.concatenate([ukv[..., :HEAD_DIM], jnp.zeros_like(ukv[..., HEAD_DIM:])],
                         axis=-1).reshape(L, MLA_KV_RANK, 512).astype(BF16)
    wv = ukv[..., HEAD_DIM:].reshape(L, MLA_KV_RANK, 256).astype(BF16)
    pm = np.zeros((128, 1024), np.float32)
    for h in range(N_HEADS):
        for i in range(ROPE_DIM):
            pm[i, h * 128 + HEAD_DIM + i] = 1.0
            pm[ROPE_DIM + i, 512 + h * 128 + HEAD_DIM + i] = 1.0
    pmat = jnp.asarray(pm, BF16)

    zr = lambda n: jnp.zeros((L, 2, n, GROUP_W), F32)
    wup_p = jnp.concatenate([rwkv_w_up, zr(96)], axis=2).astype(BF16)
    aup_p = jnp.concatenate([zr(32), rwkv_a_up, zr(64)], axis=2).astype(BF16)
    gup_p = jnp.concatenate([jnp.zeros((L, 64, GROUP_W), F32), rwkv_g_up], axis=1).astype(BF16)

    cos256, sin256, cos512, sin512 = _rope_tables(T, C)
    na_bias = _na_bias_tables(na_rpb)
    sub256 = jnp.tile(diff_subln, (1, N_HEADS))

    R = ((B + 1 + 7) // 8) * 8
    cc = jnp.concatenate([c, c_ctx[None], jnp.zeros((R - B - 1, D), F32)], axis=0)
    mod = _modulation(cc, ada_w, ada_b)
    mod2 = jnp.stack([jnp.broadcast_to(mod[:, B:B + 1], (L, B, 6 * D)), mod[:, :B]], axis=2)
    mod2 = mod2.reshape(L, 2 * B, 1, 6 * D)

    X = jnp.concatenate([ctx, x], axis=1)
    gb = 2 if B % 2 == 0 else 1
    for l in range(L):
        need_ctx = l < L - 1
        j0 = 0 if need_ctx else nct
        na_qkv, dq, dk, dv, z_rw, z_mla = _in_proj(X, mod2[l], norm1_g[l][None], w_all[l], cos256, sin256, tm, nct)
        na_o = _na_attn(na_qkv, na_bias[l], tm, j0, nct, C, rows)
        mq, mk, mv = _mla_prep(z_mla, cos512, sin512, mla_q_norm[l][None], mla_kv_norm[l][None],
                               wqm[l], wqr[l], wk[l], wv[l], pmat, tm)
        mla_o = _attn_call(functools.partial(_mla_attn_kernel, j0=j0, nct=nct, C=C, Ta=Ta), "mla_attn",
                           mq, mk, mv, [], [], tm, j0)
        rvk, gbn, dirp = _rwkv_prep(z_rw, rwkv_mu[l], rwkv_w0[l], wup_p[l], rwkv_a0[l], aup_p[l], gup_p[l],
                                    rwkv_k_k[l][None], rwkv_k_a[l][None], rwkv_r_k[l].reshape(1, GROUP_W),
                                    tm, nct)
        chk = _rwkv_chunks(rvk, dirp, tm)
        yf, yb = _rwkv_scan(chk, C // CHUNK, gb)
        rw_o = _rwkv_out(yf, yb, gbn, rwkv_ln_w[l][None], rwkv_ln_b[l][None], tm, j0)
        lam_init = 0.8 - 0.6 * math.exp(-0.3 * l)
        df_o = _attn_call(functools.partial(_diff_attn_kernel, j0=j0, nct=nct, C=C, Ta=Ta, lam_init=lam_init),
                          "diff_attn", dq, dk, dv, [diff_lambda[l], sub256[l][None]],
                          [pl.BlockSpec((4, DIFF_QK), lambda b, j: (0, 0)),
                           pl.BlockSpec((1, GROUP_W), lambda b, j: (0, 0))], tm, j0)
        X = _out_proj(X, mod2[l], (na_o, mla_o, rw_o, df_o), w_out_b[l], tm, j0, nct)
        X = _mlp(X, mod2[l], norm2_g[l][None], wa_b[l], wb_b[l], mlp_conv_w[l], mlp_conv_b[l][None], wd_b[l],
                 tm, j0, nct)
    return _final_norm(X, final_norm_g[None], tm, nct, T)
```

```python
import functools
import math

import jax
import jax.numpy as jnp
import numpy as np
from jax import lax
from jax.experimental import pallas as pl
from jax.experimental.pallas import tpu as pltpu

F32 = jnp.float32
BF16 = jnp.bfloat16

GRID_W = 64
GROUP_W = 256
HEAD_DIM = 64
N_HEADS = 4
ROPE_DIM = 32
ROPE_THETA = 10000.0
NORM_EPS = 1e-6
NEG_INF = -1e30
LOG2E = math.log2(math.e)
NA_KH = 8
NA_KW = 16
MLA_Q_RANK = 256
MLA_KV_RANK = 128
RWKV_COLS = 896
RWKV_LN_EPS = 64e-5
DIFF_QK = 32
DIFF_LN_EPS = 1e-5
IN_SPLITS = (768, 1184, 2080, 2848)

W_NA = 0
W_DIFF = 768
W_RWKV = 2048
W_MLA = 2944
W_TOT = 3456

TOKEN_TILE = 256
CHUNK = 64
RWKV_TILE = 768
HALO = 8
VMEM_LIMIT = 56 * 1024 * 1024


def _cparams(sem):
    return pltpu.CompilerParams(dimension_semantics=sem, vmem_limit_bytes=VMEM_LIMIT)


def _resident(a):
    return pl.BlockSpec(a.shape, lambda *_: (0,) * a.ndim, pipeline_mode=pl.Buffered(1))


def _dot(a, b):
    return jnp.dot(a.astype(BF16), b.astype(BF16), preferred_element_type=F32)


def _dot_nt(a, b):
    return lax.dot_general(a.astype(BF16), b.astype(BF16), (((1,), (1,)), ((), ())),
                           preferred_element_type=F32)


def _split2(x):
    hi = x.astype(BF16)
    lo = (x - hi.astype(F32)).astype(BF16)
    return hi, lo


def _split3(x):
    h1 = x.astype(BF16)
    r1 = x - h1.astype(F32)
    h2 = r1.astype(BF16)
    h3 = (r1 - h2.astype(F32)).astype(BF16)
    return h1, h2, h3


def _dot3(a, b):
    ah, al = _split2(a)
    bh, bl = _split2(b)
    d = functools.partial(jnp.dot, preferred_element_type=F32)
    return d(ah, bh) + (d(ah, bl) + d(al, bh))


def _sigmoid(x):
    return 1.0 / (1.0 + jnp.exp(-x))


def _softplus(x):
    return jnp.maximum(x, 0.0) + jnp.log(1.0 + jnp.exp(-jnp.abs(x)))


def _rms(x, g, eps):
    return x * lax.rsqrt(jnp.mean(x * x, axis=-1, keepdims=True) + eps) * g


def _head_ones(n=GROUP_W):
    r = lax.broadcasted_iota(jnp.int32, (n, n), 0) >> 6
    c = lax.broadcasted_iota(jnp.int32, (n, n), 1) >> 6
    return jnp.where(r == c, 1.0, 0.0).astype(BF16)


def _head_sum(x, ones):
    hi, lo = _split2(x)
    d = functools.partial(jnp.dot, preferred_element_type=F32)
    return d(hi, ones) + d(lo, ones)


def _bd_mask():
    r = lax.broadcasted_iota(jnp.int32, (GROUP_W, GROUP_W), 0) >> 6
    c = lax.broadcasted_iota(jnp.int32, (GROUP_W, GROUP_W), 1) >> 6
    return r == c


def _bd(x, mask):
    xb = x.astype(BF16)
    return jnp.where(mask, jnp.concatenate([xb, xb, xb, xb], axis=0), jnp.zeros((), BF16))


def _mod_kernel(s_ref, w_ref, b_ref, o_ref):
    s = s_ref[...]
    s = s * _sigmoid(s)
    o_ref[0] = _dot3(s, w_ref[0]) + b_ref[0]


def _modulation(cc, ada_w, ada_b):
    L, D, N = ada_w.shape
    R = cc.shape[0]
    tn = 1536
    return pl.pallas_call(
        _mod_kernel,
        grid=(L, N // tn),
        in_specs=[pl.BlockSpec((R, D), lambda l, n: (0, 0)),
                  pl.BlockSpec((1, D, tn), lambda l, n: (l, 0, n)),
                  pl.BlockSpec((1, 1, tn), lambda l, n: (l, 0, n))],
        out_specs=pl.BlockSpec((1, R, tn), lambda l, n: (l, 0, n)),
        out_shape=jax.ShapeDtypeStruct((L, R, N), F32),
        compiler_params=_cparams(("parallel", "parallel")),
        name="adaln_mod",
    )(cc, ada_w, ada_b.reshape(L, 1, N))


def _in_kernel(x_ref, mod_ref, g_ref, w_ref, cos_ref, sin_ref, cos5_ref, sin5_ref,
               qn_ref, kvn_ref, wqm_ref, wqr_ref, wk_ref, wv_ref, p_ref,
               na_ref, dq_ref, dk_ref, dv_ref, rw_ref, mq_ref, mk_ref, mv_ref):
    x = x_ref[0]
    D = x.shape[-1]
    m = mod_ref[0]
    h = _rms(x, g_ref[...], NORM_EPS) * (1.0 + m[:, D:2 * D]) + m[:, 0:D]
    hb = h.astype(BF16)
    dot = functools.partial(jnp.dot, preferred_element_type=F32)
    na_ref[0] = dot(hb, w_ref[:, W_NA:W_DIFF]).astype(BF16)
    d = dot(hb, w_ref[:, W_DIFF:W_RWKV])
    cos = cos_ref[...]
    sin = sin_ref[...]
    scale = DIFF_QK ** -0.5 * LOG2E
    dq_ref[0] = ((d[:, 0:256] * cos + d[:, 256:512] * sin) * scale).astype(BF16)
    dk_ref[0] = (d[:, 512:768] * cos + d[:, 768:1024] * sin).astype(BF16)
    dv_ref[0] = d[:, 1024:1280].astype(BF16)
    rw_ref[0] = dot(hb, w_ref[:, W_RWKV:W_MLA])
    z = dot(hb, w_ref[:, W_MLA:W_TOT])
    cos5 = cos5_ref[...]
    sin5 = sin5_ref[...]
    nq = _rms(z[:, 0:256], qn_ref[...], NORM_EPS).astype(BF16)
    mscale = (HEAD_DIM + ROPE_DIM) ** -0.5 * LOG2E
    mq_ref[0] = ((dot(nq, wqm_ref[...]) * cos5 + dot(nq, wqr_ref[...]) * sin5) * mscale).astype(BF16)
    nkv = _rms(z[:, 256:384], kvn_ref[...], NORM_EPS).astype(BF16)
    kr = dot(z[:, 384:512].astype(BF16), p_ref[...])
    mk_ref[0] = (dot(nkv, wk_ref[...]) + kr[:, 0:512] * cos5 + kr[:, 512:1024] * sin5).astype(BF16)
    mv_ref[0] = dot(nkv, wv_ref[...]).astype(BF16)


def _in_proj(X, mod2, g, w, cos256, sin256, cos512, sin512, qn, kvn, wqm, wqr, wk, wv, pmat, tm, nct):
    B, Ta, D = X.shape
    nt = Ta // tm
    tok = lambda n: pl.BlockSpec((1, tm, n), lambda j, b: (b, j, 0))
    tab = lambda n: pl.BlockSpec((tm, n), lambda j, b: (j, 0))
    full = _resident
    return pl.pallas_call(
        _in_kernel,
        grid=(nt, B),
        in_specs=[tok(D),
                  pl.BlockSpec((1, 1, 6 * D), lambda j, b: (2 * b + jnp.where(j >= nct, 1, 0), 0, 0)),
                  full(g), full(w), tab(256), tab(256), tab(512), tab(512),
                  full(qn), full(kvn), full(wqm), full(wqr), full(wk), full(wv), full(pmat)],
        out_specs=[tok(768), tok(256), tok(256), tok(256), tok(RWKV_COLS), tok(512), tok(512), tok(256)],
        out_shape=[jax.ShapeDtypeStruct((B, Ta, 768), BF16),
                   jax.ShapeDtypeStruct((B, Ta, 256), BF16),
                   jax.ShapeDtypeStruct((B, Ta, 256), BF16),
                   jax.ShapeDtypeStruct((B, Ta, 256), BF16),
                   jax.ShapeDtypeStruct((B, Ta, RWKV_COLS), F32),
                   jax.ShapeDtypeStruct((B, Ta, 512), BF16),
                   jax.ShapeDtypeStruct((B, Ta, 512), BF16),
                   jax.ShapeDtypeStruct((B, Ta, 256), BF16)],
        compiler_params=_cparams(("parallel", "parallel")),
        name="in_proj",
    )(X, mod2, g, w, cos256, sin256, cos512, sin512, qn, kvn, wqm, wqr, wk, wv, pmat)


def _softmax_parts(s):
    m = jnp.max(s, axis=-1, keepdims=True)
    e = jnp.exp2(s - m)
    return e, jnp.sum(e, axis=-1, keepdims=True)


def _mla_attn_body(q, k_ref, v_ref, nk):
    tq = q.shape[0]
    lane = lax.broadcasted_iota(jnp.int32, (tq, 128), 1)
    outs = []
    for pair in range(2):
        vp = v_ref[0, 0:nk, pair * 128:(pair + 1) * 128]
        o2 = []
        for hh in range(2):
            h = 2 * pair + hh
            s = _dot_nt(q[:, h * 128:(h + 1) * 128], k_ref[0, 0:nk, h * 128:(h + 1) * 128])
            e, l = _softmax_parts(s)
            o2.append(jnp.dot(e.astype(BF16), vp, preferred_element_type=F32) * (1.0 / l))
        outs.append(jnp.where(lane < 64, o2[0], o2[1]))
    return jnp.concatenate(outs, axis=1)


def _mla_attn_kernel(q_ref, k_ref, v_ref, o_ref, *, j0, nct, C, Ta):
    j = pl.program_id(1) + j0
    q = q_ref[0]
    if j0 < nct:
        @pl.when(j < nct)
        def _():
            o_ref[0] = _mla_attn_body(q, k_ref, v_ref, C).astype(BF16)

    @pl.when(j >= nct)
    def _():
        o_ref[0] = _mla_attn_body(q, k_ref, v_ref, Ta).astype(BF16)


def _attn_call(kern, name, q, k, v, extra, extra_specs, tq, j0, out_w=256):
    B, Ta, _ = q.shape
    nq = Ta // tq - j0
    return pl.pallas_call(
        kern,
        grid=(B, nq),
        in_specs=[pl.BlockSpec((1, tq, q.shape[-1]), lambda b, j: (b, j + j0, 0)),
                  pl.BlockSpec((1, Ta, k.shape[-1]), lambda b, j: (b, 0, 0)),
                  pl.BlockSpec((1, Ta, v.shape[-1]), lambda b, j: (b, 0, 0))] + extra_specs,
        out_specs=pl.BlockSpec((1, tq, out_w), lambda b, j: (b, j + j0, 0)),
        out_shape=jax.ShapeDtypeStruct((B, Ta, out_w), BF16),
        compiler_params=_cparams(("parallel", "arbitrary")),
        name=name,
    )(q, k, v, *extra)


def _diff_body(q, k_ref, v_ref, nk, lam):
    tq = q.shape[0]
    k = k_ref[0, 0:nk, :]
    sub = lax.broadcasted_iota(jnp.int32, (tq, GROUP_W), 1) >> 5
    lane = lax.broadcasted_iota(jnp.int32, (tq, 128), 1)
    zero = jnp.zeros((), BF16)
    outs = []
    for pair in range(2):
        vp = v_ref[0, 0:nk, pair * 128:(pair + 1) * 128]
        o2 = []
        for hh in range(2):
            h = 2 * pair + hh
            e1, l1 = _softmax_parts(_dot_nt(jnp.where(sub == 2 * h, q, zero), k))
            e2, l2 = _softmax_parts(_dot_nt(jnp.where(sub == 2 * h + 1, q, zero), k))
            o2.append(jnp.dot(e1.astype(BF16), vp, preferred_element_type=F32) * (1.0 / l1)
                      - jnp.dot(e2.astype(BF16), vp, preferred_element_type=F32) * (lam / l2))
        outs.append(jnp.where(lane < 64, o2[0], o2[1]))
    return jnp.concatenate(outs, axis=1)


def _diff_attn_kernel(q_ref, k_ref, v_ref, lam_ref, sub_ref, o_ref, *, j0, nct, C, Ta, lam_init):
    j = pl.program_id(1) + j0
    q = q_ref[0]
    lp = lam_ref[...]
    lam = (jnp.exp(jnp.sum(lp[0:1] * lp[1:2], axis=-1, keepdims=True))
           - jnp.exp(jnp.sum(lp[2:3] * lp[3:4], axis=-1, keepdims=True)) + lam_init)
    ones = _head_ones()

    def finish(o):
        ms = _head_sum(o * o, ones) * (1.0 / HEAD_DIM)
        return (o * lax.rsqrt(ms + DIFF_LN_EPS) * sub_ref[...]) * (1.0 - lam_init)

    if j0 < nct:
        @pl.when(j < nct)
        def _():
            o_ref[0] = finish(_diff_body(q, k_ref, v_ref, C, lam)).astype(BF16)

    @pl.when(j >= nct)
    def _():
        o_ref[0] = finish(_diff_body(q, k_ref, v_ref, Ta, lam)).astype(BF16)


def _na_kernel(qt_ref, kv_ref, bias_ref, o_ref, *, j0, nct, C, rows, R):
    j = pl.program_id(1) + j0
    tq = R * GRID_W
    zero = jnp.zeros((), BF16)

    if j0 < nct:
        @pl.when(j < nct)
        def _():
            lane = lax.broadcasted_iota(jnp.int32, (tq, 128), 1)
            outs = []
            for pair in range(2):
                q2 = qt_ref[0, :, pair * 128:(pair + 1) * 128]
                kc = kv_ref[0, 0:C, 256 + pair * 128:256 + (pair + 1) * 128]
                vc = kv_ref[0, 0:C, 512 + pair * 128:512 + (pair + 1) * 128]
                o2 = []
                for hh in range(2):
                    qm = jnp.where((lane < 64) if hh == 0 else (lane >= 64), q2, zero)
                    e, l = _softmax_parts(_dot_nt(qm, kc))
                    o2.append(jnp.dot(e.astype(BF16), vc, preferred_element_type=F32) * (1.0 / l))
                outs.append(jnp.where(lane < 64, o2[0], o2[1]))
            o_ref[0] = jnp.concatenate(outs, axis=1).astype(BF16)

    @pl.when(j >= nct)
    def _():
        lane = lax.broadcasted_iota(jnp.int32, (GRID_W, 128), 1)
        W = NA_KH * GRID_W
        dot = functools.partial(jnp.dot, preferred_element_type=F32)
        starts, deltas = [], []
        for rr in range(R):
            r = (j - nct) * R + rr
            rs = jnp.clip(r - NA_KH // 2, 0, rows - NA_KH)
            deltas.append(r - rs)
            starts.append(pl.multiple_of(C + rs * GRID_W, GRID_W))
        units = [(rr, pair) for pair in range(2) for rr in range(R)]
        qs = {}
        for rr, pair in units:
            q2 = qt_ref[0, rr * GRID_W:(rr + 1) * GRID_W, pair * 128:(pair + 1) * 128]
            qs[rr, pair] = jnp.concatenate([jnp.where(lane < 64, q2, zero), jnp.where(lane >= 64, q2, zero)], axis=0)
        s_ctx = {}
        for pair in range(2):
            kc = kv_ref[0, 0:C, 256 + pair * 128:256 + (pair + 1) * 128]
            sc = _dot_nt(jnp.concatenate([qs[rr, pair] for rr in range(R)], axis=0), kc)
            for rr in range(R):
                s_ctx[rr, pair] = sc[rr * 128:(rr + 1) * 128]
        s_lat = {}
        for rr, pair in units:
            kw = kv_ref[0, pl.ds(starts[rr], W), 256 + pair * 128:256 + (pair + 1) * 128]
            bias = jnp.concatenate([bias_ref[deltas[rr], 2 * pair], bias_ref[deltas[rr], 2 * pair + 1]], axis=0)
            s_lat[rr, pair] = _dot_nt(qs[rr, pair], kw) + bias
        e_lat, e_ctx, inv = {}, {}, {}
        for u in units:
            m = jnp.maximum(jnp.max(s_lat[u], axis=-1, keepdims=True), jnp.max(s_ctx[u], axis=-1, keepdims=True))
            e_lat[u] = jnp.exp2(s_lat[u] - m)
            ec = jnp.exp2(s_ctx[u] - m)
            e_ctx[u] = ec.astype(BF16)
            inv[u] = 1.0 / (jnp.sum(e_lat[u], axis=-1, keepdims=True) + jnp.sum(ec, axis=-1, keepdims=True))
        o_ctx = {}
        for pair in range(2):
            vc = kv_ref[0, 0:C, 512 + pair * 128:512 + (pair + 1) * 128]
            oc = dot(jnp.concatenate([e_ctx[rr, pair] for rr in range(R)], axis=0), vc)
            for rr in range(R):
                o_ctx[rr, pair] = oc[rr * 128:(rr + 1) * 128]
        o = {}
        for rr, pair in units:
            vw = kv_ref[0, pl.ds(starts[rr], W), 512 + pair * 128:512 + (pair + 1) * 128]
            ou = (dot(e_lat[rr, pair].astype(BF16), vw) + o_ctx[rr, pair]) * inv[rr, pair]
            o[rr, pair] = jnp.where(lane < 64, ou[0:GRID_W], ou[GRID_W:])
        o_ref[0] = jnp.concatenate([jnp.concatenate([o[rr, 0], o[rr, 1]], axis=1) for rr in range(R)],
                                   axis=0).astype(BF16)


def _na_attn(qkv, bias, tq, j0, nct, C, rows):
    B, Ta, _ = qkv.shape
    nq = Ta // tq - j0
    kern = functools.partial(_na_kernel, j0=j0, nct=nct, C=C, rows=rows, R=tq // GRID_W)
    return pl.pallas_call(
        kern,
        grid=(B, nq),
        in_specs=[pl.BlockSpec((1, tq, 768), lambda b, j: (b, j + j0, 0)),
                  pl.BlockSpec((1, Ta, 768), lambda b, j: (b, 0, 0)),
                  _resident(bias)],
        out_specs=pl.BlockSpec((1, tq, 256), lambda b, j: (b, j + j0, 0)),
        out_shape=jax.ShapeDtypeStruct((B, Ta, 256), BF16),
        compiler_params=_cparams(("parallel", "arbitrary")),
        name="na_attn",
    )(qkv, qkv, bias)


def _rwkv_prep_kernel(z_ref, zp_ref, zn_ref, mu_ref, w0_ref, wup_ref, a0_ref, aup_ref, gup_ref,
                      kk_ref, ka_ref, rk_ref, rvk_ref, gb_ref, dir_ref, *, nct, nt):
    j = pl.program_id(0)
    tm = z_ref.shape[1]
    firstf = jnp.where((j == 0) | (j == nct), 0.0, 1.0)
    lastf = jnp.where((j == nct - 1) | (j == nt - 1), 0.0, 1.0)
    z = z_ref[0]
    row = lax.broadcasted_iota(jnp.int32, (tm, 1), 0)
    prev = jnp.where(row == 0, zp_ref[0, HALO - 1:HALO, :] * firstf, pltpu.roll(z, 1, axis=0))
    nxt = jnp.where(row == tm - 1, zn_ref[0, 0:1, :] * lastf, pltpu.roll(z, tm - 1, axis=0))
    zs = z + mu_ref[0:1, :] * (prev - z) + mu_ref[1:2, :] * (nxt - z)
    r = zs[:, 0:256]
    k = zs[:, 256:512]
    v = zs[:, 512:768]
    low = zs[:, 768:896]
    ones = _head_ones()
    kk = k * kk_ref[...]
    kk = kk * lax.rsqrt(jnp.maximum(_head_sum(kk * kk, ones), 1e-24))
    g = _dot(_sigmoid(low), gup_ref[...])
    bonus = _head_sum(r * k * rk_ref[...], ones) * v
    rvk_ref[0, :, 0:256] = r
    rvk_ref[0, :, 256:512] = v
    rvk_ref[0, :, 512:768] = kk
    gb_ref[0, :, 0:256] = g
    gb_ref[0, :, 256:512] = bonus
    tl = jnp.tanh(low).astype(BF16)
    lb = low.astype(BF16)
    for d in range(2):
        w = -_softplus(-(w0_ref[d:d + 1, :] + jnp.dot(tl, wup_ref[d], preferred_element_type=F32))) - 0.5
        a = _sigmoid(a0_ref[d:d + 1, :] + jnp.dot(lb, aup_ref[d], preferred_element_type=F32))
        dir_ref[d, 0, :, 0:256] = -jnp.exp(w)
        dir_ref[d, 0, :, 256:512] = k * (1.0 + (a - 1.0) * ka_ref[...])
        dir_ref[d, 0, :, 512:768] = kk * a


def _rwkv_prep(z, mu, w0, wup_p, a0, aup_p, gup_p, k_k, k_a, r_k, tm, nct):
    B, Ta, _ = z.shape
    nt = Ta // tm
    hb = tm // HALO
    nh = Ta // HALO
    tok = lambda n: pl.BlockSpec((1, tm, n), lambda j, b: (b, j, 0))
    full = _resident
    kern = functools.partial(_rwkv_prep_kernel, nct=nct, nt=nt)
    return pl.pallas_call(
        kern,
        grid=(nt, B),
        in_specs=[tok(RWKV_COLS),
                  pl.BlockSpec((1, HALO, RWKV_COLS), lambda j, b: (b, jnp.maximum(j * hb - 1, 0), 0)),
                  pl.BlockSpec((1, HALO, RWKV_COLS), lambda j, b: (b, jnp.minimum((j + 1) * hb, nh - 1), 0)),
                  full(mu), full(w0), full(wup_p), full(a0), full(aup_p), full(gup_p),
                  full(k_k), full(k_a), full(r_k)],
        out_specs=[tok(768), tok(512),
                   pl.BlockSpec((2, 1, tm, 768), lambda j, b: (0, b, j, 0))],
        out_shape=[jax.ShapeDtypeStruct((B, Ta, 768), F32),
                   jax.ShapeDtypeStruct((B, Ta, 512), F32),
                   jax.ShapeDtypeStruct((2, B, Ta, 768), F32)],
        compiler_params=_cparams(("parallel", "parallel")),
        name="rwkv_prep",
    )(z, z, z, mu, w0, wup_p, a0, aup_p, gup_p, k_k, k_a, r_k)


def _rwkv_chunk_kernel(rvk_ref, dir_ref, o_ref):
    d = pl.program_id(0)
    nchunk = rvk_ref.shape[1] // CHUNK
    bdm = _bd_mask()
    row = lax.broadcasted_iota(jnp.int32, (CHUNK, GROUP_W), 0)
    col = lax.broadcasted_iota(jnp.int32, (CHUNK, GROUP_W), 1) & (CHUNK - 1)
    tdiff = jnp.where(d == 0, row - col, col - row)
    incl = tdiff >= 0
    strict = tdiff > 0
    eye = tdiff == 0
    r2 = lax.broadcasted_iota(jnp.int32, (CHUNK, CHUNK), 0)
    c2 = lax.broadcasted_iota(jnp.int32, (CHUNK, CHUNK), 1)
    tri = jnp.where(jnp.where(d == 0, r2 - c2, c2 - r2) >= 0, 1.0, 0.0).astype(BF16)
    dot = functools.partial(jnp.dot, preferred_element_type=F32)

    def hmul(x, y):
        return dot(x.astype(BF16), _bd(y, bdm))

    def hmul_t(x, yt):
        ytb = yt.astype(BF16)
        return dot(x.astype(BF16), jnp.where(bdm, jnp.concatenate([ytb, ytb, ytb, ytb], axis=1),
                                             jnp.zeros((), BF16)))

    def fold(full):
        full = jnp.where(bdm, full, 0.0)
        return (full[0:64] + full[64:128]) + (full[128:192] + full[192:256])

    cs = range(nchunk)
    sls = [slice(c * CHUNK, (c + 1) * CHUNK) for c in cs]
    r = [rvk_ref[0, sl, 0:256] for sl in sls]
    v = [rvk_ref[0, sl, 256:512] for sl in sls]
    kk = [rvk_ref[0, sl, 512:768] for sl in sls]
    lw = [dir_ref[0, 0, sl, 0:256] for sl in sls]
    kd = [dir_ref[0, 0, sl, 256:512] for sl in sls]
    b = [dir_ref[0, 0, sl, 512:768] for sl in sls]
    cum, tot = [], []
    for c in cs:
        l1, l2, l3 = _split3(lw[c])
        cum.append(dot(tri, l1) + (dot(tri, l2) + dot(tri, l3)))
        tot.append(jnp.sum(lw[c], axis=0, keepdims=True))
    rt = [r[c] * jnp.exp(cum[c]) for c in cs]
    at = [-kk[c] * jnp.exp(cum[c] - lw[c]) for c in cs]
    einv = [jnp.exp(-cum[c]) for c in cs]
    eend = [jnp.exp(tot[c] - cum[c]) for c in cs]
    ar = [jnp.concatenate([at[c], rt[c]], axis=0) for c in cs]
    xb = [hmul_t(ar[c], (b[c] * einv[c]).T) for c in cs]
    xk = [hmul_t(ar[c], (kd[c] * einv[c]).T) for c in cs]
    n = [jnp.where(strict, xb[c][0:CHUNK], 0.0) for c in cs]
    lrb = [jnp.where(incl, xb[c][CHUNK:], 0.0) for c in cs]
    al = [jnp.concatenate([jnp.where(strict, xk[c][0:CHUNK], 0.0),
                           jnp.where(incl, xk[c][CHUNK:], 0.0)], axis=0) for c in cs]
    akv_lrkv = [hmul(al[c], v[c]) for c in cs]
    p = [jnp.where(eye, 1.0, 0.0) + n[c] for c in cs]
    npow = [hmul(n[c], n[c]) for c in cs]
    for _ in range(4):
        sq = [hmul(jnp.concatenate([npow[c], p[c]], axis=0), npow[c]) for c in cs]
        npow = [sq[c][0:CHUNK] for c in cs]
        p = [p[c] + sq[c][CHUNK:] for c in cs]
    t = [p[c] + hmul(p[c], npow[c]) for c in cs]
    w = [hmul(t[c], at[c]) for c in cs]
    u0 = [hmul(t[c], akv_lrkv[c][0:CHUNK]) for c in cs]
    qh = [rt[c] + hmul(lrb[c], w[c]) for c in cs]
    y0 = [hmul(lrb[c], u0[c]) + akv_lrkv[c][CHUNK:] for c in cs]
    outs = []
    for c in cs:
        bht = (b[c] * eend[c]).T.astype(BF16)
        kht = (kd[c] * eend[c]).T.astype(BF16)
        g = fold(dot(bht, w[c].astype(BF16))) + jnp.where(eye, jnp.exp(tot[c]), 0.0)
        hm = fold(dot(bht, u0[c].astype(BF16)) + dot(kht, v[c].astype(BF16)))
        outs.append(jnp.concatenate([qh[c], y0[c], g, hm], axis=1))
    o_ref[0, 0] = jnp.concatenate(outs, axis=0)


def _rwkv_chunks(rvk, dirp, tm):
    B, Ta, _ = rvk.shape
    nt = Ta // tm
    return pl.pallas_call(
        _rwkv_chunk_kernel,
        grid=(2, B, nt),
        in_specs=[pl.BlockSpec((1, tm, 768), lambda d, b, j: (b, j, 0)),
                  pl.BlockSpec((1, 1, tm, 768), lambda d, b, j: (d, b, j, 0))],
        out_specs=pl.BlockSpec((1, 1, tm, 1024), lambda d, b, j: (d, b, j, 0)),
        out_shape=jax.ShapeDtypeStruct((2, B, Ta, 1024), F32),
        compiler_params=_cparams(("parallel", "parallel", "parallel")),
        name="rwkv_chunks",
    )(rvk, dirp)


def _rwkv_scan_kernel(f_ref, b_ref, yf_ref, yb_ref, s_ref):
    i = pl.program_id(1)
    gb = f_ref.shape[1]
    bdm = _bd_mask()

    @pl.when(i == 0)
    def _():
        s_ref[...] = jnp.zeros(s_ref.shape, F32)

    dot = functools.partial(jnp.dot, preferred_element_type=F32)
    zero = jnp.zeros((), BF16)
    chains = [(d, bb) for d in range(2) for bb in range(gb)]
    c_refs = (f_ref, b_ref)
    parts = {}
    for d, bb in chains:
        sh, sl = _split2(s_ref[d, bb])
        lh, ll = _split2(jnp.concatenate([c_refs[d][0, bb, :, 0:256], c_refs[d][0, bb, :, 512:768]], axis=0))
        parts[d, bb] = (lh, ll, jnp.where(bdm, jnp.concatenate([sh] * 4, axis=0), zero),
                        jnp.where(bdm, jnp.concatenate([sl] * 4, axis=0), zero))
    res = {}
    for ch in chains:
        lh, ll, sbh, sbl = parts[ch]
        res[ch] = dot(lh, sbh) + (dot(lh, sbl) + dot(ll, sbh))
    for d, y_ref in enumerate((yf_ref, yb_ref)):
        for bb in range(gb):
            y_ref[bb] = res[d, bb][0:CHUNK] + c_refs[d][0, bb, :, 256:512]
            s_ref[d, bb] = res[d, bb][CHUNK:] + c_refs[d][0, bb, :, 768:1024]


def _rwkv_scan(chk, ncc, gb):
    _, B, Ta, _ = chk.shape
    nc = Ta // CHUNK

    def rev_chunk(i):
        return jnp.where(i < ncc, ncc - 1 - i, nc - 1 - (i - ncc))

    return pl.pallas_call(
        _rwkv_scan_kernel,
        grid=(B // gb, nc),
        in_specs=[pl.BlockSpec((1, gb, CHUNK, 1024), lambda b, i: (0, b, i, 0)),
                  pl.BlockSpec((1, gb, CHUNK, 1024), lambda b, i: (1, b, rev_chunk(i), 0))],
        out_specs=[pl.BlockSpec((gb, CHUNK, 256), lambda b, i: (b, i, 0)),
                   pl.BlockSpec((gb, CHUNK, 256), lambda b, i: (b, rev_chunk(i), 0))],
        out_shape=[jax.ShapeDtypeStruct((B, Ta, 256), F32),
                   jax.ShapeDtypeStruct((B, Ta, 256), F32)],
        scratch_shapes=[pltpu.VMEM((2, gb, CHUNK, GROUP_W), F32)],
        compiler_params=_cparams(("parallel", "arbitrary")),
        name="rwkv_scan",
    )(chk, chk)


def _out_kernel(x_ref, mod_ref, na_ref, mla_ref, yf_ref, yb_ref, gb_ref, lnw_ref, lnb_ref, df_ref, w_ref, o_ref):
    x = x_ref[0]
    D = x.shape[-1]
    dot = functools.partial(jnp.dot, preferred_element_type=F32)
    ones = _head_ones()
    y = yf_ref[0] + yb_ref[0]
    mean = _head_sum(y, ones) * (1.0 / HEAD_DIM)
    yc = y - mean
    var = _head_sum(yc * yc, ones) * (1.0 / HEAD_DIM)
    yn = yc * lax.rsqrt(var + RWKV_LN_EPS) * lnw_ref[...] + lnb_ref[...]
    rw = ((yn + gb_ref[0, :, 256:512]) * gb_ref[0, :, 0:256]).astype(BF16)
    mix = (dot(na_ref[0], w_ref[0:256, :]) + dot(mla_ref[0], w_ref[256:512, :])) + \
          (dot(rw, w_ref[512:768, :]) + dot(df_ref[0], w_ref[768:1024, :]))
    o_ref[0] = x + mod_ref[0][:, 2 * D:3 * D] * mix


def _out_proj(X, mod2, na_o, mla_o, yf, yb, gbn, ln_w, ln_b, df_o, w, tm, j0, nct):
    B, Ta, D = X.shape
    nt = Ta // tm - j0
    tok = lambda n: pl.BlockSpec((1, tm, n), lambda j, b: (b, j + j0, 0))
    full = _resident
    return pl.pallas_call(
        _out_kernel,
        grid=(nt, B),
        in_specs=[tok(D),
                  pl.BlockSpec((1, 1, 6 * D), lambda j, b: (2 * b + jnp.where(j + j0 >= nct, 1, 0), 0, 0)),
                  tok(256), tok(256), tok(256), tok(256), tok(512), full(ln_w), full(ln_b), tok(256),
                  full(w)],
        out_specs=tok(D),
        out_shape=jax.ShapeDtypeStruct((B, Ta, D), F32),
        compiler_params=_cparams(("parallel", "parallel")),
        name="out_proj",
    )(X, mod2, na_o, mla_o, yf, yb, gbn, ln_w, ln_b, df_o, w)


def _mlp_kernel(x_ref, xp_ref, xn_ref, mod_ref, g_ref, wa_ref, wb_ref, cw_ref, cb_ref, wd_ref, gf_ref, o_ref,
                *, j0, nct, nt, fc, final):
    j = pl.program_id(0) + j0
    tm = x_ref.shape[1]
    D = x_ref.shape[2]
    dff = wa_ref.shape[1]
    firstf = jnp.where((j == 0) | (j == nct), 0.0, 1.0)
    lastf = jnp.where((j == nct - 1) | (j == nt - 1), 0.0, 1.0)
    x = x_ref[0]
    xe = jnp.concatenate([xp_ref[0], x, xn_ref[0]], axis=0)
    m = mod_ref[0]
    h = (_rms(xe, g_ref[...], NORM_EPS) * (1.0 + m[:, 4 * D:5 * D]) + m[:, 3 * D:4 * D]).astype(BF16)
    row = lax.broadcasted_iota(jnp.int32, (tm, 1), 0)
    pmask = jnp.where(row == 0, firstf, 1.0)
    nmask = jnp.where(row == tm - 1, lastf, 1.0)
    dot = functools.partial(jnp.dot, preferred_element_type=F32)
    acc = jnp.zeros((tm, D), F32)
    for c in range(dff // fc):
        cs = slice(c * fc, (c + 1) * fc)
        a = dot(h, wa_ref[:, cs])
        b = dot(h[HALO:HALO + tm], wb_ref[:, cs])
        cv = (cw_ref[0:1, cs] * (a[HALO - 1:HALO - 1 + tm] * pmask) + cw_ref[1:2, cs] * a[HALO:HALO + tm]
              + cw_ref[2:3, cs] * (a[HALO + 1:HALO + 1 + tm] * nmask) + cb_ref[:, cs])
        u = cv * _sigmoid(cv) * b
        acc = acc + dot(u.astype(BF16), wd_ref[cs, :])
    y = x + m[:, 5 * D:6 * D] * acc
    o_ref[0] = _rms(y, gf_ref[...], NORM_EPS) if final else y


def _mlp(X, mod2, g, wa, wb, cw, cb, wd, gf, tm, j0, nct, final):
    B, Ta, D = X.shape
    ntot = Ta // tm
    nt = ntot - j0
    hb = tm // HALO
    nh = Ta // HALO
    tok = lambda n: pl.BlockSpec((1, tm, n), lambda j, b: (b, j + j0, 0))
    full = _resident
    kern = functools.partial(_mlp_kernel, j0=j0, nct=nct, nt=ntot, fc=wa.shape[1], final=final)
    if final:
        assert j0 == nct
        out_spec = pl.BlockSpec((1, tm, D), lambda j, b: (b, j, 0))
        out_shape = jax.ShapeDtypeStruct((B, Ta - nct * tm, D), F32)
    else:
        out_spec, out_shape = tok(D), jax.ShapeDtypeStruct((B, Ta, D), F32)
    return pl.pallas_call(
        kern,
        grid=(nt, B),
        in_specs=[tok(D),
                  pl.BlockSpec((1, HALO, D), lambda j, b: (b, jnp.maximum((j + j0) * hb - 1, j0 * hb), 0)),
                  pl.BlockSpec((1, HALO, D), lambda j, b: (b, jnp.minimum((j + j0 + 1) * hb, nh - 1), 0)),
                  pl.BlockSpec((1, 1, 6 * D), lambda j, b: (2 * b + jnp.where(j + j0 >= nct, 1, 0), 0, 0)),
                  full(g), full(wa), full(wb), full(cw), full(cb), full(wd), full(gf)],
        out_specs=out_spec,
        out_shape=out_shape,
        compiler_params=_cparams(("parallel", "parallel")),
        name="conv_glu",
    )(X, X, X, mod2, g, wa, wb, cw, cb, wd, gf)


def _rot_cols(w):
    s = w.shape
    x = w.reshape(s[:-1] + (s[-1] // ROPE_DIM, 4, ROPE_DIM // 4))
    r1, r2, c1, c2 = x[..., 0, :], x[..., 1, :], x[..., 2, :], x[..., 3, :]
    return jnp.stack([-r2, r1, -c2, c1], axis=-2).reshape(s)


def _rope_tables(T, C):
    t = np.arange(T)
    rowp = (t // GRID_W).astype(np.float32)
    colp = (t % GRID_W).astype(np.float32)
    half = ROPE_DIM // 2
    freqs = jnp.asarray(ROPE_THETA, F32) ** (-jnp.arange(0, half, 2, dtype=F32) / half)
    ar = jnp.asarray(rowp)[:, None] * freqs[None, :]
    ac = jnp.asarray(colp)[:, None] * freqs[None, :]
    ang = jnp.concatenate([ar, ar, ac, ac], axis=-1)
    cos = jnp.concatenate([jnp.ones((C, ROPE_DIM), F32), jnp.cos(ang)], axis=0)
    sin = jnp.concatenate([jnp.zeros((C, ROPE_DIM), F32), jnp.sin(ang)], axis=0)
    Ta = T + C
    cos256 = jnp.tile(cos, (1, 8))
    sin256 = jnp.tile(sin, (1, 8))
    one = jnp.ones((Ta, HEAD_DIM), F32)
    zero = jnp.zeros((Ta, HEAD_DIM), F32)
    cos512 = jnp.tile(jnp.concatenate([one, cos, one[:, :32]], axis=1), (1, 4))
    sin512 = jnp.tile(jnp.concatenate([zero, sin, zero[:, :32]], axis=1), (1, 4))
    return cos256, sin256, cos512, sin512


def _na_bias_tables(rpb):
    cpos = np.arange(GRID_W)
    cstart = np.clip(cpos - NA_KW // 2, 0, GRID_W - NA_KW)
    col_mask = (cpos[None, :] >= cstart[:, None]) & (cpos[None, :] < cstart[:, None] + NA_KW)
    col_idx = np.clip(cpos[None, :] - cpos[:, None] + NA_KW - 1, 0, 2 * NA_KW - 2)
    tabs = []
    for delta in range(NA_KH):
        row_off = np.arange(NA_KH) - delta + NA_KH - 1
        bias = rpb[:, :, row_off][:, :, :, col_idx]
        bias = jnp.where(jnp.asarray(col_mask)[None, None, None], bias * LOG2E, NEG_INF)
        tabs.append(bias.transpose(0, 1, 3, 2, 4).reshape(rpb.shape[0], N_HEADS, GRID_W, NA_KH * GRID_W))
    return jnp.stack(tabs, axis=1)


def kernel(x, c, ctx, c_ctx, norm1_g, norm2_g, ada_w, ada_b, w_in, w_out, na_rpb, mla_q_norm, mla_kv_norm,
           mla_w_uq, mla_w_ukv, rwkv_mu, rwkv_w0, rwkv_w_up, rwkv_a0, rwkv_a_up, rwkv_g_up, rwkv_k_k,
           rwkv_k_a, rwkv_r_k, rwkv_ln_w, rwkv_ln_b, diff_lambda, diff_subln, mlp_w_up, mlp_conv_w,
           mlp_conv_b, mlp_w_down, final_norm_g):
    B, T, D = x.shape
    C = ctx.shape[1]
    L = ada_w.shape[0]
    Ta = T + C
    tm = min(TOKEN_TILE, C)
    rows = T // GRID_W
    assert C % tm == 0 and T % tm == 0 and tm % CHUNK == 0 and rows >= NA_KH and D == 1024
    nct = C // tm
    dff = mlp_w_down.shape[1]

    wi = w_in
    na_w = wi[:, :, 0:768].at[:, :, 0:256].multiply(HEAD_DIM ** -0.5 * LOG2E)
    cq_w, ckv_w, kr_w = wi[:, :, 768:1024], wi[:, :, 1024:1152], wi[:, :, 1152:1184]
    rw_w = wi[:, :, 1184:2080]
    dq_w, dk_w, dv_w = wi[:, :, 2080:2336], wi[:, :, 2336:2592], wi[:, :, 2592:2848]
    w_all = jnp.concatenate([na_w, dq_w, _rot_cols(dq_w), dk_w, _rot_cols(dk_w), dv_w, rw_w,
                             cq_w, ckv_w, kr_w, _rot_cols(kr_w), jnp.zeros((L, D, 64), F32)],
                            axis=-1).astype(BF16)
    w_out_b = w_out.astype(BF16)
    wa_b = mlp_w_up[:, :, :dff].astype(BF16)
    wb_b = mlp_w_up[:, :, dff:].astype(BF16)
    wd_b = mlp_w_down.astype(BF16)

    uq = mla_w_uq.reshape(L, MLA_Q_RANK, N_HEADS, HEAD_DIM + ROPE_DIM)
    pad32 = jnp.zeros((L, MLA_Q_RANK, N_HEADS, 32), F32)
    wqm = jnp.concatenate([uq, pad32], axis=-1).reshape(L, MLA_Q_RANK, 512).astype(BF16)
    wqr = jnp.concatenate([jnp.zeros_like(uq[..., :HEAD_DIM]), _rot_cols(uq[..., HEAD_DIM:]), pad32],
                          axis=-1).reshape(L, MLA_Q_RANK, 512).astype(BF16)
    ukv = mla_w_ukv.reshape(L, MLA_KV_RANK, N_HEADS, 2 * HEAD_DIM)
    wk = jnp.concatenate([ukv[..., :HEAD_DIM], jnp.zeros_like(ukv[..., HEAD_DIM:])],
                         axis=-1).reshape(L, MLA_KV_RANK, 512).astype(BF16)
    wv = ukv[..., HEAD_DIM:].reshape(L, MLA_KV_RANK, 256).astype(BF16)
    pm = np.zeros((128, 1024), np.float32)
    for h in range(N_HEADS):
        for i in range(ROPE_DIM):
            pm[i, h * 128 + HEAD_DIM + i] = 1.0
            pm[ROPE_DIM + i, 512 + h * 128 + HEAD_DIM + i] = 1.0
    pmat = jnp.asarray(pm, BF16)

    zr = lambda n: jnp.zeros((L, 2, n, GROUP_W), F32)
    wup_p = jnp.concatenate([rwkv_w_up, zr(96)], axis=2).astype(BF16)
    aup_p = jnp.concatenate([zr(32), rwkv_a_up, zr(64)], axis=2).astype(BF16)
    gup_p = jnp.concatenate([jnp.zeros((L, 64, GROUP_W), F32), rwkv_g_up], axis=1).astype(BF16)

    cos256, sin256, cos512, sin512 = _rope_tables(T, C)
    na_bias = _na_bias_tables(na_rpb)
    sub256 = jnp.tile(diff_subln, (1, N_HEADS))

    R = ((B + 1 + 7) // 8) * 8
    cc = jnp.concatenate([c, c_ctx[None], jnp.zeros((R - B - 1, D), F32)], axis=0)
    mod = _modulation(cc, ada_w, ada_b)
    mod2 = jnp.stack([jnp.broadcast_to(mod[:, B:B + 1], (L, B, 6 * D)), mod[:, :B]], axis=2)
    mod2 = mod2.reshape(L, 2 * B, 1, 6 * D)

    X = jnp.concatenate([ctx, x], axis=1)
    gb = 4 if B % 4 == 0 else (2 if B % 2 == 0 else 1)
    for l in range(L):
        need_ctx = l < L - 1
        j0 = 0 if need_ctx else nct
        na_qkv, dq, dk, dv, z_rw, mq, mk, mv = _in_proj(
            X, mod2[l], norm1_g[l][None], w_all[l], cos256, sin256, cos512, sin512,
            mla_q_norm[l][None], mla_kv_norm[l][None], wqm[l], wqr[l], wk[l], wv[l], pmat, tm, nct)
        na_o = _na_attn(na_qkv, na_bias[l], tm, j0, nct, C, rows)
        mla_o = _attn_call(functools.partial(_mla_attn_kernel, j0=j0, nct=nct, C=C, Ta=Ta), "mla_attn",
                           mq, mk, mv, [], [], tm, j0)
        rvk, gbn, dirp = _rwkv_prep(z_rw, rwkv_mu[l], rwkv_w0[l], wup_p[l], rwkv_a0[l], aup_p[l], gup_p[l],
                                    rwkv_k_k[l][None], rwkv_k_a[l][None], rwkv_r_k[l].reshape(1, GROUP_W),
                                    tm, nct)
        chk = _rwkv_chunks(rvk, dirp, RWKV_TILE if Ta % RWKV_TILE == 0 else tm)
        yf, yb = _rwkv_scan(chk, C // CHUNK, gb)
        lam_init = 0.8 - 0.6 * math.exp(-0.3 * l)
        df_o = _attn_call(functools.partial(_diff_attn_kernel, j0=j0, nct=nct, C=C, Ta=Ta, lam_init=lam_init),
                          "diff_attn", dq, dk, dv, [diff_lambda[l], sub256[l][None]],
                          [pl.BlockSpec((4, DIFF_QK), lambda b, j: (0, 0)),
                           pl.BlockSpec((1, GROUP_W), lambda b, j: (0, 0))], tm, j0)
        X = _out_proj(X, mod2[l], na_o, mla_o, yf, yb, gbn, rwkv_ln_w[l][None], rwkv_ln_b[l][None], df_o,
                      w_out_b[l], tm, j0, nct)
        X = _mlp(X, mod2[l], norm2_g[l][None], wa_b[l], wb_b[l], mlp_conv_w[l], mlp_conv_b[l][None], wd_b[l],
                 final_norm_g[None], tm, j0, nct, final=not need_ctx)
    return X
```

```python
import functools
import math

import jax
import jax.numpy as jnp
import numpy as np
from jax import lax
from jax.experimental import pallas as pl
from jax.experimental.pallas import tpu as pltpu

F32 = jnp.float32
BF16 = jnp.bfloat16

GRID_W = 64
GROUP_W = 256
HEAD_DIM = 64
N_HEADS = 4
ROPE_DIM = 32
ROPE_THETA = 10000.0
NORM_EPS = 1e-6
NEG_INF = -1e30
LOG2E = math.log2(math.e)
NA_KH = 8
NA_KW = 16
MLA_Q_RANK = 256
MLA_KV_RANK = 128
RWKV_COLS = 896
RWKV_LN_EPS = 64e-5
DIFF_QK = 32
DIFF_LN_EPS = 1e-5
IN_SPLITS = (768, 1184, 2080, 2848)

W_NA = 0
W_DIFF = 768
W_RWKV = 2048
W_MLA = 2944
W_TOT = 3456

TOKEN_TILE = 256
CHUNK = 64
WIDE_TILE = 768
MLP_TILE = 384
HALO = 8
VMEM_LIMIT = 56 * 1024 * 1024


def _cparams(sem):
    return pltpu.CompilerParams(dimension_semantics=sem, vmem_limit_bytes=VMEM_LIMIT)


def _resident(a):
    return pl.BlockSpec(a.shape, lambda *_: (0,) * a.ndim, pipeline_mode=pl.Buffered(1))


def _dot(a, b):
    return jnp.dot(a.astype(BF16), b.astype(BF16), preferred_element_type=F32)


def _dot_nt(a, b):
    return lax.dot_general(a.astype(BF16), b.astype(BF16), (((1,), (1,)), ((), ())),
                           preferred_element_type=F32)


def _split2(x):
    hi = x.astype(BF16)
    lo = (x - hi.astype(F32)).astype(BF16)
    return hi, lo


def _split3(x):
    h1 = x.astype(BF16)
    r1 = x - h1.astype(F32)
    h2 = r1.astype(BF16)
    h3 = (r1 - h2.astype(F32)).astype(BF16)
    return h1, h2, h3


def _dot3(a, b):
    ah, al = _split2(a)
    bh, bl = _split2(b)
    d = functools.partial(jnp.dot, preferred_element_type=F32)
    return d(ah, bh) + (d(ah, bl) + d(al, bh))


def _sigmoid(x):
    return 1.0 / (1.0 + jnp.exp(-x))


def _softplus(x):
    return jnp.maximum(x, 0.0) + jnp.log(1.0 + jnp.exp(-jnp.abs(x)))


def _rms(x, g, eps):
    return x * lax.rsqrt(jnp.mean(x * x, axis=-1, keepdims=True) + eps) * g


def _token_rows(j, tm):
    return j * tm + lax.broadcasted_iota(jnp.int32, (tm, 1), 0)


def _mod_rows(mod_ref, is_ctx, k, D):
    return jnp.where(is_ctx, mod_ref[0][:, k * D:(k + 1) * D], mod_ref[1][:, k * D:(k + 1) * D])


def _mod_spec(D):
    return pl.BlockSpec((2, 1, 6 * D), lambda j, b: (b, 0, 0))


def _head_ones(n=GROUP_W):
    r = lax.broadcasted_iota(jnp.int32, (n, n), 0) >> 6
    c = lax.broadcasted_iota(jnp.int32, (n, n), 1) >> 6
    return jnp.where(r == c, 1.0, 0.0).astype(BF16)


def _head_sum(x, ones):
    hi, lo = _split2(x)
    d = functools.partial(jnp.dot, preferred_element_type=F32)
    return d(hi, ones) + d(lo, ones)


def _bd_mask():
    r = lax.broadcasted_iota(jnp.int32, (GROUP_W, GROUP_W), 0) >> 6
    c = lax.broadcasted_iota(jnp.int32, (GROUP_W, GROUP_W), 1) >> 6
    return r == c


def _bd(x, mask):
    xb = x.astype(BF16)
    return jnp.where(mask, jnp.concatenate([xb, xb, xb, xb], axis=0), jnp.zeros((), BF16))


def _mod_kernel(s_ref, w_ref, b_ref, o_ref):
    s = s_ref[...]
    s = s * _sigmoid(s)
    o_ref[0] = _dot3(s, w_ref[0]) + b_ref[0]


def _modulation(cc, ada_w, ada_b):
    L, D, N = ada_w.shape
    R = cc.shape[0]
    tn = 1536
    return pl.pallas_call(
        _mod_kernel,
        grid=(L, N // tn),
        in_specs=[pl.BlockSpec((R, D), lambda l, n: (0, 0)),
                  pl.BlockSpec((1, D, tn), lambda l, n: (l, 0, n)),
                  pl.BlockSpec((1, 1, tn), lambda l, n: (l, 0, n))],
        out_specs=pl.BlockSpec((1, R, tn), lambda l, n: (l, 0, n)),
        out_shape=jax.ShapeDtypeStruct((L, R, N), F32),
        compiler_params=_cparams(("parallel", "parallel")),
        name="adaln_mod",
    )(cc, ada_w, ada_b.reshape(L, 1, N))


def _in_kernel(x_ref, mod_ref, g_ref, w_ref, cos_ref, sin_ref, cos5_ref, sin5_ref,
               qn_ref, kvn_ref, wqm_ref, wqr_ref, wk_ref, wv_ref, p_ref,
               na_ref, dq_ref, dk_ref, dv_ref, rw_ref, mq_ref, mk_ref, mv_ref, *, C):
    x = x_ref[0]
    tm, D = x.shape
    is_ctx = _token_rows(pl.program_id(0), tm) < C
    h = _rms(x, g_ref[...], NORM_EPS) * (1.0 + _mod_rows(mod_ref, is_ctx, 1, D)) + _mod_rows(mod_ref, is_ctx, 0, D)
    hb = h.astype(BF16)
    dot = functools.partial(jnp.dot, preferred_element_type=F32)
    na_ref[0] = dot(hb, w_ref[:, W_NA:W_DIFF]).astype(BF16)
    d = dot(hb, w_ref[:, W_DIFF:W_RWKV])
    cos = cos_ref[...]
    sin = sin_ref[...]
    scale = DIFF_QK ** -0.5 * LOG2E
    dq_ref[0] = ((d[:, 0:256] * cos + d[:, 256:512] * sin) * scale).astype(BF16)
    dk_ref[0] = (d[:, 512:768] * cos + d[:, 768:1024] * sin).astype(BF16)
    dv_ref[0] = _with_ones(d[:, 1024:1280])
    rw_ref[0] = dot(hb, w_ref[:, W_RWKV:W_MLA])
    z = dot(hb, w_ref[:, W_MLA:W_TOT])
    cos5 = cos5_ref[...]
    sin5 = sin5_ref[...]
    nq = _rms(z[:, 0:256], qn_ref[...], NORM_EPS).astype(BF16)
    mscale = (HEAD_DIM + ROPE_DIM) ** -0.5 * LOG2E
    mq_ref[0] = ((dot(nq, wqm_ref[...]) * cos5 + dot(nq, wqr_ref[...]) * sin5) * mscale).astype(BF16)
    nkv = _rms(z[:, 256:384], kvn_ref[...], NORM_EPS).astype(BF16)
    kr = dot(z[:, 384:512].astype(BF16), p_ref[...])
    mk_ref[0] = (dot(nkv, wk_ref[...]) + kr[:, 0:512] * cos5 + kr[:, 512:1024] * sin5).astype(BF16)
    mv_ref[0] = _with_ones(dot(nkv, wv_ref[...]))


def _in_proj(X, mod2, g, w, cos256, sin256, cos512, sin512, qn, kvn, wqm, wqr, wk, wv, pmat, tm, C):
    B, Ta, D = X.shape
    nt = Ta // tm
    tok = lambda n: pl.BlockSpec((1, tm, n), lambda j, b: (b, j, 0))
    tab = lambda n: pl.BlockSpec((tm, n), lambda j, b: (j, 0))
    full = _resident
    return pl.pallas_call(
        functools.partial(_in_kernel, C=C),
        grid=(nt, B),
        in_specs=[tok(D), _mod_spec(D),
                  full(g), full(w), tab(256), tab(256), tab(512), tab(512),
                  full(qn), full(kvn), full(wqm), full(wqr), full(wk), full(wv), full(pmat)],
        out_specs=[tok(768), tok(256), tok(256), tok(512), tok(RWKV_COLS), tok(512), tok(512), tok(512)],
        out_shape=[jax.ShapeDtypeStruct((B, Ta, 768), BF16),
                   jax.ShapeDtypeStruct((B, Ta, 256), BF16),
                   jax.ShapeDtypeStruct((B, Ta, 256), BF16),
                   jax.ShapeDtypeStruct((B, Ta, 512), BF16),
                   jax.ShapeDtypeStruct((B, Ta, RWKV_COLS), F32),
                   jax.ShapeDtypeStruct((B, Ta, 512), BF16),
                   jax.ShapeDtypeStruct((B, Ta, 512), BF16),
                   jax.ShapeDtypeStruct((B, Ta, 512), BF16)],
        compiler_params=_cparams(("parallel", "parallel")),
        name="in_proj",
    )(X, mod2, g, w, cos256, sin256, cos512, sin512, qn, kvn, wqm, wqr, wk, wv, pmat)


def _softmax_parts(s):
    m = jnp.max(s, axis=-1, keepdims=True)
    e = jnp.exp2(s - m)
    return e, jnp.sum(e, axis=-1, keepdims=True)


def _with_ones(v):
    one = jnp.ones((v.shape[0], 128), F32)
    return jnp.concatenate([v[:, 0:128], one, v[:, 128:256], one], axis=1).astype(BF16)


def _attend(s, vp1):
    sb = s.astype(BF16)
    e = jnp.exp2(sb - jnp.max(sb, axis=-1, keepdims=True))
    r = jnp.dot(e, vp1, preferred_element_type=F32)
    return r[:, 0:128] / r[:, 128:256]


def _mla_attn_body(q, k_ref, v_ref, nk):
    tq = q.shape[0]
    lane = lax.broadcasted_iota(jnp.int32, (tq, 128), 1)
    outs = []
    for pair in range(2):
        vp1 = v_ref[0, 0:nk, pair * 256:(pair + 1) * 256]
        o2 = [_attend(_dot_nt(q[:, h * 128:(h + 1) * 128], k_ref[0, 0:nk, h * 128:(h + 1) * 128]), vp1)
              for h in (2 * pair, 2 * pair + 1)]
        outs.append(jnp.where(lane < 64, o2[0], o2[1]))
    return jnp.concatenate(outs, axis=1)


def _mla_attn_kernel(q_ref, k_ref, v_ref, o_ref, *, j0, nct, C, Ta):
    j = pl.program_id(1) + j0
    q = q_ref[0]
    if j0 < nct:
        @pl.when(j < nct)
        def _():
            o_ref[0] = _mla_attn_body(q, k_ref, v_ref, C).astype(BF16)

    @pl.when(j >= nct)
    def _():
        o_ref[0] = _mla_attn_body(q, k_ref, v_ref, Ta).astype(BF16)


def _attn_call(kern, name, q, k, v, extra, extra_specs, tq, j0, out_w=256):
    B, Ta, _ = q.shape
    nq = Ta // tq - j0
    return pl.pallas_call(
        kern,
        grid=(B, nq),
        in_specs=[pl.BlockSpec((1, tq, q.shape[-1]), lambda b, j: (b, j + j0, 0)),
                  pl.BlockSpec((1, Ta, k.shape[-1]), lambda b, j: (b, 0, 0)),
                  pl.BlockSpec((1, Ta, v.shape[-1]), lambda b, j: (b, 0, 0))] + extra_specs,
        out_specs=pl.BlockSpec((1, tq, out_w), lambda b, j: (b, j + j0, 0)),
        out_shape=jax.ShapeDtypeStruct((B, Ta, out_w), BF16),
        compiler_params=_cparams(("parallel", "arbitrary")),
        name=name,
    )(q, k, v, *extra)


def _diff_body(q, k_ref, v_ref, nk, lam):
    tq = q.shape[0]
    k = k_ref[0, 0:nk, :]
    sub = lax.broadcasted_iota(jnp.int32, (tq, GROUP_W), 1) >> 5
    lane = lax.broadcasted_iota(jnp.int32, (tq, 128), 1)
    zero = jnp.zeros((), BF16)
    outs = []
    for pair in range(2):
        vp1 = v_ref[0, 0:nk, pair * 256:(pair + 1) * 256]
        o2 = []
        for hh in range(2):
            h = 2 * pair + hh
            o2.append(_attend(_dot_nt(jnp.where(sub == 2 * h, q, zero), k), vp1)
                      - lam * _attend(_dot_nt(jnp.where(sub == 2 * h + 1, q, zero), k), vp1))
        outs.append(jnp.where(lane < 64, o2[0], o2[1]))
    return jnp.concatenate(outs, axis=1)


def _diff_attn_kernel(q_ref, k_ref, v_ref, lam_ref, sub_ref, o_ref, *, j0, nct, C, Ta, lam_init):
    j = pl.program_id(1) + j0
    q = q_ref[0]
    lp = lam_ref[...]
    lam = (jnp.exp(jnp.sum(lp[0:1] * lp[1:2], axis=-1, keepdims=True))
           - jnp.exp(jnp.sum(lp[2:3] * lp[3:4], axis=-1, keepdims=True)) + lam_init)
    ones = _head_ones()

    def finish(o):
        ms = _head_sum(o * o, ones) * (1.0 / HEAD_DIM)
        return (o * lax.rsqrt(ms + DIFF_LN_EPS) * sub_ref[...]) * (1.0 - lam_init)

    if j0 < nct:
        @pl.when(j < nct)
        def _():
            o_ref[0] = finish(_diff_body(q, k_ref, v_ref, C, lam)).astype(BF16)

    @pl.when(j >= nct)
    def _():
        o_ref[0] = finish(_diff_body(q, k_ref, v_ref, Ta, lam)).astype(BF16)


def _na_kernel(qt_ref, kv_ref, bias_ref, o_ref, *, j0, nct, C, rows, R):
    j = pl.program_id(1) + j0
    tq = R * GRID_W
    zero = jnp.zeros((), BF16)

    if j0 < nct:
        @pl.when(j < nct)
        def _():
            lane = lax.broadcasted_iota(jnp.int32, (tq, 128), 1)
            outs = []
            for pair in range(2):
                q2 = qt_ref[0, :, pair * 128:(pair + 1) * 128]
                kc = kv_ref[0, 0:C, 256 + pair * 128:256 + (pair + 1) * 128]
                vc = kv_ref[0, 0:C, 512 + pair * 128:512 + (pair + 1) * 128]
                o2 = []
                for hh in range(2):
                    qm = jnp.where((lane < 64) if hh == 0 else (lane >= 64), q2, zero)
                    e, l = _softmax_parts(_dot_nt(qm, kc))
                    o2.append(jnp.dot(e.astype(BF16), vc, preferred_element_type=F32) * (1.0 / l))
                outs.append(jnp.where(lane < 64, o2[0], o2[1]))
            o_ref[0] = jnp.concatenate(outs, axis=1).astype(BF16)

    @pl.when(j >= nct)
    def _():
        lane = lax.broadcasted_iota(jnp.int32, (GRID_W, 128), 1)
        W = NA_KH * GRID_W
        dot = functools.partial(jnp.dot, preferred_element_type=F32)
        starts, deltas = [], []
        for rr in range(R):
            r = (j - nct) * R + rr
            rs = jnp.clip(r - NA_KH // 2, 0, rows - NA_KH)
            deltas.append(r - rs)
            starts.append(pl.multiple_of(C + rs * GRID_W, GRID_W))
        units = [(rr, pair) for pair in range(2) for rr in range(R)]
        qs = {}
        for rr, pair in units:
            q2 = qt_ref[0, rr * GRID_W:(rr + 1) * GRID_W, pair * 128:(pair + 1) * 128]
            qs[rr, pair] = jnp.concatenate([jnp.where(lane < 64, q2, zero), jnp.where(lane >= 64, q2, zero)], axis=0)
        s_ctx = {}
        for pair in range(2):
            kc = kv_ref[0, 0:C, 256 + pair * 128:256 + (pair + 1) * 128]
            sc = _dot_nt(jnp.concatenate([qs[rr, pair] for rr in range(R)], axis=0), kc)
            for rr in range(R):
                s_ctx[rr, pair] = sc[rr * 128:(rr + 1) * 128]
        s_lat = {}
        for rr, pair in units:
            kw = kv_ref[0, pl.ds(starts[rr], W), 256 + pair * 128:256 + (pair + 1) * 128]
            bias = jnp.concatenate([bias_ref[deltas[rr], 2 * pair], bias_ref[deltas[rr], 2 * pair + 1]], axis=0)
            s_lat[rr, pair] = _dot_nt(qs[rr, pair], kw) + bias
        e_lat, e_ctx, inv = {}, {}, {}
        for u in units:
            m = jnp.maximum(jnp.max(s_lat[u], axis=-1, keepdims=True), jnp.max(s_ctx[u], axis=-1, keepdims=True))
            e_lat[u] = jnp.exp2(s_lat[u] - m)
            ec = jnp.exp2(s_ctx[u] - m)
            e_ctx[u] = ec.astype(BF16)
            inv[u] = 1.0 / (jnp.sum(e_lat[u], axis=-1, keepdims=True) + jnp.sum(ec, axis=-1, keepdims=True))
        o_ctx = {}
        for pair in range(2):
            vc = kv_ref[0, 0:C, 512 + pair * 128:512 + (pair + 1) * 128]
            oc = dot(jnp.concatenate([e_ctx[rr, pair] for rr in range(R)], axis=0), vc)
            for rr in range(R):
                o_ctx[rr, pair] = oc[rr * 128:(rr + 1) * 128]
        o = {}
        for rr, pair in units:
            vw = kv_ref[0, pl.ds(starts[rr], W), 512 + pair * 128:512 + (pair + 1) * 128]
            ou = (dot(e_lat[rr, pair].astype(BF16), vw) + o_ctx[rr, pair]) * inv[rr, pair]
            o[rr, pair] = jnp.where(lane < 64, ou[0:GRID_W], ou[GRID_W:])
        o_ref[0] = jnp.concatenate([jnp.concatenate([o[rr, 0], o[rr, 1]], axis=1) for rr in range(R)],
                                   axis=0).astype(BF16)


def _na_attn(qkv, bias, tq, j0, nct, C, rows):
    B, Ta, _ = qkv.shape
    nq = Ta // tq - j0
    kern = functools.partial(_na_kernel, j0=j0, nct=nct, C=C, rows=rows, R=tq // GRID_W)
    return pl.pallas_call(
        kern,
        grid=(B, nq),
        in_specs=[pl.BlockSpec((1, tq, 768), lambda b, j: (b, j + j0, 0)),
                  pl.BlockSpec((1, Ta, 768), lambda b, j: (b, 0, 0)),
                  _resident(bias)],
        out_specs=pl.BlockSpec((1, tq, 256), lambda b, j: (b, j + j0, 0)),
        out_shape=jax.ShapeDtypeStruct((B, Ta, 256), BF16),
        compiler_params=_cparams(("parallel", "arbitrary")),
        name="na_attn",
    )(qkv, qkv, bias)


def _rwkv_prep_kernel(z_ref, zp_ref, zn_ref, mu_ref, w0_ref, wup_ref, a0_ref, aup_ref, gup_ref,
                      kk_ref, ka_ref, rk_ref, rvk_ref, gb_ref, dir_ref, *, C, Ta):
    tm = z_ref.shape[1]
    z = z_ref[0]
    row = lax.broadcasted_iota(jnp.int32, (tm, 1), 0)
    pos = _token_rows(pl.program_id(0), tm)
    has_prev = jnp.where((pos == 0) | (pos == C), 0.0, 1.0)
    has_next = jnp.where((pos == C - 1) | (pos == Ta - 1), 0.0, 1.0)
    prev = jnp.where(row == 0, zp_ref[0, HALO - 1:HALO, :], pltpu.roll(z, 1, axis=0)) * has_prev
    nxt = jnp.where(row == tm - 1, zn_ref[0, 0:1, :], pltpu.roll(z, tm - 1, axis=0)) * has_next
    zs = z + mu_ref[0:1, :] * (prev - z) + mu_ref[1:2, :] * (nxt - z)
    r = zs[:, 0:256]
    k = zs[:, 256:512]
    v = zs[:, 512:768]
    low = zs[:, 768:896]
    ones = _head_ones()
    kk = k * kk_ref[...]
    kk = kk * lax.rsqrt(jnp.maximum(_head_sum(kk * kk, ones), 1e-24))
    g = _dot(_sigmoid(low), gup_ref[...])
    bonus = _head_sum(r * k * rk_ref[...], ones) * v
    rvk_ref[0, :, 0:256] = r
    rvk_ref[0, :, 256:512] = v
    rvk_ref[0, :, 512:768] = kk
    gb_ref[0, :, 0:256] = g
    gb_ref[0, :, 256:512] = bonus
    tl = jnp.tanh(low).astype(BF16)
    lb = low.astype(BF16)
    for d in range(2):
        w = -_softplus(-(w0_ref[d:d + 1, :] + jnp.dot(tl, wup_ref[d], preferred_element_type=F32))) - 0.5
        a = _sigmoid(a0_ref[d:d + 1, :] + jnp.dot(lb, aup_ref[d], preferred_element_type=F32))
        dir_ref[d, 0, :, 0:256] = -jnp.exp(w)
        dir_ref[d, 0, :, 256:512] = k * (1.0 + (a - 1.0) * ka_ref[...])
        dir_ref[d, 0, :, 512:768] = kk * a


def _rwkv_prep(z, mu, w0, wup_p, a0, aup_p, gup_p, k_k, k_a, r_k, tm, C):
    B, Ta, _ = z.shape
    nt = Ta // tm
    hb = tm // HALO
    nh = Ta // HALO
    tok = lambda n: pl.BlockSpec((1, tm, n), lambda j, b: (b, j, 0))
    full = _resident
    kern = functools.partial(_rwkv_prep_kernel, C=C, Ta=Ta)
    return pl.pallas_call(
        kern,
        grid=(nt, B),
        in_specs=[tok(RWKV_COLS),
                  pl.BlockSpec((1, HALO, RWKV_COLS), lambda j, b: (b, jnp.maximum(j * hb - 1, 0), 0)),
                  pl.BlockSpec((1, HALO, RWKV_COLS), lambda j, b: (b, jnp.minimum((j + 1) * hb, nh - 1), 0)),
                  full(mu), full(w0), full(wup_p), full(a0), full(aup_p), full(gup_p),
                  full(k_k), full(k_a), full(r_k)],
        out_specs=[tok(768), tok(512),
                   pl.BlockSpec((2, 1, tm, 768), lambda j, b: (0, b, j, 0))],
        out_shape=[jax.ShapeDtypeStruct((B, Ta, 768), F32),
                   jax.ShapeDtypeStruct((B, Ta, 512), F32),
                   jax.ShapeDtypeStruct((2, B, Ta, 768), F32)],
        compiler_params=_cparams(("parallel", "parallel")),
        name="rwkv_prep",
    )(z, z, z, mu, w0, wup_p, a0, aup_p, gup_p, k_k, k_a, r_k)


def _rwkv_chunk_kernel(rvk_ref, dir_ref, o_ref):
    d = pl.program_id(0)
    nchunk = rvk_ref.shape[1] // CHUNK
    bdm = _bd_mask()
    row = lax.broadcasted_iota(jnp.int32, (CHUNK, GROUP_W), 0)
    col = lax.broadcasted_iota(jnp.int32, (CHUNK, GROUP_W), 1) & (CHUNK - 1)
    tdiff = jnp.where(d == 0, row - col, col - row)
    incl = tdiff >= 0
    strict = tdiff > 0
    eye = tdiff == 0
    r2 = lax.broadcasted_iota(jnp.int32, (CHUNK, CHUNK), 0)
    c2 = lax.broadcasted_iota(jnp.int32, (CHUNK, CHUNK), 1)
    tri = jnp.where(jnp.where(d == 0, r2 - c2, c2 - r2) >= 0, 1.0, 0.0).astype(BF16)
    dot = functools.partial(jnp.dot, preferred_element_type=F32)

    def hmul(x, y):
        return dot(x.astype(BF16), _bd(y, bdm))

    def hmul_t(x, yt):
        ytb = yt.astype(BF16)
        return dot(x.astype(BF16), jnp.where(bdm, jnp.concatenate([ytb, ytb, ytb, ytb], axis=1),
                                             jnp.zeros((), BF16)))

    def fold(full):
        full = jnp.where(bdm, full, 0.0)
        return (full[0:64] + full[64:128]) + (full[128:192] + full[192:256])

    cs = range(nchunk)
    sls = [slice(c * CHUNK, (c + 1) * CHUNK) for c in cs]
    r = [rvk_ref[0, sl, 0:256] for sl in sls]
    v = [rvk_ref[0, sl, 256:512] for sl in sls]
    kk = [rvk_ref[0, sl, 512:768] for sl in sls]
    lw = [dir_ref[0, 0, sl, 0:256] for sl in sls]
    kd = [dir_ref[0, 0, sl, 256:512] for sl in sls]
    b = [dir_ref[0, 0, sl, 512:768] for sl in sls]
    cum, tot = [], []
    for c in cs:
        l1, l2, l3 = _split3(lw[c])
        cum.append(dot(tri, l1) + (dot(tri, l2) + dot(tri, l3)))
        tot.append(jnp.sum(lw[c], axis=0, keepdims=True))
    rt = [r[c] * jnp.exp(cum[c]) for c in cs]
    at = [-kk[c] * jnp.exp(cum[c] - lw[c]) for c in cs]
    einv = [jnp.exp(-cum[c]) for c in cs]
    eend = [jnp.exp(tot[c] - cum[c]) for c in cs]
    ar = [jnp.concatenate([at[c], rt[c]], axis=0) for c in cs]
    xb = [hmul_t(ar[c], (b[c] * einv[c]).T) for c in cs]
    xk = [hmul_t(ar[c], (kd[c] * einv[c]).T) for c in cs]
    n = [jnp.where(strict, xb[c][0:CHUNK], 0.0) for c in cs]
    lrb = [jnp.where(incl, xb[c][CHUNK:], 0.0) for c in cs]
    al = [jnp.concatenate([jnp.where(strict, xk[c][0:CHUNK], 0.0),
                           jnp.where(incl, xk[c][CHUNK:], 0.0)], axis=0) for c in cs]
    akv_lrkv = [hmul(al[c], v[c]) for c in cs]
    p = [jnp.where(eye, 1.0, 0.0) + n[c] for c in cs]
    npow = [hmul(n[c], n[c]) for c in cs]
    for _ in range(4):
        sq = [hmul(jnp.concatenate([npow[c], p[c]], axis=0), npow[c]) for c in cs]
        npow = [sq[c][0:CHUNK] for c in cs]
        p = [p[c] + sq[c][CHUNK:] for c in cs]
    t = [p[c] + hmul(p[c], npow[c]) for c in cs]
    w = [hmul(t[c], at[c]) for c in cs]
    u0 = [hmul(t[c], akv_lrkv[c][0:CHUNK]) for c in cs]
    qh = [rt[c] + hmul(lrb[c], w[c]) for c in cs]
    y0 = [hmul(lrb[c], u0[c]) + akv_lrkv[c][CHUNK:] for c in cs]
    outs = []
    for c in cs:
        bht = (b[c] * eend[c]).T.astype(BF16)
        kht = (kd[c] * eend[c]).T.astype(BF16)
        g = fold(dot(bht, w[c].astype(BF16))) + jnp.where(eye, jnp.exp(tot[c]), 0.0)
        hm = fold(dot(bht, u0[c].astype(BF16)) + dot(kht, v[c].astype(BF16)))
        outs.append(jnp.concatenate([qh[c], y0[c], g, hm], axis=1))
    o_ref[0, 0] = jnp.concatenate(outs, axis=0)


def _rwkv_chunks(rvk, dirp, tm):
    B, Ta, _ = rvk.shape
    nt = Ta // tm
    return pl.pallas_call(
        _rwkv_chunk_kernel,
        grid=(2, B, nt),
        in_specs=[pl.BlockSpec((1, tm, 768), lambda d, b, j: (b, j, 0)),
                  pl.BlockSpec((1, 1, tm, 768), lambda d, b, j: (d, b, j, 0))],
        out_specs=pl.BlockSpec((1, 1, tm, 1024), lambda d, b, j: (d, b, j, 0)),
        out_shape=jax.ShapeDtypeStruct((2, B, Ta, 1024), F32),
        compiler_params=_cparams(("parallel", "parallel", "parallel")),
        name="rwkv_chunks",
    )(rvk, dirp)


def _rwkv_scan_kernel(f_ref, b_ref, yf_ref, yb_ref, s_ref):
    i = pl.program_id(1)
    gb = f_ref.shape[1]
    bdm = _bd_mask()

    @pl.when(i == 0)
    def _():
        s_ref[...] = jnp.zeros(s_ref.shape, F32)

    dot = functools.partial(jnp.dot, preferred_element_type=F32)
    zero = jnp.zeros((), BF16)
    chains = [(d, bb) for d in range(2) for bb in range(gb)]
    c_refs = (f_ref, b_ref)
    parts = {}
    for d, bb in chains:
        sh, sl = _split2(s_ref[d, bb])
        lh, ll = _split2(jnp.concatenate([c_refs[d][0, bb, :, 0:256], c_refs[d][0, bb, :, 512:768]], axis=0))
        parts[d, bb] = (lh, ll, jnp.where(bdm, jnp.concatenate([sh] * 4, axis=0), zero),
                        jnp.where(bdm, jnp.concatenate([sl] * 4, axis=0), zero))
    res = {}
    for ch in chains:
        lh, ll, sbh, sbl = parts[ch]
        res[ch] = dot(lh, sbh) + (dot(lh, sbl) + dot(ll, sbh))
    for d, y_ref in enumerate((yf_ref, yb_ref)):
        for bb in range(gb):
            y_ref[bb] = res[d, bb][0:CHUNK] + c_refs[d][0, bb, :, 256:512]
            s_ref[d, bb] = res[d, bb][CHUNK:] + c_refs[d][0, bb, :, 768:1024]


def _rwkv_scan(chk, ncc, gb):
    _, B, Ta, _ = chk.shape
    nc = Ta // CHUNK

    def rev_chunk(i):
        return jnp.where(i < ncc, ncc - 1 - i, nc - 1 - (i - ncc))

    return pl.pallas_call(
        _rwkv_scan_kernel,
        grid=(B // gb, nc),
        in_specs=[pl.BlockSpec((1, gb, CHUNK, 1024), lambda b, i: (0, b, i, 0)),
                  pl.BlockSpec((1, gb, CHUNK, 1024), lambda b, i: (1, b, rev_chunk(i), 0))],
        out_specs=[pl.BlockSpec((gb, CHUNK, 256), lambda b, i: (b, i, 0)),
                   pl.BlockSpec((gb, CHUNK, 256), lambda b, i: (b, rev_chunk(i), 0))],
        out_shape=[jax.ShapeDtypeStruct((B, Ta, 256), F32),
                   jax.ShapeDtypeStruct((B, Ta, 256), F32)],
        scratch_shapes=[pltpu.VMEM((2, gb, CHUNK, GROUP_W), F32)],
        compiler_params=_cparams(("parallel", "arbitrary")),
        name="rwkv_scan",
    )(chk, chk)


def _out_kernel(x_ref, mod_ref, na_ref, mla_ref, yf_ref, yb_ref, gb_ref, lnw_ref, lnb_ref, df_ref, w_ref, o_ref,
                *, j0, C):
    x = x_ref[0]
    tm, D = x.shape
    is_ctx = _token_rows(pl.program_id(0) + j0, tm) < C
    dot = functools.partial(jnp.dot, preferred_element_type=F32)
    ones = _head_ones()
    y = yf_ref[0] + yb_ref[0]
    mean = _head_sum(y, ones) * (1.0 / HEAD_DIM)
    yc = y - mean
    var = _head_sum(yc * yc, ones) * (1.0 / HEAD_DIM)
    yn = yc * lax.rsqrt(var + RWKV_LN_EPS) * lnw_ref[...] + lnb_ref[...]
    rw = ((yn + gb_ref[0, :, 256:512]) * gb_ref[0, :, 0:256]).astype(BF16)
    mix = (dot(na_ref[0], w_ref[0:256, :]) + dot(mla_ref[0], w_ref[256:512, :])) + \
          (dot(rw, w_ref[512:768, :]) + dot(df_ref[0], w_ref[768:1024, :]))
    o_ref[0] = x + _mod_rows(mod_ref, is_ctx, 2, D) * mix


def _out_proj(X, mod2, na_o, mla_o, yf, yb, gbn, ln_w, ln_b, df_o, w, tm, j0, C):
    B, Ta, D = X.shape
    nt = Ta // tm - j0
    tok = lambda n: pl.BlockSpec((1, tm, n), lambda j, b: (b, j + j0, 0))
    full = _resident
    return pl.pallas_call(
        functools.partial(_out_kernel, j0=j0, C=C),
        grid=(nt, B),
        in_specs=[tok(D), _mod_spec(D),
                  tok(256), tok(256), tok(256), tok(256), tok(512), full(ln_w), full(ln_b), tok(256),
                  full(w)],
        out_specs=tok(D),
        out_shape=jax.ShapeDtypeStruct((B, Ta, D), F32),
        compiler_params=_cparams(("parallel", "parallel")),
        name="out_proj",
    )(X, mod2, na_o, mla_o, yf, yb, gbn, ln_w, ln_b, df_o, w)


def _mlp_kernel(x_ref, xp_ref, xn_ref, mod_ref, g_ref, wa_ref, wb_ref, cw_ref, cb_ref, wd_ref, gf_ref, o_ref,
                *, j0, C, Ta, fc, final):
    j = pl.program_id(0) + j0
    tm = x_ref.shape[1]
    D = x_ref.shape[2]
    dff = wa_ref.shape[1]
    x = x_ref[0]
    xe = jnp.concatenate([xp_ref[0], x, xn_ref[0]], axis=0)
    pos = _token_rows(j, tm)
    ctx_e = (j * tm - HALO + lax.broadcasted_iota(jnp.int32, (tm + 2 * HALO, 1), 0)) < C
    h = (_rms(xe, g_ref[...], NORM_EPS) * (1.0 + _mod_rows(mod_ref, ctx_e, 4, D))
         + _mod_rows(mod_ref, ctx_e, 3, D)).astype(BF16)
    pmask = jnp.where((pos == 0) | (pos == C), 0.0, 1.0)
    nmask = jnp.where((pos == C - 1) | (pos == Ta - 1), 0.0, 1.0)
    dot = functools.partial(jnp.dot, preferred_element_type=F32)
    acc = jnp.zeros((tm, D), F32)
    for c in range(dff // fc):
        cs = slice(c * fc, (c + 1) * fc)
        a = dot(h, wa_ref[:, cs])
        b = dot(h[HALO:HALO + tm], wb_ref[:, cs])
        cv = (cw_ref[0:1, cs] * (a[HALO - 1:HALO - 1 + tm] * pmask) + cw_ref[1:2, cs] * a[HALO:HALO + tm]
              + cw_ref[2:3, cs] * (a[HALO + 1:HALO + 1 + tm] * nmask) + cb_ref[:, cs])
        u = cv * _sigmoid(cv) * b
        acc = acc + dot(u.astype(BF16), wd_ref[cs, :])
    y = x + _mod_rows(mod_ref, pos < C, 5, D) * acc
    o_ref[0] = _rms(y, gf_ref[...], NORM_EPS) if final else y


def _mlp(X, mod2, g, wa, wb, cw, cb, wd, gf, tm, j0, C, final):
    B, Ta, D = X.shape
    ntot = Ta // tm
    nt = ntot - j0
    hb = tm // HALO
    nh = Ta // HALO
    tok = lambda n: pl.BlockSpec((1, tm, n), lambda j, b: (b, j + j0, 0))
    full = _resident
    kern = functools.partial(_mlp_kernel, j0=j0, C=C, Ta=Ta, fc=wa.shape[1], final=final)
    if final:
        assert j0 * tm == C
        out_spec = pl.BlockSpec((1, tm, D), lambda j, b: (b, j, 0))
        out_shape = jax.ShapeDtypeStruct((B, Ta - C, D), F32)
    else:
        out_spec, out_shape = tok(D), jax.ShapeDtypeStruct((B, Ta, D), F32)
    return pl.pallas_call(
        kern,
        grid=(nt, B),
        in_specs=[tok(D),
                  pl.BlockSpec((1, HALO, D), lambda j, b: (b, jnp.maximum((j + j0) * hb - 1, j0 * hb), 0)),
                  pl.BlockSpec((1, HALO, D), lambda j, b: (b, jnp.minimum((j + j0 + 1) * hb, nh - 1), 0)),
                  _mod_spec(D),
                  full(g), full(wa), full(wb), full(cw), full(cb), full(wd), full(gf)],
        out_specs=out_spec,
        out_shape=out_shape,
        compiler_params=_cparams(("parallel", "parallel")),
        name="conv_glu",
    )(X, X, X, mod2, g, wa, wb, cw, cb, wd, gf)


def _rot_cols(w):
    s = w.shape
    x = w.reshape(s[:-1] + (s[-1] // ROPE_DIM, 4, ROPE_DIM // 4))
    r1, r2, c1, c2 = x[..., 0, :], x[..., 1, :], x[..., 2, :], x[..., 3, :]
    return jnp.stack([-r2, r1, -c2, c1], axis=-2).reshape(s)


def _rope_tables(T, C):
    t = np.arange(T)
    rowp = (t // GRID_W).astype(np.float32)
    colp = (t % GRID_W).astype(np.float32)
    half = ROPE_DIM // 2
    freqs = jnp.asarray(ROPE_THETA, F32) ** (-jnp.arange(0, half, 2, dtype=F32) / half)
    ar = jnp.asarray(rowp)[:, None] * freqs[None, :]
    ac = jnp.asarray(colp)[:, None] * freqs[None, :]
    ang = jnp.concatenate([ar, ar, ac, ac], axis=-1)
    cos = jnp.concatenate([jnp.ones((C, ROPE_DIM), F32), jnp.cos(ang)], axis=0)
    sin = jnp.concatenate([jnp.zeros((C, ROPE_DIM), F32), jnp.sin(ang)], axis=0)
    Ta = T + C
    cos256 = jnp.tile(cos, (1, 8))
    sin256 = jnp.tile(sin, (1, 8))
    one = jnp.ones((Ta, HEAD_DIM), F32)
    zero = jnp.zeros((Ta, HEAD_DIM), F32)
    cos512 = jnp.tile(jnp.concatenate([one, cos, one[:, :32]], axis=1), (1, 4))
    sin512 = jnp.tile(jnp.concatenate([zero, sin, zero[:, :32]], axis=1), (1, 4))
    return cos256, sin256, cos512, sin512


def _na_bias_tables(rpb):
    cpos = np.arange(GRID_W)
    cstart = np.clip(cpos - NA_KW // 2, 0, GRID_W - NA_KW)
    col_mask = (cpos[None, :] >= cstart[:, None]) & (cpos[None, :] < cstart[:, None] + NA_KW)
    col_idx = np.clip(cpos[None, :] - cpos[:, None] + NA_KW - 1, 0, 2 * NA_KW - 2)
    tabs = []
    for delta in range(NA_KH):
        row_off = np.arange(NA_KH) - delta + NA_KH - 1
        bias = rpb[:, :, row_off][:, :, :, col_idx]
        bias = jnp.where(jnp.asarray(col_mask)[None, None, None], bias * LOG2E, NEG_INF)
        tabs.append(bias.transpose(0, 1, 3, 2, 4).reshape(rpb.shape[0], N_HEADS, GRID_W, NA_KH * GRID_W))
    return jnp.stack(tabs, axis=1)


def kernel(x, c, ctx, c_ctx, norm1_g, norm2_g, ada_w, ada_b, w_in, w_out, na_rpb, mla_q_norm, mla_kv_norm,
           mla_w_uq, mla_w_ukv, rwkv_mu, rwkv_w0, rwkv_w_up, rwkv_a0, rwkv_a_up, rwkv_g_up, rwkv_k_k,
           rwkv_k_a, rwkv_r_k, rwkv_ln_w, rwkv_ln_b, diff_lambda, diff_subln, mlp_w_up, mlp_conv_w,
           mlp_conv_b, mlp_w_down, final_norm_g):
    B, T, D = x.shape
    C = ctx.shape[1]
    L = ada_w.shape[0]
    Ta = T + C
    tm = min(TOKEN_TILE, C)
    rows = T // GRID_W
    assert C % tm == 0 and T % tm == 0 and tm % CHUNK == 0 and rows >= NA_KH and D == 1024
    nct = C // tm
    t_wide = WIDE_TILE if Ta % WIDE_TILE == 0 else tm
    t_mlp = MLP_TILE if Ta % MLP_TILE == 0 else tm
    dff = mlp_w_down.shape[1]

    wi = w_in
    na_w = wi[:, :, 0:768].at[:, :, 0:256].multiply(HEAD_DIM ** -0.5 * LOG2E)
    cq_w, ckv_w, kr_w = wi[:, :, 768:1024], wi[:, :, 1024:1152], wi[:, :, 1152:1184]
    rw_w = wi[:, :, 1184:2080]
    dq_w, dk_w, dv_w = wi[:, :, 2080:2336], wi[:, :, 2336:2592], wi[:, :, 2592:2848]
    w_all = jnp.concatenate([na_w, dq_w, _rot_cols(dq_w), dk_w, _rot_cols(dk_w), dv_w, rw_w,
                             cq_w, ckv_w, kr_w, _rot_cols(kr_w), jnp.zeros((L, D, 64), F32)],
                            axis=-1).astype(BF16)
    w_out_b = w_out.astype(BF16)
    wa_b = mlp_w_up[:, :, :dff].astype(BF16)
    wb_b = mlp_w_up[:, :, dff:].astype(BF16)
    wd_b = mlp_w_down.astype(BF16)

    uq = mla_w_uq.reshape(L, MLA_Q_RANK, N_HEADS, HEAD_DIM + ROPE_DIM)
    pad32 = jnp.zeros((L, MLA_Q_RANK, N_HEADS, 32), F32)
    wqm = jnp.concatenate([uq, pad32], axis=-1).reshape(L, MLA_Q_RANK, 512).astype(BF16)
    wqr = jnp.concatenate([jnp.zeros_like(uq[..., :HEAD_DIM]), _rot_cols(uq[..., HEAD_DIM:]), pad32],
                          axis=-1).reshape(L, MLA_Q_RANK, 512).astype(BF16)
    ukv = mla_w_ukv.reshape(L, MLA_KV_RANK, N_HEADS, 2 * HEAD_DIM)
    wk = jnp.concatenate([ukv[..., :HEAD_DIM], jnp.zeros_like(ukv[..., HEAD_DIM:])],
                         axis=-1).reshape(L, MLA_KV_RANK, 512).astype(BF16)
    wv = ukv[..., HEAD_DIM:].reshape(L, MLA_KV_RANK, 256).astype(BF16)
    pm = np.zeros((128, 1024), np.float32)
    for h in range(N_HEADS):
        for i in range(ROPE_DIM):
            pm[i, h * 128 + HEAD_DIM + i] = 1.0
            pm[ROPE_DIM + i, 512 + h * 128 + HEAD_DIM + i] = 1.0
    pmat = jnp.asarray(pm, BF16)

    zr = lambda n: jnp.zeros((L, 2, n, GROUP_W), F32)
    wup_p = jnp.concatenate([rwkv_w_up, zr(96)], axis=2).astype(BF16)
    aup_p = jnp.concatenate([zr(32), rwkv_a_up, zr(64)], axis=2).astype(BF16)
    gup_p = jnp.concatenate([jnp.zeros((L, 64, GROUP_W), F32), rwkv_g_up], axis=1).astype(BF16)

    cos256, sin256, cos512, sin512 = _rope_tables(T, C)
    na_bias = _na_bias_tables(na_rpb)
    sub256 = jnp.tile(diff_subln, (1, N_HEADS))

    R = ((B + 1 + 7) // 8) * 8
    cc = jnp.concatenate([c, c_ctx[None], jnp.zeros((R - B - 1, D), F32)], axis=0)
    mod = _modulation(cc, ada_w, ada_b)
    mod2 = jnp.stack([jnp.broadcast_to(mod[:, B:B + 1], (L, B, 6 * D)), mod[:, :B]], axis=2)
    mod2 = mod2.reshape(L, 2 * B, 1, 6 * D)

    X = jnp.concatenate([ctx, x], axis=1)
    gb = 4 if B % 4 == 0 else (2 if B % 2 == 0 else 1)
    for l in range(L):
        need_ctx = l < L - 1
        j0 = 0 if need_ctx else nct
        na_qkv, dq, dk, dv, z_rw, mq, mk, mv = _in_proj(
            X, mod2[l], norm1_g[l][None], w_all[l], cos256, sin256, cos512, sin512,
            mla_q_norm[l][None], mla_kv_norm[l][None], wqm[l], wqr[l], wk[l], wv[l], pmat, t_wide, C)
        na_o = _na_attn(na_qkv, na_bias[l], tm, j0, nct, C, rows)
        mla_o = _attn_call(functools.partial(_mla_attn_kernel, j0=j0, nct=nct, C=C, Ta=Ta), "mla_attn",
                           mq, mk, mv, [], [], tm, j0)
        rvk, gbn, dirp = _rwkv_prep(z_rw, rwkv_mu[l], rwkv_w0[l], wup_p[l], rwkv_a0[l], aup_p[l], gup_p[l],
                                    rwkv_k_k[l][None], rwkv_k_a[l][None], rwkv_r_k[l].reshape(1, GROUP_W),
                                    t_wide, C)
        chk = _rwkv_chunks(rvk, dirp, t_wide)
        yf, yb = _rwkv_scan(chk, C // CHUNK, gb)
        lam_init = 0.8 - 0.6 * math.exp(-0.3 * l)
        df_o = _attn_call(functools.partial(_diff_attn_kernel, j0=j0, nct=nct, C=C, Ta=Ta, lam_init=lam_init),
                          "diff_attn", dq, dk, dv, [diff_lambda[l], sub256[l][None]],
                          [pl.BlockSpec((4, DIFF_QK), lambda b, j: (0, 0)),
                           pl.BlockSpec((1, GROUP_W), lambda b, j: (0, 0))], tm, j0)
        X = _out_proj(X, mod2[l], na_o, mla_o, yf, yb, gbn, rwkv_ln_w[l][None], rwkv_ln_b[l][None], df_o,
                      w_out_b[l], t_wide if need_ctx else tm, j0, C)
        X = _mlp(X, mod2[l], norm2_g[l][None], wa_b[l], wb_b[l], mlp_conv_w[l], mlp_conv_b[l][None], wd_b[l],
                 final_norm_g[None], t_mlp if need_ctx else tm, j0, C, final=not need_ctx)
    return X
```

```python
import functools
import math

import jax
import jax.numpy as jnp
import numpy as np
from jax import lax
from jax.experimental import pallas as pl
from jax.experimental.pallas import tpu as pltpu

F32 = jnp.float32
BF16 = jnp.bfloat16

GRID_W = 64
GROUP_W = 256
HEAD_DIM = 64
N_HEADS = 4
ROPE_DIM = 32
ROPE_THETA = 10000.0
NORM_EPS = 1e-6
NEG_INF = -1e30
LOG2E = math.log2(math.e)
NA_KH = 8
NA_KW = 16
MLA_Q_RANK = 256
MLA_KV_RANK = 128
RWKV_COLS = 896
RWKV_LN_EPS = 64e-5
DIFF_QK = 32
DIFF_LN_EPS = 1e-5
IN_SPLITS = (768, 1184, 2080, 2848)

W_NA = 0
W_DIFF = 768
W_RWKV = 2048
W_MLA = 2944
W_TOT = 3456

TOKEN_TILE = 256
CHUNK = 64
WIDE_TILE = 768
MLP_TILE = 384
HALO = 8
VMEM_LIMIT = 56 * 1024 * 1024


def _cparams(sem):
    return pltpu.CompilerParams(dimension_semantics=sem, vmem_limit_bytes=VMEM_LIMIT)


def _resident(a):
    return pl.BlockSpec(a.shape, lambda *_: (0,) * a.ndim, pipeline_mode=pl.Buffered(1))


def _dot(a, b):
    return jnp.dot(a.astype(BF16), b.astype(BF16), preferred_element_type=F32)


def _dot_nt(a, b):
    return lax.dot_general(a.astype(BF16), b.astype(BF16), (((1,), (1,)), ((), ())),
                           preferred_element_type=F32)


def _split2(x):
    hi = x.astype(BF16)
    lo = (x - hi.astype(F32)).astype(BF16)
    return hi, lo


def _split3(x):
    h1 = x.astype(BF16)
    r1 = x - h1.astype(F32)
    h2 = r1.astype(BF16)
    h3 = (r1 - h2.astype(F32)).astype(BF16)
    return h1, h2, h3


def _dot3(a, b):
    ah, al = _split2(a)
    bh, bl = _split2(b)
    d = functools.partial(jnp.dot, preferred_element_type=F32)
    return d(ah, bh) + (d(ah, bl) + d(al, bh))


def _sigmoid(x):
    return 1.0 / (1.0 + jnp.exp(-x))


def _softplus(x):
    return jnp.maximum(x, 0.0) + jnp.log(1.0 + jnp.exp(-jnp.abs(x)))


def _rms(x, g, eps):
    return x * lax.rsqrt(jnp.mean(x * x, axis=-1, keepdims=True) + eps) * g


def _token_rows(j, tm):
    return j * tm + lax.broadcasted_iota(jnp.int32, (tm, 1), 0)


def _mod_rows(mod_ref, is_ctx, k, D):
    return jnp.where(is_ctx, mod_ref[0][:, k * D:(k + 1) * D], mod_ref[1][:, k * D:(k + 1) * D])


def _mod_spec(D):
    return pl.BlockSpec((2, 1, 6 * D), lambda j, b: (b, 0, 0))


def _head_ones(n=GROUP_W):
    r = lax.broadcasted_iota(jnp.int32, (n, n), 0) >> 6
    c = lax.broadcasted_iota(jnp.int32, (n, n), 1) >> 6
    return jnp.where(r == c, 1.0, 0.0).astype(BF16)


def _head_sum(x, ones):
    hi, lo = _split2(x)
    d = functools.partial(jnp.dot, preferred_element_type=F32)
    return d(hi, ones) + d(lo, ones)


def _bd_mask():
    r = lax.broadcasted_iota(jnp.int32, (GROUP_W, GROUP_W), 0) >> 6
    c = lax.broadcasted_iota(jnp.int32, (GROUP_W, GROUP_W), 1) >> 6
    return r == c


def _bd(x, mask):
    xb = x.astype(BF16)
    return jnp.where(mask, jnp.concatenate([xb, xb, xb, xb], axis=0), jnp.zeros((), BF16))


def _mod_kernel(s_ref, w_ref, b_ref, o_ref):
    s = s_ref[...]
    s = s * _sigmoid(s)
    o_ref[0] = _dot3(s, w_ref[0]) + b_ref[0]


def _modulation(cc, ada_w, ada_b):
    L, D, N = ada_w.shape
    R = cc.shape[0]
    tn = 1536
    return pl.pallas_call(
        _mod_kernel,
        grid=(L, N // tn),
        in_specs=[pl.BlockSpec((R, D), lambda l, n: (0, 0)),
                  pl.BlockSpec((1, D, tn), lambda l, n: (l, 0, n)),
                  pl.BlockSpec((1, 1, tn), lambda l, n: (l, 0, n))],
        out_specs=pl.BlockSpec((1, R, tn), lambda l, n: (l, 0, n)),
        out_shape=jax.ShapeDtypeStruct((L, R, N), F32),
        compiler_params=_cparams(("parallel", "parallel")),
        name="adaln_mod",
    )(cc, ada_w, ada_b.reshape(L, 1, N))


def _in_kernel(x_ref, mod_ref, g_ref, w_ref, cos_ref, sin_ref, cos5_ref, sin5_ref,
               qn_ref, kvn_ref, wqm_ref, wqr_ref, wk_ref, wv_ref, p_ref,
               na_ref, dq_ref, dk_ref, dv_ref, rw_ref, mq_ref, mk_ref, mv_ref, *, C):
    x = x_ref[0]
    tm, D = x.shape
    is_ctx = _token_rows(pl.program_id(0), tm) < C
    h = _rms(x, g_ref[...], NORM_EPS) * (1.0 + _mod_rows(mod_ref, is_ctx, 1, D)) + _mod_rows(mod_ref, is_ctx, 0, D)
    hb = h.astype(BF16)
    dot = functools.partial(jnp.dot, preferred_element_type=F32)
    na_ref[0] = dot(hb, w_ref[:, W_NA:W_DIFF]).astype(BF16)
    d = dot(hb, w_ref[:, W_DIFF:W_RWKV])
    cos = cos_ref[...]
    sin = sin_ref[...]
    scale = DIFF_QK ** -0.5 * LOG2E
    dq_ref[0] = ((d[:, 0:256] * cos + d[:, 256:512] * sin) * scale).astype(BF16)
    dk_ref[0] = (d[:, 512:768] * cos + d[:, 768:1024] * sin).astype(BF16)
    dv_ref[0] = _with_ones(d[:, 1024:1280])
    rw_ref[0] = dot(hb, w_ref[:, W_RWKV:W_MLA])
    z = dot(hb, w_ref[:, W_MLA:W_TOT])
    cos5 = cos5_ref[...]
    sin5 = sin5_ref[...]
    nq = _rms(z[:, 0:256], qn_ref[...], NORM_EPS).astype(BF16)
    mscale = (HEAD_DIM + ROPE_DIM) ** -0.5 * LOG2E
    mq_ref[0] = ((dot(nq, wqm_ref[...]) * cos5 + dot(nq, wqr_ref[...]) * sin5) * mscale).astype(BF16)
    nkv = _rms(z[:, 256:384], kvn_ref[...], NORM_EPS).astype(BF16)
    kr = dot(z[:, 384:512].astype(BF16), p_ref[...])
    mk_ref[0] = (dot(nkv, wk_ref[...]) + kr[:, 0:512] * cos5 + kr[:, 512:1024] * sin5).astype(BF16)
    mv_ref[0] = _with_ones(dot(nkv, wv_ref[...]))


def _in_proj(X, mod2, g, w, cos256, sin256, cos512, sin512, qn, kvn, wqm, wqr, wk, wv, pmat, tm, C):
    B, Ta, D = X.shape
    nt = Ta // tm
    tok = lambda n: pl.BlockSpec((1, tm, n), lambda j, b: (b, j, 0))
    tab = lambda n: pl.BlockSpec((tm, n), lambda j, b: (j, 0))
    full = _resident
    return pl.pallas_call(
        functools.partial(_in_kernel, C=C),
        grid=(nt, B),
        in_specs=[tok(D), _mod_spec(D),
                  full(g), full(w), tab(256), tab(256), tab(512), tab(512),
                  full(qn), full(kvn), full(wqm), full(wqr), full(wk), full(wv), full(pmat)],
        out_specs=[tok(768), tok(256), tok(256), tok(512), tok(RWKV_COLS), tok(512), tok(512), tok(512)],
        out_shape=[jax.ShapeDtypeStruct((B, Ta, 768), BF16),
                   jax.ShapeDtypeStruct((B, Ta, 256), BF16),
                   jax.ShapeDtypeStruct((B, Ta, 256), BF16),
                   jax.ShapeDtypeStruct((B, Ta, 512), BF16),
                   jax.ShapeDtypeStruct((B, Ta, RWKV_COLS), F32),
                   jax.ShapeDtypeStruct((B, Ta, 512), BF16),
                   jax.ShapeDtypeStruct((B, Ta, 512), BF16),
                   jax.ShapeDtypeStruct((B, Ta, 512), BF16)],
        compiler_params=_cparams(("parallel", "parallel")),
        name="in_proj",
    )(X, mod2, g, w, cos256, sin256, cos512, sin512, qn, kvn, wqm, wqr, wk, wv, pmat)


def _softmax_parts(s):
    m = jnp.max(s, axis=-1, keepdims=True)
    e = jnp.exp2(s - m)
    return e, jnp.sum(e, axis=-1, keepdims=True)


def _with_ones(v):
    one = jnp.ones((v.shape[0], 128), F32)
    return jnp.concatenate([v[:, 0:128], one, v[:, 128:256], one], axis=1).astype(BF16)


def _attend(s, vp1):
    sb = s.astype(BF16)
    e = jnp.exp2(sb - jnp.max(sb, axis=-1, keepdims=True))
    r = jnp.dot(e, vp1, preferred_element_type=F32)
    return r[:, 0:128] / r[:, 128:256]


def _mla_attn_body(q, k_ref, v_ref, nk):
    tq = q.shape[0]
    lane = lax.broadcasted_iota(jnp.int32, (tq, 128), 1)
    outs = []
    for pair in range(2):
        vp1 = v_ref[0, 0:nk, pair * 256:(pair + 1) * 256]
        o2 = [_attend(_dot_nt(q[:, h * 128:(h + 1) * 128], k_ref[0, 0:nk, h * 128:(h + 1) * 128]), vp1)
              for h in (2 * pair, 2 * pair + 1)]
        outs.append(jnp.where(lane < 64, o2[0], o2[1]))
    return jnp.concatenate(outs, axis=1)


def _diff_body(q, k_ref, v_ref, nk, lam):
    tq = q.shape[0]
    k = k_ref[0, 0:nk, :]
    sub = lax.broadcasted_iota(jnp.int32, (tq, GROUP_W), 1) >> 5
    lane = lax.broadcasted_iota(jnp.int32, (tq, 128), 1)
    zero = jnp.zeros((), BF16)
    outs = []
    for pair in range(2):
        vp1 = v_ref[0, 0:nk, pair * 256:(pair + 1) * 256]
        o2 = []
        for hh in range(2):
            h = 2 * pair + hh
            o2.append(_attend(_dot_nt(jnp.where(sub == 2 * h, q, zero), k), vp1)
                      - lam * _attend(_dot_nt(jnp.where(sub == 2 * h + 1, q, zero), k), vp1))
        outs.append(jnp.where(lane < 64, o2[0], o2[1]))
    return jnp.concatenate(outs, axis=1)


def _diff_finish(o, lam_init, sub_ref):
    ms = _head_sum(o * o, _head_ones()) * (1.0 / HEAD_DIM)
    return (o * lax.rsqrt(ms + DIFF_LN_EPS) * sub_ref[...]) * (1.0 - lam_init)


def _na_ctx_body(qt_ref, kv_ref, C):
    tq = qt_ref.shape[1]
    zero = jnp.zeros((), BF16)
    lane = lax.broadcasted_iota(jnp.int32, (tq, 128), 1)
    outs = []
    for pair in range(2):
        q2 = qt_ref[0, :, pair * 128:(pair + 1) * 128]
        kc = kv_ref[0, 0:C, 256 + pair * 128:256 + (pair + 1) * 128]
        vc = kv_ref[0, 0:C, 512 + pair * 128:512 + (pair + 1) * 128]
        o2 = []
        for hh in range(2):
            qm = jnp.where((lane < 64) if hh == 0 else (lane >= 64), q2, zero)
            e, l = _softmax_parts(_dot_nt(qm, kc))
            o2.append(jnp.dot(e.astype(BF16), vc, preferred_element_type=F32) * (1.0 / l))
        outs.append(jnp.where(lane < 64, o2[0], o2[1]))
    return jnp.concatenate(outs, axis=1)


def _na_lat_body(qt_ref, kv_ref, bias_ref, first_row, C, rows, R):
    zero = jnp.zeros((), BF16)
    lane = lax.broadcasted_iota(jnp.int32, (GRID_W, 128), 1)
    W = NA_KH * GRID_W
    dot = functools.partial(jnp.dot, preferred_element_type=F32)
    starts, deltas = [], []
    for rr in range(R):
        r = first_row + rr
        rs = jnp.clip(r - NA_KH // 2, 0, rows - NA_KH)
        deltas.append(r - rs)
        starts.append(pl.multiple_of(C + rs * GRID_W, GRID_W))
    units = [(rr, pair) for pair in range(2) for rr in range(R)]
    qs = {}
    for rr, pair in units:
        q2 = qt_ref[0, rr * GRID_W:(rr + 1) * GRID_W, pair * 128:(pair + 1) * 128]
        qs[rr, pair] = jnp.concatenate([jnp.where(lane < 64, q2, zero), jnp.where(lane >= 64, q2, zero)], axis=0)
    s_ctx = {}
    for pair in range(2):
        kc = kv_ref[0, 0:C, 256 + pair * 128:256 + (pair + 1) * 128]
        sc = _dot_nt(jnp.concatenate([qs[rr, pair] for rr in range(R)], axis=0), kc)
        for rr in range(R):
            s_ctx[rr, pair] = sc[rr * 128:(rr + 1) * 128]
    s_lat = {}
    for rr, pair in units:
        kw = kv_ref[0, pl.ds(starts[rr], W), 256 + pair * 128:256 + (pair + 1) * 128]
        bias = jnp.concatenate([bias_ref[deltas[rr], 2 * pair], bias_ref[deltas[rr], 2 * pair + 1]], axis=0)
        s_lat[rr, pair] = _dot_nt(qs[rr, pair], kw) + bias
    e_lat, e_ctx, inv = {}, {}, {}
    for u in units:
        m = jnp.maximum(jnp.max(s_lat[u], axis=-1, keepdims=True), jnp.max(s_ctx[u], axis=-1, keepdims=True))
        e_lat[u] = jnp.exp2(s_lat[u] - m)
        ec = jnp.exp2(s_ctx[u] - m)
        e_ctx[u] = ec.astype(BF16)
        inv[u] = 1.0 / (jnp.sum(e_lat[u], axis=-1, keepdims=True) + jnp.sum(ec, axis=-1, keepdims=True))
    o_ctx = {}
    for pair in range(2):
        vc = kv_ref[0, 0:C, 512 + pair * 128:512 + (pair + 1) * 128]
        oc = dot(jnp.concatenate([e_ctx[rr, pair] for rr in range(R)], axis=0), vc)
        for rr in range(R):
            o_ctx[rr, pair] = oc[rr * 128:(rr + 1) * 128]
    o = {}
    for rr, pair in units:
        vw = kv_ref[0, pl.ds(starts[rr], W), 512 + pair * 128:512 + (pair + 1) * 128]
        ou = (dot(e_lat[rr, pair].astype(BF16), vw) + o_ctx[rr, pair]) * inv[rr, pair]
        o[rr, pair] = jnp.where(lane < 64, ou[0:GRID_W], ou[GRID_W:])
    return jnp.concatenate([jnp.concatenate([o[rr, 0], o[rr, 1]], axis=1) for rr in range(R)], axis=0)


def _attn_kernel(naq_ref, nakv_ref, bias_ref, mq_ref, mk_ref, mv_ref, dq_ref, dk_ref, dv_ref, lam_ref, sub_ref,
                 o_ref, *, j0, nct, C, Ta, rows, lam_init):
    j = pl.program_id(1) + j0
    tq = naq_ref.shape[1]
    lp = lam_ref[...]
    lam = (jnp.exp(jnp.sum(lp[0:1] * lp[1:2], axis=-1, keepdims=True))
           - jnp.exp(jnp.sum(lp[2:3] * lp[3:4], axis=-1, keepdims=True)) + lam_init)

    if j0 < nct:
        @pl.when(j < nct)
        def _():
            o_ref[0] = jnp.concatenate(
                [_na_ctx_body(naq_ref, nakv_ref, C),
                 _mla_attn_body(mq_ref[0], mk_ref, mv_ref, C),
                 _diff_finish(_diff_body(dq_ref[0], dk_ref, dv_ref, C, lam), lam_init, sub_ref)],
                axis=1).astype(BF16)

    @pl.when(j >= nct)
    def _():
        R = tq // GRID_W
        o_ref[0] = jnp.concatenate(
            [_na_lat_body(naq_ref, nakv_ref, bias_ref, (j - nct) * R, C, rows, R),
             _mla_attn_body(mq_ref[0], mk_ref, mv_ref, Ta),
             _diff_finish(_diff_body(dq_ref[0], dk_ref, dv_ref, Ta, lam), lam_init, sub_ref)],
            axis=1).astype(BF16)


def _attention(na_qkv, bias, mq, mk, mv, dq, dk, dv, lam_p, sub, tq, j0, nct, C, rows, lam_init):
    B, Ta, _ = na_qkv.shape
    nq = Ta // tq - j0
    qt = lambda a: pl.BlockSpec((1, tq, a.shape[-1]), lambda b, j: (b, j + j0, 0))
    seq = lambda a: pl.BlockSpec((1, Ta, a.shape[-1]), lambda b, j: (b, 0, 0))
    kern = functools.partial(_attn_kernel, j0=j0, nct=nct, C=C, Ta=Ta, rows=rows, lam_init=lam_init)
    return pl.pallas_call(
        kern,
        grid=(B, nq),
        in_specs=[qt(na_qkv), seq(na_qkv), _resident(bias), qt(mq), seq(mk), seq(mv), qt(dq), seq(dk), seq(dv),
                  _resident(lam_p), _resident(sub)],
        out_specs=pl.BlockSpec((1, tq, 768), lambda b, j: (b, j + j0, 0)),
        out_shape=jax.ShapeDtypeStruct((B, Ta, 768), BF16),
        compiler_params=_cparams(("parallel", "arbitrary")),
        name="attention",
    )(na_qkv, na_qkv, bias, mq, mk, mv, dq, dk, dv, lam_p, sub)


def _rwkv_prep_kernel(z_ref, zp_ref, zn_ref, mu_ref, w0_ref, wup_ref, a0_ref, aup_ref, gup_ref,
                      kk_ref, ka_ref, rk_ref, rvk_ref, gb_ref, dir_ref, *, C, Ta):
    tm = z_ref.shape[1]
    z = z_ref[0]
    row = lax.broadcasted_iota(jnp.int32, (tm, 1), 0)
    pos = _token_rows(pl.program_id(0), tm)
    has_prev = jnp.where((pos == 0) | (pos == C), 0.0, 1.0)
    has_next = jnp.where((pos == C - 1) | (pos == Ta - 1), 0.0, 1.0)
    prev = jnp.where(row == 0, zp_ref[0, HALO - 1:HALO, :], pltpu.roll(z, 1, axis=0)) * has_prev
    nxt = jnp.where(row == tm - 1, zn_ref[0, 0:1, :], pltpu.roll(z, tm - 1, axis=0)) * has_next
    zs = z + mu_ref[0:1, :] * (prev - z) + mu_ref[1:2, :] * (nxt - z)
    r = zs[:, 0:256]
    k = zs[:, 256:512]
    v = zs[:, 512:768]
    low = zs[:, 768:896]
    ones = _head_ones()
    kk = k * kk_ref[...]
    kk = kk * lax.rsqrt(jnp.maximum(_head_sum(kk * kk, ones), 1e-24))
    g = _dot(_sigmoid(low), gup_ref[...])
    bonus = _head_sum(r * k * rk_ref[...], ones) * v
    rvk_ref[0, :, 0:256] = r
    rvk_ref[0, :, 256:512] = v
    rvk_ref[0, :, 512:768] = kk
    gb_ref[0, :, 0:256] = g
    gb_ref[0, :, 256:512] = bonus
    tl = jnp.tanh(low).astype(BF16)
    lb = low.astype(BF16)
    for d in range(2):
        w = -_softplus(-(w0_ref[d:d + 1, :] + jnp.dot(tl, wup_ref[d], preferred_element_type=F32))) - 0.5
        a = _sigmoid(a0_ref[d:d + 1, :] + jnp.dot(lb, aup_ref[d], preferred_element_type=F32))
        dir_ref[d, 0, :, 0:256] = -jnp.exp(w)
        dir_ref[d, 0, :, 256:512] = k * (1.0 + (a - 1.0) * ka_ref[...])
        dir_ref[d, 0, :, 512:768] = kk * a


def _rwkv_prep(z, mu, w0, wup_p, a0, aup_p, gup_p, k_k, k_a, r_k, tm, C):
    B, Ta, _ = z.shape
    nt = Ta // tm
    hb = tm // HALO
    nh = Ta // HALO
    tok = lambda n: pl.BlockSpec((1, tm, n), lambda j, b: (b, j, 0))
    full = _resident
    kern = functools.partial(_rwkv_prep_kernel, C=C, Ta=Ta)
    return pl.pallas_call(
        kern,
        grid=(nt, B),
        in_specs=[tok(RWKV_COLS),
                  pl.BlockSpec((1, HALO, RWKV_COLS), lambda j, b: (b, jnp.maximum(j * hb - 1, 0), 0)),
                  pl.BlockSpec((1, HALO, RWKV_COLS), lambda j, b: (b, jnp.minimum((j + 1) * hb, nh - 1), 0)),
                  full(mu), full(w0), full(wup_p), full(a0), full(aup_p), full(gup_p),
                  full(k_k), full(k_a), full(r_k)],
        out_specs=[tok(768), tok(512),
                   pl.BlockSpec((2, 1, tm, 768), lambda j, b: (0, b, j, 0))],
        out_shape=[jax.ShapeDtypeStruct((B, Ta, 768), F32),
                   jax.ShapeDtypeStruct((B, Ta, 512), F32),
                   jax.ShapeDtypeStruct((2, B, Ta, 768), F32)],
        compiler_params=_cparams(("parallel", "parallel")),
        name="rwkv_prep",
    )(z, z, z, mu, w0, wup_p, a0, aup_p, gup_p, k_k, k_a, r_k)


def _rwkv_chunk_kernel(rvk_ref, dir_ref, g_ref, o_ref):
    d = pl.program_id(0)
    nchunk = rvk_ref.shape[1] // CHUNK
    bdm = _bd_mask()
    row = lax.broadcasted_iota(jnp.int32, (CHUNK, GROUP_W), 0)
    col = lax.broadcasted_iota(jnp.int32, (CHUNK, GROUP_W), 1) & (CHUNK - 1)
    tdiff = jnp.where(d == 0, row - col, col - row)
    incl = tdiff >= 0
    strict = tdiff > 0
    eye = tdiff == 0
    r2 = lax.broadcasted_iota(jnp.int32, (CHUNK, CHUNK), 0)
    c2 = lax.broadcasted_iota(jnp.int32, (CHUNK, CHUNK), 1)
    tri = jnp.where(jnp.where(d == 0, r2 - c2, c2 - r2) >= 0, 1.0, 0.0).astype(BF16)
    dot = functools.partial(jnp.dot, preferred_element_type=F32)

    def hmul(x, y):
        return dot(x.astype(BF16), _bd(y, bdm))

    def hmul_t(x, yt):
        ytb = yt.astype(BF16)
        return dot(x.astype(BF16), jnp.where(bdm, jnp.concatenate([ytb, ytb, ytb, ytb], axis=1),
                                             jnp.zeros((), BF16)))

    def fold(full):
        full = jnp.where(bdm, full, 0.0)
        return (full[0:64] + full[64:128]) + (full[128:192] + full[192:256])

    cs = range(nchunk)
    sls = [slice(c * CHUNK, (c + 1) * CHUNK) for c in cs]
    r = [rvk_ref[0, sl, 0:256] for sl in sls]
    v = [rvk_ref[0, sl, 256:512] for sl in sls]
    kk = [rvk_ref[0, sl, 512:768] for sl in sls]
    lw = [dir_ref[0, 0, sl, 0:256] for sl in sls]
    kd = [dir_ref[0, 0, sl, 256:512] for sl in sls]
    b = [dir_ref[0, 0, sl, 512:768] for sl in sls]
    cum, tot = [], []
    for c in cs:
        l1, l2, l3 = _split3(lw[c])
        cum.append(dot(tri, l1) + (dot(tri, l2) + dot(tri, l3)))
        tot.append(jnp.sum(lw[c], axis=0, keepdims=True))
    rt = [r[c] * jnp.exp(cum[c]) for c in cs]
    at = [-kk[c] * jnp.exp(cum[c] - lw[c]) for c in cs]
    einv = [jnp.exp(-cum[c]) for c in cs]
    eend = [jnp.exp(tot[c] - cum[c]) for c in cs]
    ar = [jnp.concatenate([at[c], rt[c]], axis=0) for c in cs]
    xb = [hmul_t(ar[c], (b[c] * einv[c]).T) for c in cs]
    xk = [hmul_t(ar[c], (kd[c] * einv[c]).T) for c in cs]
    n = [jnp.where(strict, xb[c][0:CHUNK], 0.0) for c in cs]
    lrb = [jnp.where(incl, xb[c][CHUNK:], 0.0) for c in cs]
    al = [jnp.concatenate([jnp.where(strict, xk[c][0:CHUNK], 0.0),
                           jnp.where(incl, xk[c][CHUNK:], 0.0)], axis=0) for c in cs]
    akv_lrkv = [hmul(al[c], v[c]) for c in cs]
    p = [jnp.where(eye, 1.0, 0.0) + n[c] for c in cs]
    npow = [hmul(n[c], n[c]) for c in cs]
    for _ in range(4):
        sq = [hmul(jnp.concatenate([npow[c], p[c]], axis=0), npow[c]) for c in cs]
        npow = [sq[c][0:CHUNK] for c in cs]
        p = [p[c] + sq[c][CHUNK:] for c in cs]
    t = [p[c] + hmul(p[c], npow[c]) for c in cs]
    w = [hmul(t[c], at[c]) for c in cs]
    u0 = [hmul(t[c], akv_lrkv[c][0:CHUNK]) for c in cs]
    qh = [rt[c] + hmul(lrb[c], w[c]) for c in cs]
    y0 = [hmul(lrb[c], u0[c]) + akv_lrkv[c][CHUNK:] for c in cs]
    gs, outs = [], []
    for c in cs:
        bht = (b[c] * eend[c]).T.astype(BF16)
        kht = (kd[c] * eend[c]).T.astype(BF16)
        gs.append(fold(dot(bht, w[c].astype(BF16))) + jnp.where(eye, jnp.exp(tot[c]), 0.0))
        hm = fold(dot(bht, u0[c].astype(BF16)) + dot(kht, v[c].astype(BF16)))
        outs.append(jnp.concatenate([qh[c], y0[c], hm], axis=1))
    g_ref[0, 0] = jnp.concatenate(gs, axis=0)
    o_ref[0, 0] = jnp.concatenate(outs, axis=0).astype(BF16)


def _rwkv_chunks(rvk, dirp, tm):
    B, Ta, _ = rvk.shape
    nt = Ta // tm
    blk = lambda n: pl.BlockSpec((1, 1, tm, n), lambda d, b, j: (d, b, j, 0))
    return pl.pallas_call(
        _rwkv_chunk_kernel,
        grid=(2, B, nt),
        in_specs=[pl.BlockSpec((1, tm, 768), lambda d, b, j: (b, j, 0)), blk(768)],
        out_specs=[blk(256), blk(768)],
        out_shape=[jax.ShapeDtypeStruct((2, B, Ta, 256), F32), jax.ShapeDtypeStruct((2, B, Ta, 768), BF16)],
        compiler_params=_cparams(("parallel", "parallel", "parallel")),
        name="rwkv_chunks",
    )(rvk, dirp)


def _rwkv_scan_kernel(gf_ref, cf_ref, gr_ref, cr_ref, yf_ref, yb_ref, s_ref):
    i = pl.program_id(1)
    gb = gf_ref.shape[1]
    bdm = _bd_mask()

    @pl.when(i == 0)
    def _():
        s_ref[...] = jnp.zeros(s_ref.shape, F32)

    dot = functools.partial(jnp.dot, preferred_element_type=F32)
    zero = jnp.zeros((), BF16)
    chains = [(d, bb) for d in range(2) for bb in range(gb)]
    g_refs = (gf_ref, gr_ref)
    c_refs = (cf_ref, cr_ref)
    parts = {}
    for d, bb in chains:
        sh, sl = _split2(s_ref[d, bb])
        gh, gl = _split2(g_refs[d][0, bb])
        parts[d, bb] = (jnp.concatenate([c_refs[d][0, bb, :, 0:256], gh], axis=0), gl,
                        jnp.where(bdm, jnp.concatenate([sh] * 4, axis=0), zero),
                        jnp.where(bdm, jnp.concatenate([sl] * 4, axis=0), zero))
    res = {}
    for ch in chains:
        lh, gl, sbh, sbl = parts[ch]
        r = dot(lh, sbh) + dot(lh, sbl)
        res[ch] = (r[0:CHUNK], r[CHUNK:] + dot(gl, sbh))
    for d, y_ref in enumerate((yf_ref, yb_ref)):
        for bb in range(gb):
            y_ref[bb] = res[d, bb][0] + c_refs[d][0, bb, :, 256:512].astype(F32)
            s_ref[d, bb] = res[d, bb][1] + c_refs[d][0, bb, :, 512:768].astype(F32)


def _rwkv_scan(chg, chb, ncc, gb):
    _, B, Ta, _ = chg.shape
    nc = Ta // CHUNK

    def rev_chunk(i):
        return jnp.where(i < ncc, ncc - 1 - i, nc - 1 - (i - ncc))

    fwd = lambda n: pl.BlockSpec((1, gb, CHUNK, n), lambda b, i: (0, b, i, 0))
    rev = lambda n: pl.BlockSpec((1, gb, CHUNK, n), lambda b, i: (1, b, rev_chunk(i), 0))
    return pl.pallas_call(
        _rwkv_scan_kernel,
        grid=(B // gb, nc),
        in_specs=[fwd(256), fwd(768), rev(256), rev(768)],
        out_specs=[pl.BlockSpec((gb, CHUNK, 256), lambda b, i: (b, i, 0)),
                   pl.BlockSpec((gb, CHUNK, 256), lambda b, i: (b, rev_chunk(i), 0))],
        out_shape=[jax.ShapeDtypeStruct((B, Ta, 256), F32),
                   jax.ShapeDtypeStruct((B, Ta, 256), F32)],
        scratch_shapes=[pltpu.VMEM((2, gb, CHUNK, GROUP_W), F32)],
        compiler_params=_cparams(("parallel", "arbitrary")),
        name="rwkv_scan",
    )(chg, chb, chg, chb)


def _out_kernel(x_ref, mod_ref, att_ref, yf_ref, yb_ref, gb_ref, lnw_ref, lnb_ref, w_ref, o_ref, *, j0, C):
    x = x_ref[0]
    tm, D = x.shape
    is_ctx = _token_rows(pl.program_id(0) + j0, tm) < C
    dot = functools.partial(jnp.dot, preferred_element_type=F32)
    ones = _head_ones()
    y = yf_ref[0] + yb_ref[0]
    mean = _head_sum(y, ones) * (1.0 / HEAD_DIM)
    yc = y - mean
    var = _head_sum(yc * yc, ones) * (1.0 / HEAD_DIM)
    yn = yc * lax.rsqrt(var + RWKV_LN_EPS) * lnw_ref[...] + lnb_ref[...]
    rw = ((yn + gb_ref[0, :, 256:512]) * gb_ref[0, :, 0:256]).astype(BF16)
    mix = (dot(att_ref[0, :, 0:512], w_ref[0:512, :]) + dot(rw, w_ref[512:768, :])) + \
        dot(att_ref[0, :, 512:768], w_ref[768:1024, :])
    o_ref[0] = x + _mod_rows(mod_ref, is_ctx, 2, D) * mix


def _out_proj(X, mod2, att, yf, yb, gbn, ln_w, ln_b, w, tm, j0, C):
    B, Ta, D = X.shape
    nt = Ta // tm - j0
    tok = lambda n: pl.BlockSpec((1, tm, n), lambda j, b: (b, j + j0, 0))
    full = _resident
    return pl.pallas_call(
        functools.partial(_out_kernel, j0=j0, C=C),
        grid=(nt, B),
        in_specs=[tok(D), _mod_spec(D), tok(768), tok(256), tok(256), tok(512), full(ln_w), full(ln_b), full(w)],
        out_specs=tok(D),
        out_shape=jax.ShapeDtypeStruct((B, Ta, D), F32),
        compiler_params=_cparams(("parallel", "parallel")),
        name="out_proj",
    )(X, mod2, att, yf, yb, gbn, ln_w, ln_b, w)


def _mlp_kernel(x_ref, xp_ref, xn_ref, mod_ref, g_ref, wa_ref, wb_ref, cw_ref, cb_ref, wd_ref, gf_ref, o_ref,
                *, j0, C, Ta, fc, final):
    j = pl.program_id(0) + j0
    tm = x_ref.shape[1]
    D = x_ref.shape[2]
    dff = wa_ref.shape[1]
    x = x_ref[0]
    xe = jnp.concatenate([xp_ref[0], x, xn_ref[0]], axis=0)
    pos = _token_rows(j, tm)
    ctx_e = (j * tm - HALO + lax.broadcasted_iota(jnp.int32, (tm + 2 * HALO, 1), 0)) < C
    h = (_rms(xe, g_ref[...], NORM_EPS) * (1.0 + _mod_rows(mod_ref, ctx_e, 4, D))
         + _mod_rows(mod_ref, ctx_e, 3, D)).astype(BF16)
    pmask = jnp.where((pos == 0) | (pos == C), 0.0, 1.0)
    nmask = jnp.where((pos == C - 1) | (pos == Ta - 1), 0.0, 1.0)
    dot = functools.partial(jnp.dot, preferred_element_type=F32)
    acc = jnp.zeros((tm, D), F32)
    for c in range(dff // fc):
        cs = slice(c * fc, (c + 1) * fc)
        a = dot(h, wa_ref[:, cs])
        b = dot(h[HALO:HALO + tm], wb_ref[:, cs])
        cv = (cw_ref[0:1, cs] * (a[HALO - 1:HALO - 1 + tm] * pmask) + cw_ref[1:2, cs] * a[HALO:HALO + tm]
              + cw_ref[2:3, cs] * (a[HALO + 1:HALO + 1 + tm] * nmask) + cb_ref[:, cs])
        u = cv * _sigmoid(cv) * b
        acc = acc + dot(u.astype(BF16), wd_ref[cs, :])
    y = x + _mod_rows(mod_ref, pos < C, 5, D) * acc
    o_ref[0] = _rms(y, gf_ref[...], NORM_EPS) if final else y


def _mlp(X, mod2, g, wa, wb, cw, cb, wd, gf, tm, j0, C, final):
    B, Ta, D = X.shape
    ntot = Ta // tm
    nt = ntot - j0
    hb = tm // HALO
    nh = Ta // HALO
    tok = lambda n: pl.BlockSpec((1, tm, n), lambda j, b: (b, j + j0, 0))
    full = _resident
    kern = functools.partial(_mlp_kernel, j0=j0, C=C, Ta=Ta, fc=wa.shape[1], final=final)
    if final:
        assert j0 * tm == C
        out_spec = pl.BlockSpec((1, tm, D), lambda j, b: (b, j, 0))
        out_shape = jax.ShapeDtypeStruct((B, Ta - C, D), F32)
    else:
        out_spec, out_shape = tok(D), jax.ShapeDtypeStruct((B, Ta, D), F32)
    return pl.pallas_call(
        kern,
        grid=(nt, B),
        in_specs=[tok(D),
                  pl.BlockSpec((1, HALO, D), lambda j, b: (b, jnp.maximum((j + j0) * hb - 1, j0 * hb), 0)),
                  pl.BlockSpec((1, HALO, D), lambda j, b: (b, jnp.minimum((j + j0 + 1) * hb, nh - 1), 0)),
                  _mod_spec(D),
                  full(g), full(wa), full(wb), full(cw), full(cb), full(wd), full(gf)],
        out_specs=out_spec,
        out_shape=out_shape,
        compiler_params=_cparams(("parallel", "parallel")),
        name="conv_glu",
    )(X, X, X, mod2, g, wa, wb, cw, cb, wd, gf)


def _rot_cols(w):
    s = w.shape
    x = w.reshape(s[:-1] + (s[-1] // ROPE_DIM, 4, ROPE_DIM // 4))
    r1, r2, c1, c2 = x[..., 0, :], x[..., 1, :], x[..., 2, :], x[..., 3, :]
    return jnp.stack([-r2, r1, -c2, c1], axis=-2).reshape(s)


def _rope_tables(T, C):
    t = np.arange(T)
    rowp = (t // GRID_W).astype(np.float32)
    colp = (t % GRID_W).astype(np.float32)
    half = ROPE_DIM // 2
    freqs = jnp.asarray(ROPE_THETA, F32) ** (-jnp.arange(0, half, 2, dtype=F32) / half)
    ar = jnp.asarray(rowp)[:, None] * freqs[None, :]
    ac = jnp.asarray(colp)[:, None] * freqs[None, :]
    ang = jnp.concatenate([ar, ar, ac, ac], axis=-1)
    cos = jnp.concatenate([jnp.ones((C, ROPE_DIM), F32), jnp.cos(ang)], axis=0)
    sin = jnp.concatenate([jnp.zeros((C, ROPE_DIM), F32), jnp.sin(ang)], axis=0)
    Ta = T + C
    cos256 = jnp.tile(cos, (1, 8))
    sin256 = jnp.tile(sin, (1, 8))
    one = jnp.ones((Ta, HEAD_DIM), F32)
    zero = jnp.zeros((Ta, HEAD_DIM), F32)
    cos512 = jnp.tile(jnp.concatenate([one, cos, one[:, :32]], axis=1), (1, 4))
    sin512 = jnp.tile(jnp.concatenate([zero, sin, zero[:, :32]], axis=1), (1, 4))
    return cos256, sin256, cos512, sin512


def _na_bias_tables(rpb):
    cpos = np.arange(GRID_W)
    cstart = np.clip(cpos - NA_KW // 2, 0, GRID_W - NA_KW)
    col_mask = (cpos[None, :] >= cstart[:, None]) & (cpos[None, :] < cstart[:, None] + NA_KW)
    col_idx = np.clip(cpos[None, :] - cpos[:, None] + NA_KW - 1, 0, 2 * NA_KW - 2)
    tabs = []
    for delta in range(NA_KH):
        row_off = np.arange(NA_KH) - delta + NA_KH - 1
        bias = rpb[:, :, row_off][:, :, :, col_idx]
        bias = jnp.where(jnp.asarray(col_mask)[None, None, None], bias * LOG2E, NEG_INF)
        tabs.append(bias.transpose(0, 1, 3, 2, 4).reshape(rpb.shape[0], N_HEADS, GRID_W, NA_KH * GRID_W))
    return jnp.stack(tabs, axis=1)


def kernel(x, c, ctx, c_ctx, norm1_g, norm2_g, ada_w, ada_b, w_in, w_out, na_rpb, mla_q_norm, mla_kv_norm,
           mla_w_uq, mla_w_ukv, rwkv_mu, rwkv_w0, rwkv_w_up, rwkv_a0, rwkv_a_up, rwkv_g_up, rwkv_k_k,
           rwkv_k_a, rwkv_r_k, rwkv_ln_w, rwkv_ln_b, diff_lambda, diff_subln, mlp_w_up, mlp_conv_w,
           mlp_conv_b, mlp_w_down, final_norm_g):
    B, T, D = x.shape
    C = ctx.shape[1]
    L = ada_w.shape[0]
    Ta = T + C
    tm = min(TOKEN_TILE, C)
    rows = T // GRID_W
    assert C % tm == 0 and T % tm == 0 and tm % CHUNK == 0 and rows >= NA_KH and D == 1024
    nct = C // tm
    t_wide = WIDE_TILE if Ta % WIDE_TILE == 0 else tm
    t_mlp = MLP_TILE if Ta % MLP_TILE == 0 else tm
    dff = mlp_w_down.shape[1]

    wi = w_in
    na_w = wi[:, :, 0:768].at[:, :, 0:256].multiply(HEAD_DIM ** -0.5 * LOG2E)
    cq_w, ckv_w, kr_w = wi[:, :, 768:1024], wi[:, :, 1024:1152], wi[:, :, 1152:1184]
    rw_w = wi[:, :, 1184:2080]
    dq_w, dk_w, dv_w = wi[:, :, 2080:2336], wi[:, :, 2336:2592], wi[:, :, 2592:2848]
    w_all = jnp.concatenate([na_w, dq_w, _rot_cols(dq_w), dk_w, _rot_cols(dk_w), dv_w, rw_w,
                             cq_w, ckv_w, kr_w, _rot_cols(kr_w), jnp.zeros((L, D, 64), F32)],
                            axis=-1).astype(BF16)
    w_out_b = w_out.astype(BF16)
    wa_b = mlp_w_up[:, :, :dff].astype(BF16)
    wb_b = mlp_w_up[:, :, dff:].astype(BF16)
    wd_b = mlp_w_down.astype(BF16)

    uq = mla_w_uq.reshape(L, MLA_Q_RANK, N_HEADS, HEAD_DIM + ROPE_DIM)
    pad32 = jnp.zeros((L, MLA_Q_RANK, N_HEADS, 32), F32)
    wqm = jnp.concatenate([uq, pad32], axis=-1).reshape(L, MLA_Q_RANK, 512).astype(BF16)
    wqr = jnp.concatenate([jnp.zeros_like(uq[..., :HEAD_DIM]), _rot_cols(uq[..., HEAD_DIM:]), pad32],
                          axis=-1).reshape(L, MLA_Q_RANK, 512).astype(BF16)
    ukv = mla_w_ukv.reshape(L, MLA_KV_RANK, N_HEADS, 2 * HEAD_DIM)
    wk = jnp.concatenate([ukv[..., :HEAD_DIM], jnp.zeros_like(ukv[..., HEAD_DIM:])],
                         axis=-1).reshape(L, MLA_KV_RANK, 512).astype(BF16)
    wv = ukv[..., HEAD_DIM:].reshape(L, MLA_KV_RANK, 256).astype(BF16)
    pm = np.zeros((128, 1024), np.float32)
    for h in range(N_HEADS):
        for i in range(ROPE_DIM):
            pm[i, h * 128 + HEAD_DIM + i] = 1.0
            pm[ROPE_DIM + i, 512 + h * 128 + HEAD_DIM + i] = 1.0
    pmat = jnp.asarray(pm, BF16)

    zr = lambda n: jnp.zeros((L, 2, n, GROUP_W), F32)
    wup_p = jnp.concatenate([rwkv_w_up, zr(96)], axis=2).astype(BF16)
    aup_p = jnp.concatenate([zr(32), rwkv_a_up, zr(64)], axis=2).astype(BF16)
    gup_p = jnp.concatenate([jnp.zeros((L, 64, GROUP_W), F32), rwkv_g_up], axis=1).astype(BF16)

    cos256, sin256, cos512, sin512 = _rope_tables(T, C)
    na_bias = _na_bias_tables(na_rpb)
    sub256 = jnp.tile(diff_subln, (1, N_HEADS))

    R = ((B + 1 + 7) // 8) * 8
    cc = jnp.concatenate([c, c_ctx[None], jnp.zeros((R - B - 1, D), F32)], axis=0)
    mod = _modulation(cc, ada_w, ada_b)
    mod2 = jnp.stack([jnp.broadcast_to(mod[:, B:B + 1], (L, B, 6 * D)), mod[:, :B]], axis=2)
    mod2 = mod2.reshape(L, 2 * B, 1, 6 * D)

    X = jnp.concatenate([ctx, x], axis=1)
    gb = 4 if B % 4 == 0 else (2 if B % 2 == 0 else 1)
    for l in range(L):
        need_ctx = l < L - 1
        j0 = 0 if need_ctx else nct
        na_qkv, dq, dk, dv, z_rw, mq, mk, mv = _in_proj(
            X, mod2[l], norm1_g[l][None], w_all[l], cos256, sin256, cos512, sin512,
            mla_q_norm[l][None], mla_kv_norm[l][None], wqm[l], wqr[l], wk[l], wv[l], pmat, t_wide, C)
        lam_init = 0.8 - 0.6 * math.exp(-0.3 * l)
        att = _attention(na_qkv, na_bias[l], mq, mk, mv, dq, dk, dv, diff_lambda[l], sub256[l][None],
                         tm, j0, nct, C, rows, lam_init)
        rvk, gbn, dirp = _rwkv_prep(z_rw, rwkv_mu[l], rwkv_w0[l], wup_p[l], rwkv_a0[l], aup_p[l], gup_p[l],
                                    rwkv_k_k[l][None], rwkv_k_a[l][None], rwkv_r_k[l].reshape(1, GROUP_W),
                                    t_wide, C)
        chg, chb = _rwkv_chunks(rvk, dirp, t_wide)
        yf, yb = _rwkv_scan(chg, chb, C // CHUNK, gb)
        X = _out_proj(X, mod2[l], att, yf, yb, gbn, rwkv_ln_w[l][None], rwkv_ln_b[l][None],
                      w_out_b[l], t_wide if need_ctx else tm, j0, C)
        X = _mlp(X, mod2[l], norm2_g[l][None], wa_b[l], wb_b[l], mlp_conv_w[l], mlp_conv_b[l][None], wd_b[l],
                 final_norm_g[None], t_mlp if need_ctx else tm, j0, C, final=not need_ctx)
    return X
```

```python
import functools
import math

import jax
import jax.numpy as jnp
import numpy as np
from jax import lax
from jax.experimental import pallas as pl
from jax.experimental.pallas import tpu as pltpu

F32 = jnp.float32
BF16 = jnp.bfloat16

GRID_W = 64
GROUP_W = 256
HEAD_DIM = 64
N_HEADS = 4
ROPE_DIM = 32
ROPE_THETA = 10000.0
NORM_EPS = 1e-6
NEG_INF = -1e30
LOG2E = math.log2(math.e)
NA_KH = 8
NA_KW = 16
MLA_Q_RANK = 256
MLA_KV_RANK = 128
RWKV_COLS = 896
RWKV_LN_EPS = 64e-5
DIFF_QK = 32
DIFF_LN_EPS = 1e-5
IN_SPLITS = (768, 1184, 2080, 2848)

W_NA = 0
W_DIFF = 768
W_RWKV = 2048
W_MLA = 2944
W_TOT = 3456

TOKEN_TILE = 256
CHUNK = 64
WIDE_TILE = 768
MLP_TILE = 384
HALO = 8
VMEM_LIMIT = 56 * 1024 * 1024


def _cparams(sem):
    return pltpu.CompilerParams(dimension_semantics=sem, vmem_limit_bytes=VMEM_LIMIT)


def _resident(a):
    return pl.BlockSpec(a.shape, lambda *_: (0,) * a.ndim, pipeline_mode=pl.Buffered(1))


def _dot(a, b):
    return jnp.dot(a.astype(BF16), b.astype(BF16), preferred_element_type=F32)


def _dot_nt(a, b):
    return lax.dot_general(a.astype(BF16), b.astype(BF16), (((1,), (1,)), ((), ())),
                           preferred_element_type=F32)


def _split2(x):
    hi = x.astype(BF16)
    lo = (x - hi.astype(F32)).astype(BF16)
    return hi, lo


def _split3(x):
    h1 = x.astype(BF16)
    r1 = x - h1.astype(F32)
    h2 = r1.astype(BF16)
    h3 = (r1 - h2.astype(F32)).astype(BF16)
    return h1, h2, h3


def _dot3(a, b):
    ah, al = _split2(a)
    bh, bl = _split2(b)
    d = functools.partial(jnp.dot, preferred_element_type=F32)
    return d(ah, bh) + (d(ah, bl) + d(al, bh))


def _sigmoid(x):
    return 1.0 / (1.0 + jnp.exp(-x))


def _rms(x, g, eps):
    return x * lax.rsqrt(jnp.mean(x * x, axis=-1, keepdims=True) + eps) * g


def _token_rows(j, tm):
    return j * tm + lax.broadcasted_iota(jnp.int32, (tm, 1), 0)


def _mod_rows(mod_ref, is_ctx, k, D):
    return jnp.where(is_ctx, mod_ref[0][:, k * D:(k + 1) * D], mod_ref[1][:, k * D:(k + 1) * D])


def _mod_spec(D):
    return pl.BlockSpec((2, 1, 6 * D), lambda j, b: (b, 0, 0))


def _head_ones(n=GROUP_W):
    r = lax.broadcasted_iota(jnp.int32, (n, n), 0) >> 6
    c = lax.broadcasted_iota(jnp.int32, (n, n), 1) >> 6
    return jnp.where(r == c, 1.0, 0.0).astype(BF16)


def _head_sum(x, ones):
    hi, lo = _split2(x)
    d = functools.partial(jnp.dot, preferred_element_type=F32)
    return d(hi, ones) + d(lo, ones)


def _bd_mask():
    r = lax.broadcasted_iota(jnp.int32, (GROUP_W, GROUP_W), 0) >> 6
    c = lax.broadcasted_iota(jnp.int32, (GROUP_W, GROUP_W), 1) >> 6
    return r == c


def _bd(x, mask):
    xb = x.astype(BF16)
    return jnp.where(mask, jnp.concatenate([xb, xb, xb, xb], axis=0), jnp.zeros((), BF16))


def _mod_kernel(s_ref, w_ref, b_ref, o_ref):
    s = s_ref[...]
    s = s * _sigmoid(s)
    o_ref[0] = _dot3(s, w_ref[0]) + b_ref[0]


def _modulation(cc, ada_w, ada_b):
    L, D, N = ada_w.shape
    R = cc.shape[0]
    tn = 1536
    return pl.pallas_call(
        _mod_kernel,
        grid=(L, N // tn),
        in_specs=[pl.BlockSpec((R, D), lambda l, n: (0, 0)),
                  pl.BlockSpec((1, D, tn), lambda l, n: (l, 0, n)),
                  pl.BlockSpec((1, 1, tn), lambda l, n: (l, 0, n))],
        out_specs=pl.BlockSpec((1, R, tn), lambda l, n: (l, 0, n)),
        out_shape=jax.ShapeDtypeStruct((L, R, N), F32),
        compiler_params=_cparams(("parallel", "parallel")),
        name="adaln_mod",
    )(cc, ada_w, ada_b.reshape(L, 1, N))


def _in_kernel(x_ref, mod_ref, g_ref, w_ref, cos_ref, sin_ref, cos5_ref, sin5_ref,
               qn_ref, kvn_ref, wqm_ref, wqr_ref, wk_ref, wv_ref, p_ref,
               na_ref, dq_ref, dk_ref, dv_ref, rw_ref, mq_ref, mk_ref, mv_ref, *, C):
    x = x_ref[0]
    tm, D = x.shape
    is_ctx = _token_rows(pl.program_id(0), tm) < C
    h = _rms(x, g_ref[...], NORM_EPS) * (1.0 + _mod_rows(mod_ref, is_ctx, 1, D)) + _mod_rows(mod_ref, is_ctx, 0, D)
    hb = h.astype(BF16)
    dot = functools.partial(jnp.dot, preferred_element_type=F32)
    zall = dot(hb, w_ref[...])
    na_ref[0] = zall[:, W_NA:W_DIFF].astype(BF16)
    d = zall[:, W_DIFF:W_RWKV]
    cos = cos_ref[...]
    sin = sin_ref[...]
    scale = DIFF_QK ** -0.5 * LOG2E
    dq_ref[0] = ((d[:, 0:256] * cos + d[:, 256:512] * sin) * scale).astype(BF16)
    dk_ref[0] = (d[:, 512:768] * cos + d[:, 768:1024] * sin).astype(BF16)
    dv_ref[0] = _with_ones(d[:, 1024:1280])
    rw_ref[0] = zall[:, W_RWKV:W_MLA]
    z = zall[:, W_MLA:W_TOT]
    cos5 = cos5_ref[...]
    sin5 = sin5_ref[...]
    nq = _rms(z[:, 0:256], qn_ref[...], NORM_EPS).astype(BF16)
    mscale = (HEAD_DIM + ROPE_DIM) ** -0.5 * LOG2E
    mq_ref[0] = ((dot(nq, wqm_ref[...]) * cos5 + dot(nq, wqr_ref[...]) * sin5) * mscale).astype(BF16)
    nkv = _rms(z[:, 256:384], kvn_ref[...], NORM_EPS).astype(BF16)
    kr = dot(z[:, 384:512].astype(BF16), p_ref[...])
    mk_ref[0] = (dot(nkv, wk_ref[...]) + kr[:, 0:512] * cos5 + kr[:, 512:1024] * sin5).astype(BF16)
    mv_ref[0] = _with_ones(dot(nkv, wv_ref[...]))


def _in_proj(X, mod2, g, w, cos256, sin256, cos512, sin512, qn, kvn, wqm, wqr, wk, wv, pmat, tm, C):
    B, Ta, D = X.shape
    nt = Ta // tm
    tok = lambda n: pl.BlockSpec((1, tm, n), lambda j, b: (b, j, 0))
    tab = lambda n: pl.BlockSpec((tm, n), lambda j, b: (j, 0))
    full = _resident
    return pl.pallas_call(
        functools.partial(_in_kernel, C=C),
        grid=(nt, B),
        in_specs=[tok(D), _mod_spec(D),
                  full(g), full(w), tab(256), tab(256), tab(512), tab(512),
                  full(qn), full(kvn), full(wqm), full(wqr), full(wk), full(wv), full(pmat)],
        out_specs=[tok(768), tok(256), tok(256), tok(512), tok(RWKV_COLS), tok(512), tok(512), tok(512)],
        out_shape=[jax.ShapeDtypeStruct((B, Ta, 768), BF16),
                   jax.ShapeDtypeStruct((B, Ta, 256), BF16),
                   jax.ShapeDtypeStruct((B, Ta, 256), BF16),
                   jax.ShapeDtypeStruct((B, Ta, 512), BF16),
                   jax.ShapeDtypeStruct((B, Ta, RWKV_COLS), F32),
                   jax.ShapeDtypeStruct((B, Ta, 512), BF16),
                   jax.ShapeDtypeStruct((B, Ta, 512), BF16),
                   jax.ShapeDtypeStruct((B, Ta, 512), BF16)],
        compiler_params=_cparams(("parallel", "parallel")),
        name="in_proj",
    )(X, mod2, g, w, cos256, sin256, cos512, sin512, qn, kvn, wqm, wqr, wk, wv, pmat)


def _softmax_parts(s):
    m = jnp.max(s, axis=-1, keepdims=True)
    e = jnp.exp2(s - m)
    return e, jnp.sum(e, axis=-1, keepdims=True)


def _with_ones(v):
    one = jnp.ones((v.shape[0], 128), F32)
    return jnp.concatenate([v[:, 0:128], one, v[:, 128:256], one], axis=1).astype(BF16)


def _exp_scores(s):
    sb = s.astype(BF16)
    return jnp.exp2(sb - jnp.max(sb, axis=-1, keepdims=True))


def _attend(s, vp1):
    r = jnp.dot(_exp_scores(s), vp1, preferred_element_type=F32)
    return r[:, 0:128] / r[:, 128:256]


def _mla_attn_body(q, k_ref, v_ref, nk):
    tq = q.shape[0]
    lane = lax.broadcasted_iota(jnp.int32, (tq, 128), 1)
    outs = []
    for pair in range(2):
        vp1 = v_ref[0, 0:nk, pair * 256:(pair + 1) * 256]
        o2 = [_attend(_dot_nt(q[:, h * 128:(h + 1) * 128], k_ref[0, 0:nk, h * 128:(h + 1) * 128]), vp1)
              for h in (2 * pair, 2 * pair + 1)]
        outs.append(jnp.where(lane < 64, o2[0], o2[1]))
    return jnp.concatenate(outs, axis=1)


def _diff_body(q, k_ref, v_ref, nk, lam):
    tq = q.shape[0]
    k = k_ref[0, 0:nk, :]
    sub = lax.broadcasted_iota(jnp.int32, (tq, GROUP_W), 1) >> 5
    lane = lax.broadcasted_iota(jnp.int32, (tq, 128), 1)
    zero = jnp.zeros((), BF16)
    outs = []
    for pair in range(2):
        vp1 = v_ref[0, 0:nk, pair * 256:(pair + 1) * 256]
        o2 = []
        for hh in range(2):
            h = 2 * pair + hh
            o2.append(_attend(_dot_nt(jnp.where(sub == 2 * h, q, zero), k), vp1)
                      - lam * _attend(_dot_nt(jnp.where(sub == 2 * h + 1, q, zero), k), vp1))
        outs.append(jnp.where(lane < 64, o2[0], o2[1]))
    return jnp.concatenate(outs, axis=1)


def _diff_finish(o, lam_init, sub_ref):
    ms = _head_sum(o * o, _head_ones()) * (1.0 / HEAD_DIM)
    return (o * lax.rsqrt(ms + DIFF_LN_EPS) * sub_ref[...]) * (1.0 - lam_init)


def _na_ctx_body(qt_ref, kv_ref, C):
    tq = qt_ref.shape[1]
    zero = jnp.zeros((), BF16)
    lane = lax.broadcasted_iota(jnp.int32, (tq, 128), 1)
    outs = []
    for pair in range(2):
        q2 = qt_ref[0, :, pair * 128:(pair + 1) * 128]
        kc = kv_ref[0, 0:C, 256 + pair * 128:256 + (pair + 1) * 128]
        vc = kv_ref[0, 0:C, 512 + pair * 128:512 + (pair + 1) * 128]
        o2 = []
        for hh in range(2):
            qm = jnp.where((lane < 64) if hh == 0 else (lane >= 64), q2, zero)
            e, l = _softmax_parts(_dot_nt(qm, kc))
            o2.append(jnp.dot(e.astype(BF16), vc, preferred_element_type=F32) * (1.0 / l))
        outs.append(jnp.where(lane < 64, o2[0], o2[1]))
    return jnp.concatenate(outs, axis=1)


def _na_lat_body(qt_ref, kv_ref, bias_ref, first_row, C, rows, R):
    zero = jnp.zeros((), BF16)
    lane = lax.broadcasted_iota(jnp.int32, (GRID_W, 128), 1)
    W = NA_KH * GRID_W
    dot = functools.partial(jnp.dot, preferred_element_type=F32)
    starts, deltas = [], []
    for rr in range(R):
        r = first_row + rr
        rs = jnp.clip(r - NA_KH // 2, 0, rows - NA_KH)
        deltas.append(r - rs)
        starts.append(pl.multiple_of(C + rs * GRID_W, GRID_W))
    units = [(rr, pair) for pair in range(2) for rr in range(R)]
    qs = {}
    for rr, pair in units:
        q2 = qt_ref[0, rr * GRID_W:(rr + 1) * GRID_W, pair * 128:(pair + 1) * 128]
        qs[rr, pair] = jnp.concatenate([jnp.where(lane < 64, q2, zero), jnp.where(lane >= 64, q2, zero)], axis=0)
    s_ctx = {}
    for pair in range(2):
        kc = kv_ref[0, 0:C, 256 + pair * 128:256 + (pair + 1) * 128]
        sc = _dot_nt(jnp.concatenate([qs[rr, pair] for rr in range(R)], axis=0), kc)
        for rr in range(R):
            s_ctx[rr, pair] = sc[rr * 128:(rr + 1) * 128]
    s_lat = {}
    for rr, pair in units:
        kw = kv_ref[0, pl.ds(starts[rr], W), 256 + pair * 128:256 + (pair + 1) * 128]
        bias = jnp.concatenate([bias_ref[deltas[rr], 2 * pair], bias_ref[deltas[rr], 2 * pair + 1]], axis=0)
        s_lat[rr, pair] = _dot_nt(qs[rr, pair], kw) + bias
    e_lat, e_ctx, inv = {}, {}, {}
    for u in units:
        m = jnp.maximum(jnp.max(s_lat[u], axis=-1, keepdims=True), jnp.max(s_ctx[u], axis=-1, keepdims=True))
        e_lat[u] = jnp.exp2(s_lat[u] - m)
        ec = jnp.exp2(s_ctx[u] - m)
        e_ctx[u] = ec.astype(BF16)
        inv[u] = 1.0 / (jnp.sum(e_lat[u], axis=-1, keepdims=True) + jnp.sum(ec, axis=-1, keepdims=True))
    o_ctx = {}
    for pair in range(2):
        vc = kv_ref[0, 0:C, 512 + pair * 128:512 + (pair + 1) * 128]
        oc = dot(jnp.concatenate([e_ctx[rr, pair] for rr in range(R)], axis=0), vc)
        for rr in range(R):
            o_ctx[rr, pair] = oc[rr * 128:(rr + 1) * 128]
    o = {}
    for rr, pair in units:
        vw = kv_ref[0, pl.ds(starts[rr], W), 512 + pair * 128:512 + (pair + 1) * 128]
        ou = (dot(e_lat[rr, pair].astype(BF16), vw) + o_ctx[rr, pair]) * inv[rr, pair]
        o[rr, pair] = jnp.where(lane < 64, ou[0:GRID_W], ou[GRID_W:])
    return jnp.concatenate([jnp.concatenate([o[rr, 0], o[rr, 1]], axis=1) for rr in range(R)], axis=0)


def _attn_kernel(naq_ref, nakv_ref, bias_ref, mq_ref, mk_ref, mv_ref, dq_ref, dk_ref, dv_ref, lam_ref, sub_ref,
                 o_ref, *, j0, nct, C, Ta, rows, lam_init):
    j = pl.program_id(1) + j0
    tq = naq_ref.shape[1]
    lp = lam_ref[...]
    lam = (jnp.exp(jnp.sum(lp[0:1] * lp[1:2], axis=-1, keepdims=True))
           - jnp.exp(jnp.sum(lp[2:3] * lp[3:4], axis=-1, keepdims=True)) + lam_init)

    if j0 < nct:
        @pl.when(j < nct)
        def _():
            o_ref[0] = jnp.concatenate(
                [_na_ctx_body(naq_ref, nakv_ref, C),
                 _mla_attn_body(mq_ref[0], mk_ref, mv_ref, C),
                 _diff_finish(_diff_body(dq_ref[0], dk_ref, dv_ref, C, lam), lam_init, sub_ref)],
                axis=1).astype(BF16)

    @pl.when(j >= nct)
    def _():
        R = tq // GRID_W
        o_ref[0] = jnp.concatenate(
            [_na_lat_body(naq_ref, nakv_ref, bias_ref, (j - nct) * R, C, rows, R),
             _mla_attn_body(mq_ref[0], mk_ref, mv_ref, Ta),
             _diff_finish(_diff_body(dq_ref[0], dk_ref, dv_ref, Ta, lam), lam_init, sub_ref)],
            axis=1).astype(BF16)


def _attention(na_qkv, bias, mq, mk, mv, dq, dk, dv, lam_p, sub, tq, j0, nct, C, rows, lam_init):
    B, Ta, _ = na_qkv.shape
    nq = Ta // tq - j0
    qt = lambda a: pl.BlockSpec((1, tq, a.shape[-1]), lambda b, j: (b, j + j0, 0))
    seq = lambda a: pl.BlockSpec((1, Ta, a.shape[-1]), lambda b, j: (b, 0, 0))
    kern = functools.partial(_attn_kernel, j0=j0, nct=nct, C=C, Ta=Ta, rows=rows, lam_init=lam_init)
    return pl.pallas_call(
        kern,
        grid=(B, nq),
        in_specs=[qt(na_qkv), seq(na_qkv), _resident(bias), qt(mq), seq(mk), seq(mv), qt(dq), seq(dk), seq(dv),
                  _resident(lam_p), _resident(sub)],
        out_specs=pl.BlockSpec((1, tq, 768), lambda b, j: (b, j + j0, 0)),
        out_shape=jax.ShapeDtypeStruct((B, Ta, 768), BF16),
        compiler_params=_cparams(("parallel", "arbitrary")),
        name="attention",
    )(na_qkv, na_qkv, bias, mq, mk, mv, dq, dk, dv, lam_p, sub)


def _rwkv_prep_kernel(z_ref, zp_ref, zn_ref, mu_ref, w0_ref, wup_ref, a0_ref, aup_ref, gup_ref,
                      kk_ref, ka_ref, rk_ref, rvk_ref, gb_ref, dir_ref, *, C, Ta):
    tm = z_ref.shape[1]
    z = z_ref[0]
    row = lax.broadcasted_iota(jnp.int32, (tm, 1), 0)
    pos = _token_rows(pl.program_id(0), tm)
    has_prev = jnp.where((pos == 0) | (pos == C), 0.0, 1.0)
    has_next = jnp.where((pos == C - 1) | (pos == Ta - 1), 0.0, 1.0)
    prev = jnp.where(row == 0, zp_ref[0, HALO - 1:HALO, :], pltpu.roll(z, 1, axis=0)) * has_prev
    nxt = jnp.where(row == tm - 1, zn_ref[0, 0:1, :], pltpu.roll(z, tm - 1, axis=0)) * has_next
    zs = z + mu_ref[0:1, :] * (prev - z) + mu_ref[1:2, :] * (nxt - z)
    r = zs[:, 0:256]
    k = zs[:, 256:512]
    v = zs[:, 512:768]
    low = zs[:, 768:896]
    ones = _head_ones()
    kk = k * kk_ref[...]
    kk = kk * lax.rsqrt(jnp.maximum(_head_sum(kk * kk, ones), 1e-24))
    g = _dot(_sigmoid(low), gup_ref[...])
    bonus = _head_sum(r * k * rk_ref[...], ones) * v
    rvk_ref[0, :, 0:256] = r
    rvk_ref[0, :, 256:512] = v
    rvk_ref[0, :, 512:768] = kk
    gb_ref[0, :, 0:256] = g
    gb_ref[0, :, 256:512] = bonus
    tl = jnp.tanh(low).astype(BF16)
    lb = low.astype(BF16)
    for d in range(2):
        wx = w0_ref[d:d + 1, :] + jnp.dot(tl, wup_ref[d], preferred_element_type=F32)
        a = _sigmoid(a0_ref[d:d + 1, :] + jnp.dot(lb, aup_ref[d], preferred_element_type=F32))
        dir_ref[d, 0, :, 0:256] = -math.exp(-0.5) * _sigmoid(wx)
        dir_ref[d, 0, :, 256:512] = k * (1.0 + (a - 1.0) * ka_ref[...])
        dir_ref[d, 0, :, 512:768] = kk * a


def _rwkv_prep(z, mu, w0, wup_p, a0, aup_p, gup_p, k_k, k_a, r_k, tm, C):
    B, Ta, _ = z.shape
    nt = Ta // tm
    hb = tm // HALO
    nh = Ta // HALO
    tok = lambda n: pl.BlockSpec((1, tm, n), lambda j, b: (b, j, 0))
    full = _resident
    kern = functools.partial(_rwkv_prep_kernel, C=C, Ta=Ta)
    return pl.pallas_call(
        kern,
        grid=(nt, B),
        in_specs=[tok(RWKV_COLS),
                  pl.BlockSpec((1, HALO, RWKV_COLS), lambda j, b: (b, jnp.maximum(j * hb - 1, 0), 0)),
                  pl.BlockSpec((1, HALO, RWKV_COLS), lambda j, b: (b, jnp.minimum((j + 1) * hb, nh - 1), 0)),
                  full(mu), full(w0), full(wup_p), full(a0), full(aup_p), full(gup_p),
                  full(k_k), full(k_a), full(r_k)],
        out_specs=[tok(768), tok(512),
                   pl.BlockSpec((2, 1, tm, 768), lambda j, b: (0, b, j, 0))],
        out_shape=[jax.ShapeDtypeStruct((B, Ta, 768), F32),
                   jax.ShapeDtypeStruct((B, Ta, 512), F32),
                   jax.ShapeDtypeStruct((2, B, Ta, 768), F32)],
        compiler_params=_cparams(("parallel", "parallel")),
        name="rwkv_prep",
    )(z, z, z, mu, w0, wup_p, a0, aup_p, gup_p, k_k, k_a, r_k)


def _rwkv_chunk_kernel(rvk_ref, dir_ref, g_ref, o_ref):
    d = pl.program_id(0)
    nchunk = rvk_ref.shape[1] // CHUNK
    bdm = _bd_mask()
    row = lax.broadcasted_iota(jnp.int32, (CHUNK, GROUP_W), 0)
    col = lax.broadcasted_iota(jnp.int32, (CHUNK, GROUP_W), 1) & (CHUNK - 1)
    tdiff = jnp.where(d == 0, row - col, col - row)
    incl = tdiff >= 0
    strict = tdiff > 0
    eye = tdiff == 0
    r2 = lax.broadcasted_iota(jnp.int32, (CHUNK, CHUNK), 0)
    c2 = lax.broadcasted_iota(jnp.int32, (CHUNK, CHUNK), 1)
    tri = jnp.where(jnp.where(d == 0, r2 - c2, c2 - r2) >= 0, 1.0, 0.0).astype(BF16)
    dot = functools.partial(jnp.dot, preferred_element_type=F32)

    def hmul(x, y):
        return dot(x.astype(BF16), _bd(y, bdm))

    def hmul_t(x, yt):
        ytb = yt.astype(BF16)
        return dot(x.astype(BF16), jnp.where(bdm, jnp.concatenate([ytb, ytb, ytb, ytb], axis=1),
                                             jnp.zeros((), BF16)))

    def head_t(x):
        xt = x.T
        return jnp.concatenate([xt[0:64], xt[64:128], xt[128:192], xt[192:256]], axis=1)

    cs = range(nchunk)
    sls = [slice(c * CHUNK, (c + 1) * CHUNK) for c in cs]
    r = [rvk_ref[0, sl, 0:256] for sl in sls]
    v = [rvk_ref[0, sl, 256:512] for sl in sls]
    kk = [rvk_ref[0, sl, 512:768] for sl in sls]
    lw = [dir_ref[0, 0, sl, 0:256] for sl in sls]
    kd = [dir_ref[0, 0, sl, 256:512] for sl in sls]
    b = [dir_ref[0, 0, sl, 512:768] for sl in sls]
    cum, tot = [], []
    for c in cs:
        l1, l2, l3 = _split3(lw[c])
        cum.append(dot(tri, l1) + (dot(tri, l2) + dot(tri, l3)))
        tot.append(jnp.sum(lw[c], axis=0, keepdims=True))
    rt = [r[c] * jnp.exp(cum[c]) for c in cs]
    at = [-kk[c] * jnp.exp(cum[c] - lw[c]) for c in cs]
    einv = [jnp.exp(-cum[c]) for c in cs]
    eend = [jnp.exp(tot[c] - cum[c]) for c in cs]
    ar = [jnp.concatenate([at[c], rt[c]], axis=0) for c in cs]
    xb = [hmul_t(ar[c], (b[c] * einv[c]).T) for c in cs]
    xk = [hmul_t(ar[c], (kd[c] * einv[c]).T) for c in cs]
    n = [jnp.where(strict, xb[c][0:CHUNK], 0.0) for c in cs]
    lrb = [jnp.where(incl, xb[c][CHUNK:], 0.0) for c in cs]
    bht = [head_t(b[c] * eend[c]) for c in cs]
    al = [jnp.concatenate([jnp.where(strict, xk[c][0:CHUNK], 0.0),
                           jnp.where(incl, xk[c][CHUNK:], 0.0),
                           head_t(kd[c] * eend[c])], axis=0) for c in cs]
    akv_lrkv = [hmul(al[c], v[c]) for c in cs]
    p = [jnp.where(eye, 1.0, 0.0) + n[c] for c in cs]
    npow = [hmul(n[c], n[c]) for c in cs]
    for _ in range(4):
        sq = [hmul(jnp.concatenate([npow[c], p[c]], axis=0), npow[c]) for c in cs]
        npow = [sq[c][0:CHUNK] for c in cs]
        p = [p[c] + sq[c][CHUNK:] for c in cs]
    t = [p[c] + hmul(p[c], npow[c]) for c in cs]
    w = [hmul(t[c], at[c]) for c in cs]
    u0 = [hmul(t[c], akv_lrkv[c][0:CHUNK]) for c in cs]
    lb = [jnp.concatenate([lrb[c], bht[c]], axis=0) for c in cs]
    xw = [hmul(lb[c], w[c]) for c in cs]
    xu = [hmul(lb[c], u0[c]) for c in cs]
    gs, outs = [], []
    for c in cs:
        gs.append(xw[c][CHUNK:] + jnp.where(eye, jnp.exp(tot[c]), 0.0))
        outs.append(jnp.concatenate([rt[c] + xw[c][0:CHUNK],
                                     xu[c][0:CHUNK] + akv_lrkv[c][CHUNK:2 * CHUNK],
                                     xu[c][CHUNK:] + akv_lrkv[c][2 * CHUNK:]], axis=1))
    g_ref[0, 0] = jnp.concatenate(gs, axis=0)
    o_ref[0, 0] = jnp.concatenate(outs, axis=0).astype(BF16)


def _rwkv_chunks(rvk, dirp, tm):
    B, Ta, _ = rvk.shape
    nt = Ta // tm
    blk = lambda n: pl.BlockSpec((1, 1, tm, n), lambda d, b, j: (d, b, j, 0))
    return pl.pallas_call(
        _rwkv_chunk_kernel,
        grid=(2, B, nt),
        in_specs=[pl.BlockSpec((1, tm, 768), lambda d, b, j: (b, j, 0)), blk(768)],
        out_specs=[blk(256), blk(768)],
        out_shape=[jax.ShapeDtypeStruct((2, B, Ta, 256), F32), jax.ShapeDtypeStruct((2, B, Ta, 768), BF16)],
        compiler_params=_cparams(("parallel", "parallel", "parallel")),
        name="rwkv_chunks",
    )(rvk, dirp)


def _rwkv_scan_kernel(gf_ref, cf_ref, gr_ref, cr_ref, yf_ref, yb_ref, s_ref):
    i = pl.program_id(1)
    gb = gf_ref.shape[1]
    bdm = _bd_mask()

    @pl.when(i == 0)
    def _():
        s_ref[...] = jnp.zeros(s_ref.shape, F32)

    dot = functools.partial(jnp.dot, preferred_element_type=F32)
    zero = jnp.zeros((), BF16)
    chains = [(d, bb) for d in range(2) for bb in range(gb)]
    g_refs = (gf_ref, gr_ref)
    c_refs = (cf_ref, cr_ref)
    parts = {}
    for d, bb in chains:
        sh, sl = _split2(s_ref[d, bb])
        gh, gl = _split2(g_refs[d][0, bb])
        parts[d, bb] = (jnp.concatenate([c_refs[d][0, bb, :, 0:256], gh], axis=0), gl,
                        jnp.where(bdm, jnp.concatenate([sh] * 4, axis=0), zero),
                        jnp.where(bdm, jnp.concatenate([sl] * 4, axis=0), zero))
    res = {}
    for ch in chains:
        lh, gl, sbh, sbl = parts[ch]
        r = dot(lh, sbh) + dot(lh, sbl)
        res[ch] = (r[0:CHUNK], r[CHUNK:] + dot(gl, sbh))
    for d, y_ref in enumerate((yf_ref, yb_ref)):
        for bb in range(gb):
            y_ref[bb] = res[d, bb][0] + c_refs[d][0, bb, :, 256:512].astype(F32)
            s_ref[d, bb] = res[d, bb][1] + c_refs[d][0, bb, :, 512:768].astype(F32)


def _rwkv_scan(chg, chb, ncc, gb):
    _, B, Ta, _ = chg.shape
    nc = Ta // CHUNK

    def rev_chunk(i):
        return jnp.where(i < ncc, ncc - 1 - i, nc - 1 - (i - ncc))

    fwd = lambda n: pl.BlockSpec((1, gb, CHUNK, n), lambda b, i: (0, b, i, 0))
    rev = lambda n: pl.BlockSpec((1, gb, CHUNK, n), lambda b, i: (1, b, rev_chunk(i), 0))
    return pl.pallas_call(
        _rwkv_scan_kernel,
        grid=(B // gb, nc),
        in_specs=[fwd(256), fwd(768), rev(256), rev(768)],
        out_specs=[pl.BlockSpec((gb, CHUNK, 256), lambda b, i: (b, i, 0)),
                   pl.BlockSpec((gb, CHUNK, 256), lambda b, i: (b, rev_chunk(i), 0))],
        out_shape=[jax.ShapeDtypeStruct((B, Ta, 256), F32),
                   jax.ShapeDtypeStruct((B, Ta, 256), F32)],
        scratch_shapes=[pltpu.VMEM((2, gb, CHUNK, GROUP_W), F32)],
        compiler_params=_cparams(("parallel", "arbitrary")),
        name="rwkv_scan",
    )(chg, chb, chg, chb)


def _out_kernel(x_ref, mod_ref, att_ref, yf_ref, yb_ref, gb_ref, lnw_ref, lnb_ref, w_ref, o_ref, *, j0, C):
    x = x_ref[0]
    tm, D = x.shape
    is_ctx = _token_rows(pl.program_id(0) + j0, tm) < C
    dot = functools.partial(jnp.dot, preferred_element_type=F32)
    ones = _head_ones()
    y = yf_ref[0] + yb_ref[0]
    mean = _head_sum(y, ones) * (1.0 / HEAD_DIM)
    yc = y - mean
    var = _head_sum(yc * yc, ones) * (1.0 / HEAD_DIM)
    yn = yc * lax.rsqrt(var + RWKV_LN_EPS) * lnw_ref[...] + lnb_ref[...]
    rw = ((yn + gb_ref[0, :, 256:512]) * gb_ref[0, :, 0:256]).astype(BF16)
    mix = (dot(att_ref[0, :, 0:512], w_ref[0:512, :]) + dot(rw, w_ref[512:768, :])) + \
        dot(att_ref[0, :, 512:768], w_ref[768:1024, :])
    o_ref[0] = x + _mod_rows(mod_ref, is_ctx, 2, D) * mix


def _out_proj(X, mod2, att, yf, yb, gbn, ln_w, ln_b, w, tm, j0, C):
    B, Ta, D = X.shape
    nt = Ta // tm - j0
    tok = lambda n: pl.BlockSpec((1, tm, n), lambda j, b: (b, j + j0, 0))
    full = _resident
    return pl.pallas_call(
        functools.partial(_out_kernel, j0=j0, C=C),
        grid=(nt, B),
        in_specs=[tok(D), _mod_spec(D), tok(768), tok(256), tok(256), tok(512), full(ln_w), full(ln_b), full(w)],
        out_specs=tok(D),
        out_shape=jax.ShapeDtypeStruct((B, Ta, D), F32),
        compiler_params=_cparams(("parallel", "parallel")),
        name="out_proj",
    )(X, mod2, att, yf, yb, gbn, ln_w, ln_b, w)


def _mlp_kernel(x_ref, xp_ref, xn_ref, mod_ref, g_ref, wa_ref, wb_ref, cw_ref, cb_ref, wd_ref, gf_ref, o_ref,
                *, j0, C, Ta, fc, final):
    j = pl.program_id(0) + j0
    tm = x_ref.shape[1]
    D = x_ref.shape[2]
    dff = wa_ref.shape[1]
    x = x_ref[0]
    xe = jnp.concatenate([xp_ref[0], x, xn_ref[0]], axis=0)
    pos = _token_rows(j, tm)
    ctx_e = (j * tm - HALO + lax.broadcasted_iota(jnp.int32, (tm + 2 * HALO, 1), 0)) < C
    h = (_rms(xe, g_ref[...], NORM_EPS) * (1.0 + _mod_rows(mod_ref, ctx_e, 4, D))
         + _mod_rows(mod_ref, ctx_e, 3, D)).astype(BF16)
    pmask = jnp.where((pos == 0) | (pos == C), 0.0, 1.0)
    nmask = jnp.where((pos == C - 1) | (pos == Ta - 1), 0.0, 1.0)
    dot = functools.partial(jnp.dot, preferred_element_type=F32)
    acc = jnp.zeros((tm, D), F32)
    for c in range(dff // fc):
        cs = slice(c * fc, (c + 1) * fc)
        a = dot(h, wa_ref[:, cs])
        b = dot(h[HALO:HALO + tm], wb_ref[:, cs])
        cv = (cw_ref[0:1, cs] * (a[HALO - 1:HALO - 1 + tm] * pmask) + cw_ref[1:2, cs] * a[HALO:HALO + tm]
              + cw_ref[2:3, cs] * (a[HALO + 1:HALO + 1 + tm] * nmask) + cb_ref[:, cs])
        u = cv * _sigmoid(cv) * b
        acc = acc + dot(u.astype(BF16), wd_ref[cs, :])
    y = x + _mod_rows(mod_ref, pos < C, 5, D) * acc
    o_ref[0] = _rms(y, gf_ref[...], NORM_EPS) if final else y


def _mlp(X, mod2, g, wa, wb, cw, cb, wd, gf, tm, j0, C, final):
    B, Ta, D = X.shape
    ntot = Ta // tm
    nt = ntot - j0
    hb = tm // HALO
    nh = Ta // HALO
    tok = lambda n: pl.BlockSpec((1, tm, n), lambda j, b: (b, j + j0, 0))
    full = _resident
    kern = functools.partial(_mlp_kernel, j0=j0, C=C, Ta=Ta, fc=wa.shape[1], final=final)
    if final:
        assert j0 * tm == C
        out_spec = pl.BlockSpec((1, tm, D), lambda j, b: (b, j, 0))
        out_shape = jax.ShapeDtypeStruct((B, Ta - C, D), F32)
    else:
        out_spec, out_shape = tok(D), jax.ShapeDtypeStruct((B, Ta, D), F32)
    return pl.pallas_call(
        kern,
        grid=(nt, B),
        in_specs=[tok(D),
                  pl.BlockSpec((1, HALO, D), lambda j, b: (b, jnp.maximum((j + j0) * hb - 1, j0 * hb), 0)),
                  pl.BlockSpec((1, HALO, D), lambda j, b: (b, jnp.minimum((j + j0 + 1) * hb, nh - 1), 0)),
                  _mod_spec(D),
                  full(g), full(wa), full(wb), full(cw), full(cb), full(wd), full(gf)],
        out_specs=out_spec,
        out_shape=out_shape,
        compiler_params=_cparams(("parallel", "parallel")),
        name="conv_glu",
    )(X, X, X, mod2, g, wa, wb, cw, cb, wd, gf)


def _rot_cols(w):
    s = w.shape
    x = w.reshape(s[:-1] + (s[-1] // ROPE_DIM, 4, ROPE_DIM // 4))
    r1, r2, c1, c2 = x[..., 0, :], x[..., 1, :], x[..., 2, :], x[..., 3, :]
    return jnp.stack([-r2, r1, -c2, c1], axis=-2).reshape(s)


def _rope_tables(T, C):
    t = np.arange(T)
    rowp = (t // GRID_W).astype(np.float32)
    colp = (t % GRID_W).astype(np.float32)
    half = ROPE_DIM // 2
    freqs = jnp.asarray(ROPE_THETA, F32) ** (-jnp.arange(0, half, 2, dtype=F32) / half)
    ar = jnp.asarray(rowp)[:, None] * freqs[None, :]
    ac = jnp.asarray(colp)[:, None] * freqs[None, :]
    ang = jnp.concatenate([ar, ar, ac, ac], axis=-1)
    cos = jnp.concatenate([jnp.ones((C, ROPE_DIM), F32), jnp.cos(ang)], axis=0)
    sin = jnp.concatenate([jnp.zeros((C, ROPE_DIM), F32), jnp.sin(ang)], axis=0)
    Ta = T + C
    cos256 = jnp.tile(cos, (1, 8))
    sin256 = jnp.tile(sin, (1, 8))
    one = jnp.ones((Ta, HEAD_DIM), F32)
    zero = jnp.zeros((Ta, HEAD_DIM), F32)
    cos512 = jnp.tile(jnp.concatenate([one, cos, one[:, :32]], axis=1), (1, 4))
    sin512 = jnp.tile(jnp.concatenate([zero, sin, zero[:, :32]], axis=1), (1, 4))
    return cos256, sin256, cos512, sin512


def _na_bias_tables(rpb):
    cpos = np.arange(GRID_W)
    cstart = np.clip(cpos - NA_KW // 2, 0, GRID_W - NA_KW)
    col_mask = (cpos[None, :] >= cstart[:, None]) & (cpos[None, :] < cstart[:, None] + NA_KW)
    col_idx = np.clip(cpos[None, :] - cpos[:, None] + NA_KW - 1, 0, 2 * NA_KW - 2)
    tabs = []
    for delta in range(NA_KH):
        row_off = np.arange(NA_KH) - delta + NA_KH - 1
        bias = rpb[:, :, row_off][:, :, :, col_idx]
        bias = jnp.where(jnp.asarray(col_mask)[None, None, None], bias * LOG2E, NEG_INF)
        tabs.append(bias.transpose(0, 1, 3, 2, 4).reshape(rpb.shape[0], N_HEADS, GRID_W, NA_KH * GRID_W))
    return jnp.stack(tabs, axis=1)


def kernel(x, c, ctx, c_ctx, norm1_g, norm2_g, ada_w, ada_b, w_in, w_out, na_rpb, mla_q_norm, mla_kv_norm,
           mla_w_uq, mla_w_ukv, rwkv_mu, rwkv_w0, rwkv_w_up, rwkv_a0, rwkv_a_up, rwkv_g_up, rwkv_k_k,
           rwkv_k_a, rwkv_r_k, rwkv_ln_w, rwkv_ln_b, diff_lambda, diff_subln, mlp_w_up, mlp_conv_w,
           mlp_conv_b, mlp_w_down, final_norm_g):
    B, T, D = x.shape
    C = ctx.shape[1]
    L = ada_w.shape[0]
    Ta = T + C
    tm = min(TOKEN_TILE, C)
    rows = T // GRID_W
    assert C % tm == 0 and T % tm == 0 and tm % CHUNK == 0 and rows >= NA_KH and D == 1024
    nct = C // tm
    t_wide = WIDE_TILE if Ta % WIDE_TILE == 0 else tm
    t_mlp = MLP_TILE if Ta % MLP_TILE == 0 else tm
    dff = mlp_w_down.shape[1]

    wi = w_in
    na_w = wi[:, :, 0:768]
    cq_w, ckv_w, kr_w = wi[:, :, 768:1024], wi[:, :, 1024:1152], wi[:, :, 1152:1184]
    rw_w = wi[:, :, 1184:2080]
    dq_w, dk_w, dv_w = wi[:, :, 2080:2336], wi[:, :, 2336:2592], wi[:, :, 2592:2848]
    col_scale = np.ones((W_TOT,), np.float32)
    col_scale[0:256] = HEAD_DIM ** -0.5 * LOG2E
    w_all = (jnp.concatenate([na_w, dq_w, _rot_cols(dq_w), dk_w, _rot_cols(dk_w), dv_w, rw_w,
                              cq_w, ckv_w, kr_w, _rot_cols(kr_w), jnp.zeros((L, D, 64), F32)], axis=-1)
             * jnp.asarray(col_scale)).astype(BF16)
    w_out_b = w_out.astype(BF16)
    wa_b = mlp_w_up[:, :, :dff].astype(BF16)
    wb_b = mlp_w_up[:, :, dff:].astype(BF16)
    wd_b = mlp_w_down.astype(BF16)

    uq = mla_w_uq.reshape(L, MLA_Q_RANK, N_HEADS, HEAD_DIM + ROPE_DIM)
    pad32 = jnp.zeros((L, MLA_Q_RANK, N_HEADS, 32), F32)
    wqm = jnp.concatenate([uq, pad32], axis=-1).reshape(L, MLA_Q_RANK, 512).astype(BF16)
    wqr = jnp.concatenate([jnp.zeros_like(uq[..., :HEAD_DIM]), _rot_cols(uq[..., HEAD_DIM:]), pad32],
                          axis=-1).reshape(L, MLA_Q_RANK, 512).astype(BF16)
    ukv = mla_w_ukv.reshape(L, MLA_KV_RANK, N_HEADS, 2 * HEAD_DIM)
    wk = jnp.concatenate([ukv[..., :HEAD_DIM], jnp.zeros_like(ukv[..., HEAD_DIM:])],
                         axis=-1).reshape(L, MLA_KV_RANK, 512).astype(BF16)
    wv = ukv[..., HEAD_DIM:].reshape(L, MLA_KV_RANK, 256).astype(BF16)
    pm = np.zeros((128, 1024), np.float32)
    for h in range(N_HEADS):
        for i in range(ROPE_DIM):
            pm[i, h * 128 + HEAD_DIM + i] = 1.0
            pm[ROPE_DIM + i, 512 + h * 128 + HEAD_DIM + i] = 1.0
    pmat = jnp.asarray(pm, BF16)

    zr = lambda n: jnp.zeros((L, 2, n, GROUP_W), F32)
    wup_p = jnp.concatenate([rwkv_w_up, zr(96)], axis=2).astype(BF16)
    aup_p = jnp.concatenate([zr(32), rwkv_a_up, zr(64)], axis=2).astype(BF16)
    gup_p = jnp.concatenate([jnp.zeros((L, 64, GROUP_W), F32), rwkv_g_up], axis=1).astype(BF16)

    cos256, sin256, cos512, sin512 = _rope_tables(T, C)
    na_bias = _na_bias_tables(na_rpb)
    sub256 = jnp.tile(diff_subln, (1, N_HEADS))

    R = ((B + 1 + 7) // 8) * 8
    cc = jnp.concatenate([c, c_ctx[None], jnp.zeros((R - B - 1, D), F32)], axis=0)
    mod = _modulation(cc, ada_w, ada_b)
    mod2 = jnp.stack([jnp.broadcast_to(mod[:, B:B + 1], (L, B, 6 * D)), mod[:, :B]], axis=2)
    mod2 = mod2.reshape(L, 2 * B, 1, 6 * D)

    X = jnp.concatenate([ctx, x], axis=1)
    gb = next(n for n in (8, 4, 2, 1) if B % n == 0)
    for l in range(L):
        need_ctx = l < L - 1
        j0 = 0 if need_ctx else nct
        na_qkv, dq, dk, dv, z_rw, mq, mk, mv = _in_proj(
            X, mod2[l], norm1_g[l][None], w_all[l], cos256, sin256, cos512, sin512,
            mla_q_norm[l][None], mla_kv_norm[l][None], wqm[l], wqr[l], wk[l], wv[l], pmat, t_wide, C)
        lam_init = 0.8 - 0.6 * math.exp(-0.3 * l)
        att = _attention(na_qkv, na_bias[l], mq, mk, mv, dq, dk, dv, diff_lambda[l], sub256[l][None],
                         tm, j0, nct, C, rows, lam_init)
        rvk, gbn, dirp = _rwkv_prep(z_rw, rwkv_mu[l], rwkv_w0[l], wup_p[l], rwkv_a0[l], aup_p[l], gup_p[l],
                                    rwkv_k_k[l][None], rwkv_k_a[l][None], rwkv_r_k[l].reshape(1, GROUP_W),
                                    t_wide, C)
        chg, chb = _rwkv_chunks(rvk, dirp, t_wide)
        yf, yb = _rwkv_scan(chg, chb, C // CHUNK, gb)
        X = _out_proj(X, mod2[l], att, yf, yb, gbn, rwkv_ln_w[l][None], rwkv_ln_b[l][None],
                      w_out_b[l], t_wide if need_ctx else tm, j0, C)
        X = _mlp(X, mod2[l], norm2_g[l][None], wa_b[l], wb_b[l], mlp_conv_w[l], mlp_conv_b[l][None], wd_b[l],
                 final_norm_g[None], t_mlp if need_ctx else tm, j0, C, final=not need_ctx)
    return X
```

```python
import functools
import math

import jax
import jax.numpy as jnp
import numpy as np
from jax import lax
from jax.experimental import pallas as pl
from jax.experimental.pallas import tpu as pltpu

F32 = jnp.float32
BF16 = jnp.bfloat16

GRID_W = 64
GROUP_W = 256
HEAD_DIM = 64
N_HEADS = 4
ROPE_DIM = 32
ROPE_THETA = 10000.0
NORM_EPS = 1e-6
NEG_INF = -1e30
LOG2E = math.log2(math.e)
NA_KH = 8
NA_KW = 16
MLA_Q_RANK = 256
MLA_KV_RANK = 128
RWKV_COLS = 896
RWKV_LN_EPS = 64e-5
DIFF_QK = 32
DIFF_LN_EPS = 1e-5
IN_SPLITS = (768, 1184, 2080, 2848)

W_NA = 0
W_DIFF = 768
W_RWKV = 2048
W_MLA = 2944
W_TOT = 3456

TOKEN_TILE = 256
CHUNK = 64
WIDE_TILE = 768
MLP_TILE = 384
HALO = 8
MLP_HALO = 16
VMEM_LIMIT = 56 * 1024 * 1024


def _cparams(sem):
    return pltpu.CompilerParams(dimension_semantics=sem, vmem_limit_bytes=VMEM_LIMIT)


def _resident(a):
    return pl.BlockSpec(a.shape, lambda *_: (0,) * a.ndim, pipeline_mode=pl.Buffered(1))


def _dot(a, b):
    return jnp.dot(a.astype(BF16), b.astype(BF16), preferred_element_type=F32)


def _dot_nt(a, b):
    return lax.dot_general(a.astype(BF16), b.astype(BF16), (((1,), (1,)), ((), ())),
                           preferred_element_type=F32)


def _split2(x):
    hi = x.astype(BF16)
    lo = (x - hi.astype(F32)).astype(BF16)
    return hi, lo


def _split3(x):
    h1 = x.astype(BF16)
    r1 = x - h1.astype(F32)
    h2 = r1.astype(BF16)
    h3 = (r1 - h2.astype(F32)).astype(BF16)
    return h1, h2, h3


def _dot3(a, b):
    ah, al = _split2(a)
    bh, bl = _split2(b)
    d = functools.partial(jnp.dot, preferred_element_type=F32)
    return d(ah, bh) + (d(ah, bl) + d(al, bh))


def _sigmoid(x):
    return 1.0 / (1.0 + jnp.exp(-x))


def _rms(x, g, eps):
    return x * lax.rsqrt(jnp.mean(x * x, axis=-1, keepdims=True) + eps) * g


def _token_rows(j, tm):
    return j * tm + lax.broadcasted_iota(jnp.int32, (tm, 1), 0)


def _mod_rows(mod_ref, is_ctx, k, D):
    return jnp.where(is_ctx, mod_ref[0][:, k * D:(k + 1) * D], mod_ref[1][:, k * D:(k + 1) * D])


def _mod_spec(D):
    return pl.BlockSpec((2, 1, 6 * D), lambda j, b: (b, 0, 0))


def _head_ones(n=GROUP_W):
    r = lax.broadcasted_iota(jnp.int32, (n, n), 0) >> 6
    c = lax.broadcasted_iota(jnp.int32, (n, n), 1) >> 6
    return jnp.where(r == c, 1.0, 0.0).astype(BF16)


def _head_sum(x, ones):
    hi, lo = _split2(x)
    d = functools.partial(jnp.dot, preferred_element_type=F32)
    return d(hi, ones) + d(lo, ones)


def _bd_mask():
    r = lax.broadcasted_iota(jnp.int32, (GROUP_W, GROUP_W), 0) >> 6
    c = lax.broadcasted_iota(jnp.int32, (GROUP_W, GROUP_W), 1) >> 6
    return r == c


def _bd(x, mask):
    xb = x.astype(BF16)
    return jnp.where(mask, jnp.concatenate([xb, xb, xb, xb], axis=0), jnp.zeros((), BF16))


def _mod_kernel(s_ref, w_ref, b_ref, o_ref):
    s = s_ref[...]
    s = s * _sigmoid(s)
    o_ref[0] = _dot3(s, w_ref[0]) + b_ref[0]


def _modulation(cc, ada_w, ada_b):
    L, D, N = ada_w.shape
    R = cc.shape[0]
    tn = 1536
    return pl.pallas_call(
        _mod_kernel,
        grid=(L, N // tn),
        in_specs=[pl.BlockSpec((R, D), lambda l, n: (0, 0)),
                  pl.BlockSpec((1, D, tn), lambda l, n: (l, 0, n)),
                  pl.BlockSpec((1, 1, tn), lambda l, n: (l, 0, n))],
        out_specs=pl.BlockSpec((1, R, tn), lambda l, n: (l, 0, n)),
        out_shape=jax.ShapeDtypeStruct((L, R, N), F32),
        compiler_params=_cparams(("parallel", "parallel")),
        name="adaln_mod",
    )(cc, ada_w, ada_b.reshape(L, 1, N))


def _in_kernel(x_ref, mod_ref, g_ref, w_ref, cos_ref, sin_ref, cos5_ref, sin5_ref,
               qn_ref, kvn_ref, wqm_ref, wqr_ref, wk_ref, wv_ref, p_ref,
               na_ref, dq_ref, dk_ref, dv_ref, rw_ref, mq_ref, mk_ref, mv_ref, *, C):
    x = x_ref[0]
    tm, D = x.shape
    is_ctx = _token_rows(pl.program_id(0), tm) < C
    h = _rms(x, g_ref[...], NORM_EPS) * (1.0 + _mod_rows(mod_ref, is_ctx, 1, D)) + _mod_rows(mod_ref, is_ctx, 0, D)
    hb = h.astype(BF16)
    dot = functools.partial(jnp.dot, preferred_element_type=F32)
    zall = dot(hb, w_ref[...])
    na_ref[0] = zall[:, W_NA:W_DIFF].astype(BF16)
    d = zall[:, W_DIFF:W_RWKV]
    cos = cos_ref[...]
    sin = sin_ref[...]
    scale = DIFF_QK ** -0.5 * LOG2E
    dq_ref[0] = ((d[:, 0:256] * cos + d[:, 256:512] * sin) * scale).astype(BF16)
    dk_ref[0] = (d[:, 512:768] * cos + d[:, 768:1024] * sin).astype(BF16)
    dv_ref[0] = _with_ones(d[:, 1024:1280])
    rw_ref[0] = zall[:, W_RWKV:W_MLA]
    z = zall[:, W_MLA:W_TOT]
    cos5 = cos5_ref[...]
    sin5 = sin5_ref[...]
    nq = _rms(z[:, 0:256], qn_ref[...], NORM_EPS).astype(BF16)
    mscale = (HEAD_DIM + ROPE_DIM) ** -0.5 * LOG2E
    mq_ref[0] = ((dot(nq, wqm_ref[...]) * cos5 + dot(nq, wqr_ref[...]) * sin5) * mscale).astype(BF16)
    nkv = _rms(z[:, 256:384], kvn_ref[...], NORM_EPS).astype(BF16)
    kr = dot(z[:, 384:512].astype(BF16), p_ref[...])
    mk_ref[0] = (dot(nkv, wk_ref[...]) + kr[:, 0:512] * cos5 + kr[:, 512:1024] * sin5).astype(BF16)
    mv_ref[0] = _with_ones(dot(nkv, wv_ref[...]))


def _in_proj(X, mod2, g, w, cos256, sin256, cos512, sin512, qn, kvn, wqm, wqr, wk, wv, pmat, tm, C):
    B, Ta, D = X.shape
    nt = Ta // tm
    tok = lambda n: pl.BlockSpec((1, tm, n), lambda j, b: (b, j, 0))
    tab = lambda n: pl.BlockSpec((tm, n), lambda j, b: (j, 0))
    full = _resident
    return pl.pallas_call(
        functools.partial(_in_kernel, C=C),
        grid=(nt, B),
        in_specs=[tok(D), _mod_spec(D),
                  full(g), full(w), tab(256), tab(256), tab(512), tab(512),
                  full(qn), full(kvn), full(wqm), full(wqr), full(wk), full(wv), full(pmat)],
        out_specs=[tok(768), tok(256), tok(256), tok(512), tok(RWKV_COLS), tok(512), tok(512), tok(512)],
        out_shape=[jax.ShapeDtypeStruct((B, Ta, 768), BF16),
                   jax.ShapeDtypeStruct((B, Ta, 256), BF16),
                   jax.ShapeDtypeStruct((B, Ta, 256), BF16),
                   jax.ShapeDtypeStruct((B, Ta, 512), BF16),
                   jax.ShapeDtypeStruct((B, Ta, RWKV_COLS), F32),
                   jax.ShapeDtypeStruct((B, Ta, 512), BF16),
                   jax.ShapeDtypeStruct((B, Ta, 512), BF16),
                   jax.ShapeDtypeStruct((B, Ta, 512), BF16)],
        compiler_params=_cparams(("parallel", "parallel")),
        name="in_proj",
    )(X, mod2, g, w, cos256, sin256, cos512, sin512, qn, kvn, wqm, wqr, wk, wv, pmat)


def _softmax_parts(s):
    m = jnp.max(s, axis=-1, keepdims=True)
    e = jnp.exp2(s - m)
    return e, jnp.sum(e, axis=-1, keepdims=True)


def _with_ones(v):
    one = jnp.ones((v.shape[0], 128), F32)
    return jnp.concatenate([v[:, 0:128], one, v[:, 128:256], one], axis=1).astype(BF16)


def _exp_scores(s):
    sb = s.astype(BF16)
    return jnp.exp2(sb - jnp.max(sb, axis=-1, keepdims=True))


def _attend(s, vp1):
    r = jnp.dot(_exp_scores(s), vp1, preferred_element_type=F32)
    return r[:, 0:128] / r[:, 128:256]


def _mla_attn_body(q, k_ref, v_ref, nk):
    tq = q.shape[0]
    lane = lax.broadcasted_iota(jnp.int32, (tq, 128), 1)
    outs = []
    for pair in range(2):
        vp1 = v_ref[0, 0:nk, pair * 256:(pair + 1) * 256]
        o2 = [_attend(_dot_nt(q[:, h * 128:(h + 1) * 128], k_ref[0, 0:nk, h * 128:(h + 1) * 128]), vp1)
              for h in (2 * pair, 2 * pair + 1)]
        outs.append(jnp.where(lane < 64, o2[0], o2[1]))
    return jnp.concatenate(outs, axis=1)


def _diff_body(q, k_ref, v_ref, nk, lam):
    tq = q.shape[0]
    k = k_ref[0, 0:nk, :]
    sub = lax.broadcasted_iota(jnp.int32, (tq, GROUP_W), 1) >> 5
    lane = lax.broadcasted_iota(jnp.int32, (tq, 128), 1)
    zero = jnp.zeros((), BF16)
    outs = []
    for pair in range(2):
        vp1 = v_ref[0, 0:nk, pair * 256:(pair + 1) * 256]
        o2 = []
        for hh in range(2):
            h = 2 * pair + hh
            o2.append(_attend(_dot_nt(jnp.where(sub == 2 * h, q, zero), k), vp1)
                      - lam * _attend(_dot_nt(jnp.where(sub == 2 * h + 1, q, zero), k), vp1))
        outs.append(jnp.where(lane < 64, o2[0], o2[1]))
    return jnp.concatenate(outs, axis=1)


def _diff_finish(o, lam_init, sub_ref):
    ms = _head_sum(o * o, _head_ones()) * (1.0 / HEAD_DIM)
    return (o * lax.rsqrt(ms + DIFF_LN_EPS) * sub_ref[...]) * (1.0 - lam_init)


def _na_ctx_body(qt_ref, kv_ref, C):
    tq = qt_ref.shape[1]
    zero = jnp.zeros((), BF16)
    lane = lax.broadcasted_iota(jnp.int32, (tq, 128), 1)
    outs = []
    for pair in range(2):
        q2 = qt_ref[0, :, pair * 128:(pair + 1) * 128]
        kc = kv_ref[0, 0:C, 256 + pair * 128:256 + (pair + 1) * 128]
        vc = kv_ref[0, 0:C, 512 + pair * 128:512 + (pair + 1) * 128]
        o2 = []
        for hh in range(2):
            qm = jnp.where((lane < 64) if hh == 0 else (lane >= 64), q2, zero)
            e, l = _softmax_parts(_dot_nt(qm, kc))
            o2.append(jnp.dot(e.astype(BF16), vc, preferred_element_type=F32) * (1.0 / l))
        outs.append(jnp.where(lane < 64, o2[0], o2[1]))
    return jnp.concatenate(outs, axis=1)


def _na_lat_body(qt_ref, kv_ref, bias_ref, first_row, C, rows, R):
    zero = jnp.zeros((), BF16)
    lane = lax.broadcasted_iota(jnp.int32, (GRID_W, 128), 1)
    W = NA_KH * GRID_W
    dot = functools.partial(jnp.dot, preferred_element_type=F32)
    starts, deltas = [], []
    for rr in range(R):
        r = first_row + rr
        rs = jnp.clip(r - NA_KH // 2, 0, rows - NA_KH)
        deltas.append(r - rs)
        starts.append(pl.multiple_of(C + rs * GRID_W, GRID_W))
    units = [(rr, pair) for pair in range(2) for rr in range(R)]
    qs = {}
    for rr, pair in units:
        q2 = qt_ref[0, rr * GRID_W:(rr + 1) * GRID_W, pair * 128:(pair + 1) * 128]
        qs[rr, pair] = jnp.concatenate([jnp.where(lane < 64, q2, zero), jnp.where(lane >= 64, q2, zero)], axis=0)
    s_ctx = {}
    for pair in range(2):
        kc = kv_ref[0, 0:C, 256 + pair * 128:256 + (pair + 1) * 128]
        sc = _dot_nt(jnp.concatenate([qs[rr, pair] for rr in range(R)], axis=0), kc)
        for rr in range(R):
            s_ctx[rr, pair] = sc[rr * 128:(rr + 1) * 128]
    s_lat = {}
    for rr, pair in units:
        kw = kv_ref[0, pl.ds(starts[rr], W), 256 + pair * 128:256 + (pair + 1) * 128]
        bias = jnp.concatenate([bias_ref[deltas[rr], 2 * pair], bias_ref[deltas[rr], 2 * pair + 1]], axis=0)
        s_lat[rr, pair] = _dot_nt(qs[rr, pair], kw) + bias
    e_lat, e_ctx, inv = {}, {}, {}
    for u in units:
        m = jnp.maximum(jnp.max(s_lat[u], axis=-1, keepdims=True), jnp.max(s_ctx[u], axis=-1, keepdims=True))
        e_lat[u] = jnp.exp2(s_lat[u] - m)
        ec = jnp.exp2(s_ctx[u] - m)
        e_ctx[u] = ec.astype(BF16)
        inv[u] = 1.0 / (jnp.sum(e_lat[u], axis=-1, keepdims=True) + jnp.sum(ec, axis=-1, keepdims=True))
    o_ctx = {}
    for pair in range(2):
        vc = kv_ref[0, 0:C, 512 + pair * 128:512 + (pair + 1) * 128]
        oc = dot(jnp.concatenate([e_ctx[rr, pair] for rr in range(R)], axis=0), vc)
        for rr in range(R):
            o_ctx[rr, pair] = oc[rr * 128:(rr + 1) * 128]
    o = {}
    for rr, pair in units:
        vw = kv_ref[0, pl.ds(starts[rr], W), 512 + pair * 128:512 + (pair + 1) * 128]
        ou = (dot(e_lat[rr, pair].astype(BF16), vw) + o_ctx[rr, pair]) * inv[rr, pair]
        o[rr, pair] = jnp.where(lane < 64, ou[0:GRID_W], ou[GRID_W:])
    return jnp.concatenate([jnp.concatenate([o[rr, 0], o[rr, 1]], axis=1) for rr in range(R)], axis=0)


def _attn_kernel(naq_ref, nakv_ref, bias_ref, mq_ref, mk_ref, mv_ref, dq_ref, dk_ref, dv_ref, lam_ref, sub_ref,
                 o_ref, *, nsub, skip_ctx, nct, C, Ta, rows, lam_init):
    tq = naq_ref.shape[1] // nsub
    R = tq // GRID_W
    lp = lam_ref[...]
    lam = (jnp.exp(jnp.sum(lp[0:1] * lp[1:2], axis=-1, keepdims=True))
           - jnp.exp(jnp.sum(lp[2:3] * lp[3:4], axis=-1, keepdims=True)) + lam_init)

    for sub in range(nsub):
        g = pl.program_id(1) * nsub + sub
        rs = pl.ds(sub * tq, tq)
        naq = naq_ref.at[:, rs, :]

        def ctx_tile(rs=rs, naq=naq):
            o_ref[0, rs, :] = jnp.concatenate(
                [_na_ctx_body(naq, nakv_ref, C),
                 _mla_attn_body(mq_ref[0, rs, :], mk_ref, mv_ref, C),
                 _diff_finish(_diff_body(dq_ref[0, rs, :], dk_ref, dv_ref, C, lam), lam_init, sub_ref)],
                axis=1).astype(BF16)

        def lat_tile(rs=rs, naq=naq, g=g):
            o_ref[0, rs, :] = jnp.concatenate(
                [_na_lat_body(naq, nakv_ref, bias_ref, (g - nct) * R, C, rows, R),
                 _mla_attn_body(mq_ref[0, rs, :], mk_ref, mv_ref, Ta),
                 _diff_finish(_diff_body(dq_ref[0, rs, :], dk_ref, dv_ref, Ta, lam), lam_init, sub_ref)],
                axis=1).astype(BF16)

        if sub < nct:
            if not skip_ctx:
                pl.when(g < nct)(ctx_tile)
            pl.when(g >= nct)(lat_tile)
        else:
            lat_tile()


def _attention(na_qkv, bias, mq, mk, mv, dq, dk, dv, lam_p, sub, tq, nsub, skip_ctx, nct, C, rows, lam_init):
    B, Ta, _ = na_qkv.shape
    ts = tq * nsub
    qt = lambda a: pl.BlockSpec((1, ts, a.shape[-1]), lambda b, j: (b, j, 0))
    seq = lambda a: pl.BlockSpec((1, Ta, a.shape[-1]), lambda b, j: (b, 0, 0))
    kern = functools.partial(_attn_kernel, nsub=nsub, skip_ctx=skip_ctx, nct=nct, C=C, Ta=Ta, rows=rows,
                             lam_init=lam_init)
    return pl.pallas_call(
        kern,
        grid=(B, Ta // ts),
        in_specs=[qt(na_qkv), seq(na_qkv), _resident(bias), qt(mq), seq(mk), seq(mv), qt(dq), seq(dk), seq(dv),
                  _resident(lam_p), _resident(sub)],
        out_specs=pl.BlockSpec((1, ts, 768), lambda b, j: (b, j, 0)),
        out_shape=jax.ShapeDtypeStruct((B, Ta, 768), BF16),
        compiler_params=_cparams(("parallel", "arbitrary")),
        name="attention",
    )(na_qkv, na_qkv, bias, mq, mk, mv, dq, dk, dv, lam_p, sub)


def _rwkv_prep_kernel(z_ref, zp_ref, zn_ref, mu_ref, w0_ref, wup_ref, a0_ref, aup_ref, gup_ref,
                      kk_ref, ka_ref, rk_ref, rvk_ref, gb_ref, dir_ref, *, C, Ta):
    tm = z_ref.shape[1]
    z = z_ref[0]
    row = lax.broadcasted_iota(jnp.int32, (tm, 1), 0)
    pos = _token_rows(pl.program_id(0), tm)
    has_prev = jnp.where((pos == 0) | (pos == C), 0.0, 1.0)
    has_next = jnp.where((pos == C - 1) | (pos == Ta - 1), 0.0, 1.0)
    prev = jnp.where(row == 0, zp_ref[0, HALO - 1:HALO, :], pltpu.roll(z, 1, axis=0)) * has_prev
    nxt = jnp.where(row == tm - 1, zn_ref[0, 0:1, :], pltpu.roll(z, tm - 1, axis=0)) * has_next
    zs = z + mu_ref[0:1, :] * (prev - z) + mu_ref[1:2, :] * (nxt - z)
    r = zs[:, 0:256]
    k = zs[:, 256:512]
    v = zs[:, 512:768]
    low = zs[:, 768:896]
    ones = _head_ones()
    kk = k * kk_ref[...]
    kk = kk * lax.rsqrt(jnp.maximum(_head_sum(kk * kk, ones), 1e-24))
    g = _dot(_sigmoid(low), gup_ref[...])
    bonus = _head_sum(r * k * rk_ref[...], ones) * v
    rvk_ref[0, :, 0:256] = r
    rvk_ref[0, :, 256:512] = v
    rvk_ref[0, :, 512:768] = kk
    gb_ref[0, :, 0:256] = g
    gb_ref[0, :, 256:512] = bonus
    tl = jnp.tanh(low).astype(BF16)
    lb = low.astype(BF16)
    for d in range(2):
        wx = w0_ref[d:d + 1, :] + jnp.dot(tl, wup_ref[d], preferred_element_type=F32)
        a = _sigmoid(a0_ref[d:d + 1, :] + jnp.dot(lb, aup_ref[d], preferred_element_type=F32))
        dir_ref[d, 0, :, 0:256] = -math.exp(-0.5) * _sigmoid(wx)
        dir_ref[d, 0, :, 256:512] = k * (1.0 + (a - 1.0) * ka_ref[...])
        dir_ref[d, 0, :, 512:768] = kk * a


def _rwkv_prep(z, mu, w0, wup_p, a0, aup_p, gup_p, k_k, k_a, r_k, tm, C):
    B, Ta, _ = z.shape
    nt = Ta // tm
    hb = tm // HALO
    nh = Ta // HALO
    tok = lambda n: pl.BlockSpec((1, tm, n), lambda j, b: (b, j, 0))
    full = _resident
    kern = functools.partial(_rwkv_prep_kernel, C=C, Ta=Ta)
    return pl.pallas_call(
        kern,
        grid=(nt, B),
        in_specs=[tok(RWKV_COLS),
                  pl.BlockSpec((1, HALO, RWKV_COLS), lambda j, b: (b, jnp.maximum(j * hb - 1, 0), 0)),
                  pl.BlockSpec((1, HALO, RWKV_COLS), lambda j, b: (b, jnp.minimum((j + 1) * hb, nh - 1), 0)),
                  full(mu), full(w0), full(wup_p), full(a0), full(aup_p), full(gup_p),
                  full(k_k), full(k_a), full(r_k)],
        out_specs=[tok(768), tok(512),
                   pl.BlockSpec((2, 1, tm, 768), lambda j, b: (0, b, j, 0))],
        out_shape=[jax.ShapeDtypeStruct((B, Ta, 768), F32),
                   jax.ShapeDtypeStruct((B, Ta, 512), F32),
                   jax.ShapeDtypeStruct((2, B, Ta, 768), F32)],
        compiler_params=_cparams(("parallel", "parallel")),
        name="rwkv_prep",
    )(z, z, z, mu, w0, wup_p, a0, aup_p, gup_p, k_k, k_a, r_k)


def _rwkv_chunk_kernel(rvk_ref, dir_ref, g_ref, o_ref):
    d = pl.program_id(0)
    nchunk = rvk_ref.shape[1] // CHUNK
    bdm = _bd_mask()
    row = lax.broadcasted_iota(jnp.int32, (CHUNK, GROUP_W), 0)
    col = lax.broadcasted_iota(jnp.int32, (CHUNK, GROUP_W), 1) & (CHUNK - 1)
    tdiff = jnp.where(d == 0, row - col, col - row)
    incl = tdiff >= 0
    strict = tdiff > 0
    eye = tdiff == 0
    r2 = lax.broadcasted_iota(jnp.int32, (CHUNK, CHUNK), 0)
    c2 = lax.broadcasted_iota(jnp.int32, (CHUNK, CHUNK), 1)
    tri = jnp.where(jnp.where(d == 0, r2 - c2, c2 - r2) >= 0, 1.0, 0.0).astype(BF16)
    dot = functools.partial(jnp.dot, preferred_element_type=F32)

    def hmul(x, y):
        return dot(x.astype(BF16), _bd(y, bdm))

    def hmul_t(x, yt):
        ytb = yt.astype(BF16)
        return dot(x.astype(BF16), jnp.where(bdm, jnp.concatenate([ytb, ytb, ytb, ytb], axis=1),
                                             jnp.zeros((), BF16)))

    def head_t(x):
        xt = x.T
        return jnp.concatenate([xt[0:64], xt[64:128], xt[128:192], xt[192:256]], axis=1)

    cs = range(nchunk)
    sls = [slice(c * CHUNK, (c + 1) * CHUNK) for c in cs]
    r = [rvk_ref[0, sl, 0:256] for sl in sls]
    v = [rvk_ref[0, sl, 256:512] for sl in sls]
    kk = [rvk_ref[0, sl, 512:768] for sl in sls]
    lw = [dir_ref[0, 0, sl, 0:256] for sl in sls]
    kd = [dir_ref[0, 0, sl, 256:512] for sl in sls]
    b = [dir_ref[0, 0, sl, 512:768] for sl in sls]
    cum, tot = [], []
    for c in cs:
        l1, l2, l3 = _split3(lw[c])
        cum.append(dot(tri, l1) + (dot(tri, l2) + dot(tri, l3)))
        tot.append(jnp.sum(lw[c], axis=0, keepdims=True))
    rt = [r[c] * jnp.exp(cum[c]) for c in cs]
    at = [-kk[c] * jnp.exp(cum[c] - lw[c]) for c in cs]
    einv = [jnp.exp(-cum[c]) for c in cs]
    eend = [jnp.exp(tot[c] - cum[c]) for c in cs]
    ar = [jnp.concatenate([at[c], rt[c]], axis=0) for c in cs]
    xb = [hmul_t(ar[c], (b[c] * einv[c]).T) for c in cs]
    xk = [hmul_t(ar[c], (kd[c] * einv[c]).T) for c in cs]
    n = [jnp.where(strict, xb[c][0:CHUNK], 0.0) for c in cs]
    lrb = [jnp.where(incl, xb[c][CHUNK:], 0.0) for c in cs]
    bht = [head_t(b[c] * eend[c]) for c in cs]
    al = [jnp.concatenate([jnp.where(strict, xk[c][0:CHUNK], 0.0),
                           jnp.where(incl, xk[c][CHUNK:], 0.0),
                           head_t(kd[c] * eend[c])], axis=0) for c in cs]
    akv_lrkv = [hmul(al[c], v[c]) for c in cs]
    p = [jnp.where(eye, 1.0, 0.0) + n[c] for c in cs]
    npow = [hmul(n[c], n[c]) for c in cs]
    for _ in range(4):
        sq = [hmul(jnp.concatenate([npow[c], p[c]], axis=0), npow[c]) for c in cs]
        npow = [sq[c][0:CHUNK] for c in cs]
        p = [p[c] + sq[c][CHUNK:] for c in cs]
    t = [p[c] + hmul(p[c], npow[c]) for c in cs]
    w = [hmul(t[c], at[c]) for c in cs]
    u0 = [hmul(t[c], akv_lrkv[c][0:CHUNK]) for c in cs]
    lb = [jnp.concatenate([lrb[c], bht[c]], axis=0) for c in cs]
    xw = [hmul(lb[c], w[c]) for c in cs]
    xu = [hmul(lb[c], u0[c]) for c in cs]
    gs, outs = [], []
    for c in cs:
        gs.append(xw[c][CHUNK:] + jnp.where(eye, jnp.exp(tot[c]), 0.0))
        outs.append(jnp.concatenate([rt[c] + xw[c][0:CHUNK],
                                     xu[c][0:CHUNK] + akv_lrkv[c][CHUNK:2 * CHUNK],
                                     xu[c][CHUNK:] + akv_lrkv[c][2 * CHUNK:]], axis=1))
    g_ref[0, 0] = jnp.concatenate(gs, axis=0)
    o_ref[0, 0] = jnp.concatenate(outs, axis=0).astype(BF16)


def _rwkv_chunks(rvk, dirp, tm):
    B, Ta, _ = rvk.shape
    nt = Ta // tm
    blk = lambda n: pl.BlockSpec((1, 1, tm, n), lambda d, b, j: (d, b, j, 0))
    return pl.pallas_call(
        _rwkv_chunk_kernel,
        grid=(2, B, nt),
        in_specs=[pl.BlockSpec((1, tm, 768), lambda d, b, j: (b, j, 0)), blk(768)],
        out_specs=[blk(256), blk(768)],
        out_shape=[jax.ShapeDtypeStruct((2, B, Ta, 256), F32), jax.ShapeDtypeStruct((2, B, Ta, 768), BF16)],
        compiler_params=_cparams(("parallel", "parallel", "parallel")),
        name="rwkv_chunks",
    )(rvk, dirp)


def _rwkv_scan_kernel(gf_ref, cf_ref, gr_ref, cr_ref, yf_ref, yb_ref, s_ref):
    i = pl.program_id(1)
    gb = gf_ref.shape[1]
    bdm = _bd_mask()

    @pl.when(i == 0)
    def _():
        s_ref[...] = jnp.zeros(s_ref.shape, F32)

    dot = functools.partial(jnp.dot, preferred_element_type=F32)
    zero = jnp.zeros((), BF16)
    chains = [(d, bb) for d in range(2) for bb in range(gb)]
    g_refs = (gf_ref, gr_ref)
    c_refs = (cf_ref, cr_ref)
    parts = {}
    for d, bb in chains:
        sh, sl = _split2(s_ref[d, bb])
        gh, gl = _split2(g_refs[d][0, bb])
        parts[d, bb] = (jnp.concatenate([c_refs[d][0, bb, :, 0:256], gh], axis=0), gl,
                        jnp.where(bdm, jnp.concatenate([sh] * 4, axis=0), zero),
                        jnp.where(bdm, jnp.concatenate([sl] * 4, axis=0), zero))
    res = {}
    for ch in chains:
        lh, gl, sbh, sbl = parts[ch]
        r = dot(lh, sbh) + dot(lh, sbl)
        res[ch] = (r[0:CHUNK], r[CHUNK:] + dot(gl, sbh))
    for d, y_ref in enumerate((yf_ref, yb_ref)):
        for bb in range(gb):
            y_ref[bb] = res[d, bb][0] + c_refs[d][0, bb, :, 256:512].astype(F32)
            s_ref[d, bb] = res[d, bb][1] + c_refs[d][0, bb, :, 512:768].astype(F32)


def _rwkv_scan(chg, chb, ncc, gb):
    _, B, Ta, _ = chg.shape
    nc = Ta // CHUNK

    def rev_chunk(i):
        return jnp.where(i < ncc, ncc - 1 - i, nc - 1 - (i - ncc))

    fwd = lambda n: pl.BlockSpec((1, gb, CHUNK, n), lambda b, i: (0, b, i, 0))
    rev = lambda n: pl.BlockSpec((1, gb, CHUNK, n), lambda b, i: (1, b, rev_chunk(i), 0))
    return pl.pallas_call(
        _rwkv_scan_kernel,
        grid=(B // gb, nc),
        in_specs=[fwd(256), fwd(768), rev(256), rev(768)],
        out_specs=[pl.BlockSpec((gb, CHUNK, 256), lambda b, i: (b, i, 0)),
                   pl.BlockSpec((gb, CHUNK, 256), lambda b, i: (b, rev_chunk(i), 0))],
        out_shape=[jax.ShapeDtypeStruct((B, Ta, 256), F32),
                   jax.ShapeDtypeStruct((B, Ta, 256), F32)],
        scratch_shapes=[pltpu.VMEM((2, gb, CHUNK, GROUP_W), F32)],
        compiler_params=_cparams(("parallel", "arbitrary")),
        name="rwkv_scan",
    )(chg, chb, chg, chb)


def _out_kernel(x_ref, mod_ref, att_ref, yf_ref, yb_ref, gb_ref, lnw_ref, lnb_ref, w_ref, o_ref, *, j0, C):
    x = x_ref[0]
    tm, D = x.shape
    is_ctx = _token_rows(pl.program_id(0) + j0, tm) < C
    dot = functools.partial(jnp.dot, preferred_element_type=F32)
    ones = _head_ones()
    y = yf_ref[0] + yb_ref[0]
    mean = _head_sum(y, ones) * (1.0 / HEAD_DIM)
    yc = y - mean
    var = _head_sum(yc * yc, ones) * (1.0 / HEAD_DIM)
    yn = yc * lax.rsqrt(var + RWKV_LN_EPS) * lnw_ref[...] + lnb_ref[...]
    rw = ((yn + gb_ref[0, :, 256:512]) * gb_ref[0, :, 0:256]).astype(BF16)
    mix = (dot(att_ref[0, :, 0:512], w_ref[0:512, :]) + dot(rw, w_ref[512:768, :])) + \
        dot(att_ref[0, :, 512:768], w_ref[768:1024, :])
    o_ref[0] = x + _mod_rows(mod_ref, is_ctx, 2, D) * mix


def _out_proj(X, mod2, att, yf, yb, gbn, ln_w, ln_b, w, tm, j0, C):
    B, Ta, D = X.shape
    nt = Ta // tm - j0
    tok = lambda n: pl.BlockSpec((1, tm, n), lambda j, b: (b, j + j0, 0))
    full = _resident
    return pl.pallas_call(
        functools.partial(_out_kernel, j0=j0, C=C),
        grid=(nt, B),
        in_specs=[tok(D), _mod_spec(D), tok(768), tok(256), tok(256), tok(512), full(ln_w), full(ln_b), full(w)],
        out_specs=tok(D),
        out_shape=jax.ShapeDtypeStruct((B, Ta, D), F32),
        compiler_params=_cparams(("parallel", "parallel")),
        name="out_proj",
    )(X, mod2, att, yf, yb, gbn, ln_w, ln_b, w)


def _mlp_kernel(x_ref, xp_ref, xn_ref, mod_ref, g_ref, wa_ref, wb_ref, cw_ref, cb_ref, wd_ref, gf_ref, o_ref,
                *, j0, C, Ta, fc, final):
    j = pl.program_id(0) + j0
    tm = x_ref.shape[1]
    D = x_ref.shape[2]
    dff = wa_ref.shape[1]
    x = x_ref[0]
    ne = tm + 2 * MLP_HALO
    xe = jnp.concatenate([xp_ref[0], x, xn_ref[0]], axis=0)
    pos = _token_rows(j, tm)
    ctx_e = (j * tm - MLP_HALO + lax.broadcasted_iota(jnp.int32, (ne, 1), 0)) < C
    h = (_rms(xe, g_ref[...], NORM_EPS) * (1.0 + _mod_rows(mod_ref, ctx_e, 4, D))
         + _mod_rows(mod_ref, ctx_e, 3, D)).astype(BF16)
    pmask = jnp.where((pos == 0) | (pos == C), 0.0, 1.0)
    nmask = jnp.where((pos == C - 1) | (pos == Ta - 1), 0.0, 1.0)
    dot = functools.partial(jnp.dot, preferred_element_type=F32)
    acc = jnp.zeros((tm, D), F32)
    for c in range(dff // fc):
        cs = slice(c * fc, (c + 1) * fc)
        a = dot(h, wa_ref[:, cs])
        mid = slice(MLP_HALO, MLP_HALO + tm)
        b = dot(h[mid], wb_ref[:, cs])
        cv = (cw_ref[0:1, cs] * (pltpu.roll(a, 1, axis=0)[mid] * pmask) + cw_ref[1:2, cs] * a[mid]
              + cw_ref[2:3, cs] * (pltpu.roll(a, ne - 1, axis=0)[mid] * nmask) + cb_ref[:, cs])
        u = cv * _sigmoid(cv) * b
        acc = acc + dot(u.astype(BF16), wd_ref[cs, :])
    y = x + _mod_rows(mod_ref, pos < C, 5, D) * acc
    o_ref[0] = _rms(y, gf_ref[...], NORM_EPS) if final else y


def _mlp(X, mod2, g, wa, wb, cw, cb, wd, gf, tm, j0, C, final):
    B, Ta, D = X.shape
    ntot = Ta // tm
    nt = ntot - j0
    hb = tm // MLP_HALO
    nh = Ta // MLP_HALO
    tok = lambda n: pl.BlockSpec((1, tm, n), lambda j, b: (b, j + j0, 0))
    full = _resident
    kern = functools.partial(_mlp_kernel, j0=j0, C=C, Ta=Ta, fc=wa.shape[1], final=final)
    if final:
        assert j0 * tm == C
        out_spec = pl.BlockSpec((1, tm, D), lambda j, b: (b, j, 0))
        out_shape = jax.ShapeDtypeStruct((B, Ta - C, D), F32)
    else:
        out_spec, out_shape = tok(D), jax.ShapeDtypeStruct((B, Ta, D), F32)
    return pl.pallas_call(
        kern,
        grid=(nt, B),
        in_specs=[tok(D),
                  pl.BlockSpec((1, MLP_HALO, D), lambda j, b: (b, jnp.maximum((j + j0) * hb - 1, j0 * hb), 0)),
                  pl.BlockSpec((1, MLP_HALO, D), lambda j, b: (b, jnp.minimum((j + j0 + 1) * hb, nh - 1), 0)),
                  _mod_spec(D),
                  full(g), full(wa), full(wb), full(cw), full(cb), full(wd), full(gf)],
        out_specs=out_spec,
        out_shape=out_shape,
        compiler_params=_cparams(("parallel", "parallel")),
        name="conv_glu",
    )(X, X, X, mod2, g, wa, wb, cw, cb, wd, gf)


def _rot_cols(w):
    s = w.shape
    x = w.reshape(s[:-1] + (s[-1] // ROPE_DIM, 4, ROPE_DIM // 4))
    r1, r2, c1, c2 = x[..., 0, :], x[..., 1, :], x[..., 2, :], x[..., 3, :]
    return jnp.stack([-r2, r1, -c2, c1], axis=-2).reshape(s)


def _rope_tables(T, C):
    t = np.arange(T)
    rowp = (t // GRID_W).astype(np.float32)
    colp = (t % GRID_W).astype(np.float32)
    half = ROPE_DIM // 2
    freqs = jnp.asarray(ROPE_THETA, F32) ** (-jnp.arange(0, half, 2, dtype=F32) / half)
    ar = jnp.asarray(rowp)[:, None] * freqs[None, :]
    ac = jnp.asarray(colp)[:, None] * freqs[None, :]
    ang = jnp.concatenate([ar, ar, ac, ac], axis=-1)
    cos = jnp.concatenate([jnp.ones((C, ROPE_DIM), F32), jnp.cos(ang)], axis=0)
    sin = jnp.concatenate([jnp.zeros((C, ROPE_DIM), F32), jnp.sin(ang)], axis=0)
    Ta = T + C
    cos256 = jnp.tile(cos, (1, 8))
    sin256 = jnp.tile(sin, (1, 8))
    one = jnp.ones((Ta, HEAD_DIM), F32)
    zero = jnp.zeros((Ta, HEAD_DIM), F32)
    cos512 = jnp.tile(jnp.concatenate([one, cos, one[:, :32]], axis=1), (1, 4))
    sin512 = jnp.tile(jnp.concatenate([zero, sin, zero[:, :32]], axis=1), (1, 4))
    return cos256, sin256, cos512, sin512


def _na_bias_tables(rpb):
    cpos = np.arange(GRID_W)
    cstart = np.clip(cpos - NA_KW // 2, 0, GRID_W - NA_KW)
    col_mask = (cpos[None, :] >= cstart[:, None]) & (cpos[None, :] < cstart[:, None] + NA_KW)
    col_idx = np.clip(cpos[None, :] - cpos[:, None] + NA_KW - 1, 0, 2 * NA_KW - 2)
    tabs = []
    for delta in range(NA_KH):
        row_off = np.arange(NA_KH) - delta + NA_KH - 1
        bias = rpb[:, :, row_off][:, :, :, col_idx]
        bias = jnp.where(jnp.asarray(col_mask)[None, None, None], bias * LOG2E, NEG_INF)
        tabs.append(bias.transpose(0, 1, 3, 2, 4).reshape(rpb.shape[0], N_HEADS, GRID_W, NA_KH * GRID_W))
    return jnp.stack(tabs, axis=1)


def kernel(x, c, ctx, c_ctx, norm1_g, norm2_g, ada_w, ada_b, w_in, w_out, na_rpb, mla_q_norm, mla_kv_norm,
           mla_w_uq, mla_w_ukv, rwkv_mu, rwkv_w0, rwkv_w_up, rwkv_a0, rwkv_a_up, rwkv_g_up, rwkv_k_k,
           rwkv_k_a, rwkv_r_k, rwkv_ln_w, rwkv_ln_b, diff_lambda, diff_subln, mlp_w_up, mlp_conv_w,
           mlp_conv_b, mlp_w_down, final_norm_g):
    B, T, D = x.shape
    C = ctx.shape[1]
    L = ada_w.shape[0]
    Ta = T + C
    tm = min(TOKEN_TILE, C)
    rows = T // GRID_W
    assert C % tm == 0 and T % tm == 0 and tm % CHUNK == 0 and rows >= NA_KH and D == 1024
    nct = C // tm
    t_wide = WIDE_TILE if Ta % WIDE_TILE == 0 else tm
    t_mlp = MLP_TILE if Ta % MLP_TILE == 0 else tm
    att_sub = next(n for n in (3, 2, 1) if (Ta // tm) % n == 0)
    dff = mlp_w_down.shape[1]

    wi = w_in
    na_w = wi[:, :, 0:768]
    cq_w, ckv_w, kr_w = wi[:, :, 768:1024], wi[:, :, 1024:1152], wi[:, :, 1152:1184]
    rw_w = wi[:, :, 1184:2080]
    dq_w, dk_w, dv_w = wi[:, :, 2080:2336], wi[:, :, 2336:2592], wi[:, :, 2592:2848]
    col_scale = np.ones((W_TOT,), np.float32)
    col_scale[0:256] = HEAD_DIM ** -0.5 * LOG2E
    w_all = (jnp.concatenate([na_w, dq_w, _rot_cols(dq_w), dk_w, _rot_cols(dk_w), dv_w, rw_w,
                              cq_w, ckv_w, kr_w, _rot_cols(kr_w), jnp.zeros((L, D, 64), F32)], axis=-1)
             * jnp.asarray(col_scale)).astype(BF16)
    w_out_b = w_out.astype(BF16)
    wa_b = mlp_w_up[:, :, :dff].astype(BF16)
    wb_b = mlp_w_up[:, :, dff:].astype(BF16)
    wd_b = mlp_w_down.astype(BF16)

    uq = mla_w_uq.reshape(L, MLA_Q_RANK, N_HEADS, HEAD_DIM + ROPE_DIM)
    pad32 = jnp.zeros((L, MLA_Q_RANK, N_HEADS, 32), F32)
    wqm = jnp.concatenate([uq, pad32], axis=-1).reshape(L, MLA_Q_RANK, 512).astype(BF16)
    wqr = jnp.concatenate([jnp.zeros_like(uq[..., :HEAD_DIM]), _rot_cols(uq[..., HEAD_DIM:]), pad32],
                          axis=-1).reshape(L, MLA_Q_RANK, 512).astype(BF16)
    ukv = mla_w_ukv.reshape(L, MLA_KV_RANK, N_HEADS, 2 * HEAD_DIM)
    wk = jnp.concatenate([ukv[..., :HEAD_DIM], jnp.zeros_like(ukv[..., HEAD_DIM:])],
                         axis=-1).reshape(L, MLA_KV_RANK, 512).astype(BF16)
    wv = ukv[..., HEAD_DIM:].reshape(L, MLA_KV_RANK, 256).astype(BF16)
    pm = np.zeros((128, 1024), np.float32)
    for h in range(N_HEADS):
        for i in range(ROPE_DIM):
            pm[i, h * 128 + HEAD_DIM + i] = 1.0
            pm[ROPE_DIM + i, 512 + h * 128 + HEAD_DIM + i] = 1.0
    pmat = jnp.asarray(pm, BF16)

    zr = lambda n: jnp.zeros((L, 2, n, GROUP_W), F32)
    wup_p = jnp.concatenate([rwkv_w_up, zr(96)], axis=2).astype(BF16)
    aup_p = jnp.concatenate([zr(32), rwkv_a_up, zr(64)], axis=2).astype(BF16)
    gup_p = jnp.concatenate([jnp.zeros((L, 64, GROUP_W), F32), rwkv_g_up], axis=1).astype(BF16)

    cos256, sin256, cos512, sin512 = _rope_tables(T, C)
    na_bias = _na_bias_tables(na_rpb)
    sub256 = jnp.tile(diff_subln, (1, N_HEADS))

    R = ((B + 1 + 7) // 8) * 8
    cc = jnp.concatenate([c, c_ctx[None], jnp.zeros((R - B - 1, D), F32)], axis=0)
    mod = _modulation(cc, ada_w, ada_b)
    mod2 = jnp.stack([jnp.broadcast_to(mod[:, B:B + 1], (L, B, 6 * D)), mod[:, :B]], axis=2)
    mod2 = mod2.reshape(L, 2 * B, 1, 6 * D)

    X = jnp.concatenate([ctx, x], axis=1)
    gb = next(n for n in (8, 4, 2, 1) if B % n == 0)
    for l in range(L):
        need_ctx = l < L - 1
        j0 = 0 if need_ctx else nct
        na_qkv, dq, dk, dv, z_rw, mq, mk, mv = _in_proj(
            X, mod2[l], norm1_g[l][None], w_all[l], cos256, sin256, cos512, sin512,
            mla_q_norm[l][None], mla_kv_norm[l][None], wqm[l], wqr[l], wk[l], wv[l], pmat, t_wide, C)
        lam_init = 0.8 - 0.6 * math.exp(-0.3 * l)
        att = _attention(na_qkv, na_bias[l], mq, mk, mv, dq, dk, dv, diff_lambda[l], sub256[l][None],
                         tm, att_sub, not need_ctx, nct, C, rows, lam_init)
        rvk, gbn, dirp = _rwkv_prep(z_rw, rwkv_mu[l], rwkv_w0[l], wup_p[l], rwkv_a0[l], aup_p[l], gup_p[l],
                                    rwkv_k_k[l][None], rwkv_k_a[l][None], rwkv_r_k[l].reshape(1, GROUP_W),
                                    t_wide, C)
        chg, chb = _rwkv_chunks(rvk, dirp, t_wide)
        yf, yb = _rwkv_scan(chg, chb, C // CHUNK, gb)
        X = _out_proj(X, mod2[l], att, yf, yb, gbn, rwkv_ln_w[l][None], rwkv_ln_b[l][None],
                      w_out_b[l], t_wide if need_ctx else tm, j0, C)
        X = _mlp(X, mod2[l], norm2_g[l][None], wa_b[l], wb_b[l], mlp_conv_w[l], mlp_conv_b[l][None], wd_b[l],
                 final_norm_g[None], t_mlp if need_ctx else tm, j0, C, final=not need_ctx)
    return X
```

```python
import functools
import math

import jax
import jax.numpy as jnp
import numpy as np
from jax import lax
from jax.experimental import pallas as pl
from jax.experimental.pallas import tpu as pltpu

F32 = jnp.float32
BF16 = jnp.bfloat16

GRID_W = 64
GROUP_W = 256
HEAD_DIM = 64
N_HEADS = 4
ROPE_DIM = 32
ROPE_THETA = 10000.0
NORM_EPS = 1e-6
NEG_INF = -1e30
LOG2E = math.log2(math.e)
NA_KH = 8
NA_KW = 16
MLA_Q_RANK = 256
MLA_KV_RANK = 128
RWKV_COLS = 896
RWKV_LN_EPS = 64e-5
DIFF_QK = 32
DIFF_LN_EPS = 1e-5
IN_SPLITS = (768, 1184, 2080, 2848)

W_NA = 0
W_DIFF = 768
W_RWKV = 2048
W_MLA = 2944
W_TOT = 3456

TOKEN_TILE = 256
CHUNK = 64
WIDE_TILE = 768
MLP_TILE = 384
HALO = 8
MLP_HALO = HALO
VMEM_LIMIT = 56 * 1024 * 1024


def _cparams(sem):
    return pltpu.CompilerParams(dimension_semantics=sem, vmem_limit_bytes=VMEM_LIMIT)


def _resident(a):
    return pl.BlockSpec(a.shape, lambda *_: (0,) * a.ndim, pipeline_mode=pl.Buffered(1))


def _dot(a, b):
    return jnp.dot(a.astype(BF16), b.astype(BF16), preferred_element_type=F32)


def _dot_nt(a, b):
    return lax.dot_general(a.astype(BF16), b.astype(BF16), (((1,), (1,)), ((), ())),
                           preferred_element_type=F32)


def _split2(x):
    hi = x.astype(BF16)
    lo = (x - hi.astype(F32)).astype(BF16)
    return hi, lo


def _split3(x):
    h1 = x.astype(BF16)
    r1 = x - h1.astype(F32)
    h2 = r1.astype(BF16)
    h3 = (r1 - h2.astype(F32)).astype(BF16)
    return h1, h2, h3


def _dot3(a, b):
    ah, al = _split2(a)
    bh, bl = _split2(b)
    d = functools.partial(jnp.dot, preferred_element_type=F32)
    return d(ah, bh) + (d(ah, bl) + d(al, bh))


def _sigmoid(x):
    return 1.0 / (1.0 + jnp.exp(-x))


def _rms(x, g, eps):
    return x * lax.rsqrt(jnp.mean(x * x, axis=-1, keepdims=True) + eps) * g


def _token_rows(j, tm):
    return j * tm + lax.broadcasted_iota(jnp.int32, (tm, 1), 0)


def _mod_rows(mod_ref, is_ctx, k, D):
    return jnp.where(is_ctx, mod_ref[0][:, k * D:(k + 1) * D], mod_ref[1][:, k * D:(k + 1) * D])


def _mod_spec(D):
    return pl.BlockSpec((2, 1, 6 * D), lambda j, b: (b, 0, 0))


def _head_ones(n=GROUP_W):
    r = lax.broadcasted_iota(jnp.int32, (n, n), 0) >> 6
    c = lax.broadcasted_iota(jnp.int32, (n, n), 1) >> 6
    return jnp.where(r == c, 1.0, 0.0).astype(BF16)


def _head_sum(x, ones):
    hi, lo = _split2(x)
    d = functools.partial(jnp.dot, preferred_element_type=F32)
    return d(hi, ones) + d(lo, ones)


def _bd_mask():
    r = lax.broadcasted_iota(jnp.int32, (GROUP_W, GROUP_W), 0) >> 6
    c = lax.broadcasted_iota(jnp.int32, (GROUP_W, GROUP_W), 1) >> 6
    return r == c


def _bd(x, mask):
    xb = x.astype(BF16)
    return jnp.where(mask, jnp.concatenate([xb, xb, xb, xb], axis=0), jnp.zeros((), BF16))


def _mod_kernel(s_ref, w_ref, b_ref, o_ref):
    s = s_ref[...]
    s = s * _sigmoid(s)
    o_ref[0] = _dot3(s, w_ref[0]) + b_ref[0]


def _modulation(cc, ada_w, ada_b):
    L, D, N = ada_w.shape
    R = cc.shape[0]
    tn = 1536
    return pl.pallas_call(
        _mod_kernel,
        grid=(L, N // tn),
        in_specs=[pl.BlockSpec((R, D), lambda l, n: (0, 0)),
                  pl.BlockSpec((1, D, tn), lambda l, n: (l, 0, n)),
                  pl.BlockSpec((1, 1, tn), lambda l, n: (l, 0, n))],
        out_specs=pl.BlockSpec((1, R, tn), lambda l, n: (l, 0, n)),
        out_shape=jax.ShapeDtypeStruct((L, R, N), F32),
        compiler_params=_cparams(("parallel", "parallel")),
        name="adaln_mod",
    )(cc, ada_w, ada_b.reshape(L, 1, N))


def _in_kernel(x_ref, mod_ref, g_ref, w_ref, cos_ref, sin_ref, cos5_ref, sin5_ref,
               qn_ref, kvn_ref, wqm_ref, wqr_ref, wk_ref, wv_ref, p_ref,
               na_ref, dq_ref, dk_ref, dv_ref, rw_ref, mq_ref, mk_ref, mv_ref, *, C):
    x = x_ref[0]
    tm, D = x.shape
    is_ctx = _token_rows(pl.program_id(0), tm) < C
    h = _rms(x, g_ref[...], NORM_EPS) * (1.0 + _mod_rows(mod_ref, is_ctx, 1, D)) + _mod_rows(mod_ref, is_ctx, 0, D)
    hb = h.astype(BF16)
    dot = functools.partial(jnp.dot, preferred_element_type=F32)
    zall = dot(hb, w_ref[...])
    na_ref[0, :, 0:256] = (zall[:, 0:256] * (HEAD_DIM ** -0.5 * LOG2E)).astype(BF16)
    na_ref[0, :, 256:768] = zall[:, 256:W_DIFF].astype(BF16)
    d = zall[:, W_DIFF:W_RWKV]
    cos = cos_ref[...]
    sin = sin_ref[...]
    scale = DIFF_QK ** -0.5 * LOG2E
    dq_ref[0] = ((d[:, 0:256] * cos + d[:, 256:512] * sin) * scale).astype(BF16)
    dk_ref[0] = (d[:, 512:768] * cos + d[:, 768:1024] * sin).astype(BF16)
    dv_ref[0] = _with_ones(d[:, 1024:1280])
    rw_ref[0] = zall[:, W_RWKV:W_MLA]
    z = zall[:, W_MLA:W_TOT]
    cos5 = cos5_ref[...]
    sin5 = sin5_ref[...]
    nq = _rms(z[:, 0:256], qn_ref[...], NORM_EPS).astype(BF16)
    mscale = (HEAD_DIM + ROPE_DIM) ** -0.5 * LOG2E
    mq_ref[0] = ((dot(nq, wqm_ref[...]) * cos5 + dot(nq, wqr_ref[...]) * sin5) * mscale).astype(BF16)
    nkv = _rms(z[:, 256:384], kvn_ref[...], NORM_EPS).astype(BF16)
    kr = dot(z[:, 384:512].astype(BF16), p_ref[...])
    mk_ref[0] = (dot(nkv, wk_ref[...]) + kr[:, 0:512] * cos5 + kr[:, 512:1024] * sin5).astype(BF16)
    mv_ref[0] = _with_ones(dot(nkv, wv_ref[...]))


def _in_proj(X, mod2, g, w, cos256, sin256, cos512, sin512, qn, kvn, wqm, wqr, wk, wv, pmat, tm, C):
    B, Ta, D = X.shape
    nt = Ta // tm
    tok = lambda n: pl.BlockSpec((1, tm, n), lambda j, b: (b, j, 0))
    tab = lambda n: pl.BlockSpec((tm, n), lambda j, b: (j, 0))
    full = _resident
    return pl.pallas_call(
        functools.partial(_in_kernel, C=C),
        grid=(nt, B),
        in_specs=[tok(D), _mod_spec(D),
                  full(g), full(w), tab(256), tab(256), tab(512), tab(512),
                  full(qn), full(kvn), full(wqm), full(wqr), full(wk), full(wv), full(pmat)],
        out_specs=[tok(768), tok(256), tok(256), tok(512), tok(RWKV_COLS), tok(512), tok(512), tok(512)],
        out_shape=[jax.ShapeDtypeStruct((B, Ta, 768), BF16),
                   jax.ShapeDtypeStruct((B, Ta, 256), BF16),
                   jax.ShapeDtypeStruct((B, Ta, 256), BF16),
                   jax.ShapeDtypeStruct((B, Ta, 512), BF16),
                   jax.ShapeDtypeStruct((B, Ta, RWKV_COLS), F32),
                   jax.ShapeDtypeStruct((B, Ta, 512), BF16),
                   jax.ShapeDtypeStruct((B, Ta, 512), BF16),
                   jax.ShapeDtypeStruct((B, Ta, 512), BF16)],
        compiler_params=_cparams(("parallel", "parallel")),
        name="in_proj",
    )(X, mod2, g, w, cos256, sin256, cos512, sin512, qn, kvn, wqm, wqr, wk, wv, pmat)


def _softmax_parts(s):
    m = jnp.max(s, axis=-1, keepdims=True)
    e = jnp.exp2(s - m)
    return e, jnp.sum(e, axis=-1, keepdims=True)


def _with_ones(v):
    one = jnp.ones((v.shape[0], 128), F32)
    return jnp.concatenate([v[:, 0:128], one, v[:, 128:256], one], axis=1).astype(BF16)


def _attend(s, vp1):
    sb = s.astype(BF16)
    e = jnp.exp2(sb - jnp.max(sb, axis=-1, keepdims=True))
    r = jnp.dot(e, vp1, preferred_element_type=F32)
    return r[:, 0:128] / r[:, 128:256]


def _mla_attn_body(q, k_ref, v_ref, nk):
    tq = q.shape[0]
    lane = lax.broadcasted_iota(jnp.int32, (tq, 128), 1)
    outs = []
    for pair in range(2):
        vp1 = v_ref[0, 0:nk, pair * 256:(pair + 1) * 256]
        o2 = [_attend(_dot_nt(q[:, h * 128:(h + 1) * 128], k_ref[0, 0:nk, h * 128:(h + 1) * 128]), vp1)
              for h in (2 * pair, 2 * pair + 1)]
        outs.append(jnp.where(lane < 64, o2[0], o2[1]))
    return jnp.concatenate(outs, axis=1)


def _diff_body(q, k_ref, v_ref, nk, lam):
    tq = q.shape[0]
    k = k_ref[0, 0:nk, :]
    sub = lax.broadcasted_iota(jnp.int32, (tq, GROUP_W), 1) >> 5
    lane = lax.broadcasted_iota(jnp.int32, (tq, 128), 1)
    zero = jnp.zeros((), BF16)
    outs = []
    for pair in range(2):
        vp1 = v_ref[0, 0:nk, pair * 256:(pair + 1) * 256]
        o2 = []
        for hh in range(2):
            h = 2 * pair + hh
            o2.append(_attend(_dot_nt(jnp.where(sub == 2 * h, q, zero), k), vp1)
                      - lam * _attend(_dot_nt(jnp.where(sub == 2 * h + 1, q, zero), k), vp1))
        outs.append(jnp.where(lane < 64, o2[0], o2[1]))
    return jnp.concatenate(outs, axis=1)


def _diff_finish(o, lam_init, sub_ref):
    ms = _head_sum(o * o, _head_ones()) * (1.0 / HEAD_DIM)
    return (o * lax.rsqrt(ms + DIFF_LN_EPS) * sub_ref[...]) * (1.0 - lam_init)


def _na_ctx_body(qt_ref, kv_ref, C):
    tq = qt_ref.shape[1]
    zero = jnp.zeros((), BF16)
    lane = lax.broadcasted_iota(jnp.int32, (tq, 128), 1)
    outs = []
    for pair in range(2):
        q2 = qt_ref[0, :, pair * 128:(pair + 1) * 128]
        kc = kv_ref[0, 0:C, 256 + pair * 128:256 + (pair + 1) * 128]
        vc = kv_ref[0, 0:C, 512 + pair * 128:512 + (pair + 1) * 128]
        o2 = []
        for hh in range(2):
            qm = jnp.where((lane < 64) if hh == 0 else (lane >= 64), q2, zero)
            e, l = _softmax_parts(_dot_nt(qm, kc))
            o2.append(jnp.dot(e.astype(BF16), vc, preferred_element_type=F32) * (1.0 / l))
        outs.append(jnp.where(lane < 64, o2[0], o2[1]))
    return jnp.concatenate(outs, axis=1)


def _na_lat_body(qt_ref, kv_ref, bias_ref, first_row, C, rows, R):
    zero = jnp.zeros((), BF16)
    lane = lax.broadcasted_iota(jnp.int32, (GRID_W, 128), 1)
    W = NA_KH * GRID_W
    dot = functools.partial(jnp.dot, preferred_element_type=F32)
    starts, deltas = [], []
    for rr in range(R):
        r = first_row + rr
        rs = jnp.clip(r - NA_KH // 2, 0, rows - NA_KH)
        deltas.append(r - rs)
        starts.append(pl.multiple_of(C + rs * GRID_W, GRID_W))
    units = [(rr, pair) for pair in range(2) for rr in range(R)]
    qs = {}
    for rr, pair in units:
        q2 = qt_ref[0, rr * GRID_W:(rr + 1) * GRID_W, pair * 128:(pair + 1) * 128]
        qs[rr, pair] = jnp.concatenate([jnp.where(lane < 64, q2, zero), jnp.where(lane >= 64, q2, zero)], axis=0)
    s_ctx = {}
    for pair in range(2):
        kc = kv_ref[0, 0:C, 256 + pair * 128:256 + (pair + 1) * 128]
        sc = _dot_nt(jnp.concatenate([qs[rr, pair] for rr in range(R)], axis=0), kc)
        for rr in range(R):
            s_ctx[rr, pair] = sc[rr * 128:(rr + 1) * 128]
    s_lat = {}
    for rr, pair in units:
        kw = kv_ref[0, pl.ds(starts[rr], W), 256 + pair * 128:256 + (pair + 1) * 128]
        bias = jnp.concatenate([bias_ref[deltas[rr], 2 * pair], bias_ref[deltas[rr], 2 * pair + 1]], axis=0)
        s_lat[rr, pair] = _dot_nt(qs[rr, pair], kw) + bias
    e_lat, e_ctx, inv = {}, {}, {}
    for u in units:
        m = jnp.maximum(jnp.max(s_lat[u], axis=-1, keepdims=True), jnp.max(s_ctx[u], axis=-1, keepdims=True))
        e_lat[u] = jnp.exp2(s_lat[u] - m)
        ec = jnp.exp2(s_ctx[u] - m)
        e_ctx[u] = ec.astype(BF16)
        inv[u] = 1.0 / (jnp.sum(e_lat[u], axis=-1, keepdims=True) + jnp.sum(ec, axis=-1, keepdims=True))
    o_ctx = {}
    for pair in range(2):
        vc = kv_ref[0, 0:C, 512 + pair * 128:512 + (pair + 1) * 128]
        oc = dot(jnp.concatenate([e_ctx[rr, pair] for rr in range(R)], axis=0), vc)
        for rr in range(R):
            o_ctx[rr, pair] = oc[rr * 128:(rr + 1) * 128]
    o = {}
    for rr, pair in units:
        vw = kv_ref[0, pl.ds(starts[rr], W), 512 + pair * 128:512 + (pair + 1) * 128]
        ou = (dot(e_lat[rr, pair].astype(BF16), vw) + o_ctx[rr, pair]) * inv[rr, pair]
        o[rr, pair] = jnp.where(lane < 64, ou[0:GRID_W], ou[GRID_W:])
    return jnp.concatenate([jnp.concatenate([o[rr, 0], o[rr, 1]], axis=1) for rr in range(R)], axis=0)


def _attn_kernel(naq_ref, nakv_ref, bias_ref, mq_ref, mk_ref, mv_ref, dq_ref, dk_ref, dv_ref, lam_ref, sub_ref,
                 o_ref, *, nsub, skip_ctx, nct, C, Ta, rows, lam_init):
    tq = naq_ref.shape[1] // nsub
    R = tq // GRID_W
    lp = lam_ref[...]
    lam = (jnp.exp(jnp.sum(lp[0:1] * lp[1:2], axis=-1, keepdims=True))
           - jnp.exp(jnp.sum(lp[2:3] * lp[3:4], axis=-1, keepdims=True)) + lam_init)

    for sub in range(nsub):
        g = pl.program_id(1) * nsub + sub
        rs = pl.ds(sub * tq, tq)
        naq = naq_ref.at[:, rs, :]

        def ctx_tile(rs=rs, naq=naq):
            o_ref[0, rs, :] = jnp.concatenate(
                [_na_ctx_body(naq, nakv_ref, C),
                 _mla_attn_body(mq_ref[0, rs, :], mk_ref, mv_ref, C),
                 _diff_finish(_diff_body(dq_ref[0, rs, :], dk_ref, dv_ref, C, lam), lam_init, sub_ref)],
                axis=1).astype(BF16)

        def lat_tile(rs=rs, naq=naq, g=g):
            o_ref[0, rs, :] = jnp.concatenate(
                [_na_lat_body(naq, nakv_ref, bias_ref, (g - nct) * R, C, rows, R),
                 _mla_attn_body(mq_ref[0, rs, :], mk_ref, mv_ref, Ta),
                 _diff_finish(_diff_body(dq_ref[0, rs, :], dk_ref, dv_ref, Ta, lam), lam_init, sub_ref)],
                axis=1).astype(BF16)

        if sub < nct:
            if not skip_ctx:
                pl.when(g < nct)(ctx_tile)
            pl.when(g >= nct)(lat_tile)
        else:
            lat_tile()


def _attention(na_qkv, bias, mq, mk, mv, dq, dk, dv, lam_p, sub, tq, nsub, skip_ctx, nct, C, rows, lam_init):
    B, Ta, _ = na_qkv.shape
    ts = tq * nsub
    qt = lambda a: pl.BlockSpec((1, ts, a.shape[-1]), lambda b, j: (b, j, 0))
    seq = lambda a: pl.BlockSpec((1,) + a.shape[1:], lambda b, j: (b, 0, 0))
    kern = functools.partial(_attn_kernel, nsub=nsub, skip_ctx=skip_ctx, nct=nct, C=C, Ta=Ta, rows=rows,
                             lam_init=lam_init)
    return pl.pallas_call(
        kern,
        grid=(B, Ta // ts),
        in_specs=[qt(na_qkv), seq(na_qkv), _resident(bias), qt(mq), seq(mk), seq(mv), qt(dq), seq(dk), seq(dv),
                  _resident(lam_p), _resident(sub)],
        out_specs=pl.BlockSpec((1, ts, 768), lambda b, j: (b, j, 0)),
        out_shape=jax.ShapeDtypeStruct((B, Ta, 768), BF16),
        compiler_params=_cparams(("parallel", "arbitrary")),
        name="attention",
    )(na_qkv, na_qkv, bias, mq, mk, mv, dq, dk, dv, lam_p, sub)


def _rwkv_prep_kernel(z_ref, zp_ref, zn_ref, mu_ref, w0_ref, wup_ref, a0_ref, aup_ref, gup_ref,
                      kk_ref, ka_ref, rk_ref, rvk_ref, gb_ref, dir_ref, *, C, Ta):
    tm = z_ref.shape[1]
    z = z_ref[0]
    row = lax.broadcasted_iota(jnp.int32, (tm, 1), 0)
    pos = _token_rows(pl.program_id(0), tm)
    has_prev = jnp.where((pos == 0) | (pos == C), 0.0, 1.0)
    has_next = jnp.where((pos == C - 1) | (pos == Ta - 1), 0.0, 1.0)
    prev = jnp.where(row == 0, zp_ref[0, HALO - 1:HALO, :], pltpu.roll(z, 1, axis=0)) * has_prev
    nxt = jnp.where(row == tm - 1, zn_ref[0, 0:1, :], pltpu.roll(z, tm - 1, axis=0)) * has_next
    zs = z + mu_ref[0:1, :] * (prev - z) + mu_ref[1:2, :] * (nxt - z)
    r = zs[:, 0:256]
    k = zs[:, 256:512]
    v = zs[:, 512:768]
    low = zs[:, 768:896]
    ones = _head_ones()
    kk = k * kk_ref[...]
    kk = kk * lax.rsqrt(jnp.maximum(_head_sum(kk * kk, ones), 1e-24))
    g = _dot(_sigmoid(low), gup_ref[...])
    bonus = _head_sum(r * k * rk_ref[...], ones) * v
    rvk_ref[0, :, 0:256] = r
    rvk_ref[0, :, 256:512] = v
    rvk_ref[0, :, 512:768] = kk
    gb_ref[0, :, 0:256] = g
    gb_ref[0, :, 256:512] = bonus
    tl = jnp.tanh(low).astype(BF16)
    lb = low.astype(BF16)
    for d in range(2):
        wx = w0_ref[d:d + 1, :] + jnp.dot(tl, wup_ref[d], preferred_element_type=F32)
        a = _sigmoid(a0_ref[d:d + 1, :] + jnp.dot(lb, aup_ref[d], preferred_element_type=F32))
        dir_ref[d, 0, :, 0:256] = -math.exp(-0.5) * _sigmoid(wx)
        dir_ref[d, 0, :, 256:512] = k * (1.0 + (a - 1.0) * ka_ref[...])
        dir_ref[d, 0, :, 512:768] = kk * a


def _rwkv_prep(z, mu, w0, wup_p, a0, aup_p, gup_p, k_k, k_a, r_k, tm, C):
    B, Ta, _ = z.shape
    nt = Ta // tm
    hb = tm // HALO
    nh = Ta // HALO
    tok = lambda n: pl.BlockSpec((1, tm, n), lambda j, b: (b, j, 0))
    full = _resident
    kern = functools.partial(_rwkv_prep_kernel, C=C, Ta=Ta)
    return pl.pallas_call(
        kern,
        grid=(nt, B),
        in_specs=[tok(RWKV_COLS),
                  pl.BlockSpec((1, HALO, RWKV_COLS), lambda j, b: (b, jnp.maximum(j * hb - 1, 0), 0)),
                  pl.BlockSpec((1, HALO, RWKV_COLS), lambda j, b: (b, jnp.minimum((j + 1) * hb, nh - 1), 0)),
                  full(mu), full(w0), full(wup_p), full(a0), full(aup_p), full(gup_p),
                  full(k_k), full(k_a), full(r_k)],
        out_specs=[tok(768), tok(512),
                   pl.BlockSpec((2, 1, tm, 768), lambda j, b: (0, b, j, 0))],
        out_shape=[jax.ShapeDtypeStruct((B, Ta, 768), F32),
                   jax.ShapeDtypeStruct((B, Ta, 512), F32),
                   jax.ShapeDtypeStruct((2, B, Ta, 768), F32)],
        compiler_params=_cparams(("parallel", "parallel")),
        name="rwkv_prep",
    )(z, z, z, mu, w0, wup_p, a0, aup_p, gup_p, k_k, k_a, r_k)


def _rwkv_chunk_kernel(rvk_ref, dir_ref, g_ref, o_ref):
    d = pl.program_id(0)
    nchunk = rvk_ref.shape[1] // CHUNK
    bdm = _bd_mask()
    row = lax.broadcasted_iota(jnp.int32, (CHUNK, GROUP_W), 0)
    col = lax.broadcasted_iota(jnp.int32, (CHUNK, GROUP_W), 1) & (CHUNK - 1)
    tdiff = jnp.where(d == 0, row - col, col - row)
    incl = tdiff >= 0
    strict = tdiff > 0
    eye = tdiff == 0
    r2 = lax.broadcasted_iota(jnp.int32, (CHUNK, CHUNK), 0)
    c2 = lax.broadcasted_iota(jnp.int32, (CHUNK, CHUNK), 1)
    tri = jnp.where(jnp.where(d == 0, r2 - c2, c2 - r2) >= 0, 1.0, 0.0).astype(BF16)
    dot = functools.partial(jnp.dot, preferred_element_type=F32)

    def hmul(x, y):
        return dot(x.astype(BF16), _bd(y, bdm))

    def hmul_t(x, yt):
        ytb = yt.astype(BF16)
        return dot(x.astype(BF16), jnp.where(bdm, jnp.concatenate([ytb, ytb, ytb, ytb], axis=1),
                                             jnp.zeros((), BF16)))

    def head_t(x):
        xt = x.T
        return jnp.concatenate([xt[0:64], xt[64:128], xt[128:192], xt[192:256]], axis=1)

    cs = range(nchunk)
    sls = [slice(c * CHUNK, (c + 1) * CHUNK) for c in cs]
    r = [rvk_ref[0, sl, 0:256] for sl in sls]
    v = [rvk_ref[0, sl, 256:512] for sl in sls]
    kk = [rvk_ref[0, sl, 512:768] for sl in sls]
    lw = [dir_ref[0, 0, sl, 0:256] for sl in sls]
    kd = [dir_ref[0, 0, sl, 256:512] for sl in sls]
    b = [dir_ref[0, 0, sl, 512:768] for sl in sls]
    cum, tot = [], []
    for c in cs:
        l1, l2, l3 = _split3(lw[c])
        cum.append(dot(tri, l1) + (dot(tri, l2) + dot(tri, l3)))
        tot.append(jnp.sum(lw[c], axis=0, keepdims=True))
    rt = [r[c] * jnp.exp(cum[c]) for c in cs]
    at = [-kk[c] * jnp.exp(cum[c] - lw[c]) for c in cs]
    einv = [jnp.exp(-cum[c]) for c in cs]
    eend = [jnp.exp(tot[c] - cum[c]) for c in cs]
    ar = [jnp.concatenate([at[c], rt[c]], axis=0) for c in cs]
    xb = [hmul_t(ar[c], (b[c] * einv[c]).T) for c in cs]
    xk = [hmul_t(ar[c], (kd[c] * einv[c]).T) for c in cs]
    n = [jnp.where(strict, xb[c][0:CHUNK], 0.0) for c in cs]
    lrb = [jnp.where(incl, xb[c][CHUNK:], 0.0) for c in cs]
    bht = [head_t(b[c] * eend[c]) for c in cs]
    al = [jnp.concatenate([jnp.where(strict, xk[c][0:CHUNK], 0.0),
                           jnp.where(incl, xk[c][CHUNK:], 0.0),
                           head_t(kd[c] * eend[c])], axis=0) for c in cs]
    akv_lrkv = [hmul(al[c], v[c]) for c in cs]
    p = [jnp.where(eye, 1.0, 0.0) + n[c] for c in cs]
    npow = [hmul(n[c], n[c]) for c in cs]
    for _ in range(4):
        sq = [hmul(jnp.concatenate([npow[c], p[c]], axis=0), npow[c]) for c in cs]
        npow = [sq[c][0:CHUNK] for c in cs]
        p = [p[c] + sq[c][CHUNK:] for c in cs]
    t = [p[c] + hmul(p[c], npow[c]) for c in cs]
    w = [hmul(t[c], at[c]) for c in cs]
    u0 = [hmul(t[c], akv_lrkv[c][0:CHUNK]) for c in cs]
    lb = [jnp.concatenate([lrb[c], bht[c]], axis=0) for c in cs]
    xw = [hmul(lb[c], w[c]) for c in cs]
    xu = [hmul(lb[c], u0[c]) for c in cs]
    gs, outs = [], []
    for c in cs:
        gs.append(xw[c][CHUNK:] + jnp.where(eye, jnp.exp(tot[c]), 0.0))
        outs.append(jnp.concatenate([rt[c] + xw[c][0:CHUNK],
                                     xu[c][0:CHUNK] + akv_lrkv[c][CHUNK:2 * CHUNK],
                                     xu[c][CHUNK:] + akv_lrkv[c][2 * CHUNK:]], axis=1))
    g_ref[0, 0] = jnp.concatenate(gs, axis=0)
    o_ref[0, 0] = jnp.concatenate(outs, axis=0).astype(BF16)


def _rwkv_chunks(rvk, dirp, tm):
    B, Ta, _ = rvk.shape
    nt = Ta // tm
    blk = lambda n: pl.BlockSpec((1, 1, tm, n), lambda d, b, j: (d, b, j, 0))
    return pl.pallas_call(
        _rwkv_chunk_kernel,
        grid=(2, B, nt),
        in_specs=[pl.BlockSpec((1, tm, 768), lambda d, b, j: (b, j, 0)), blk(768)],
        out_specs=[blk(256), blk(768)],
        out_shape=[jax.ShapeDtypeStruct((2, B, Ta, 256), F32), jax.ShapeDtypeStruct((2, B, Ta, 768), BF16)],
        compiler_params=_cparams(("parallel", "parallel", "parallel")),
        name="rwkv_chunks",
    )(rvk, dirp)


def _rwkv_scan_kernel(gf_ref, cf_ref, gr_ref, cr_ref, yf_ref, yb_ref, s_ref):
    i = pl.program_id(1)
    gb = gf_ref.shape[1]
    bdm = _bd_mask()

    @pl.when(i == 0)
    def _():
        s_ref[...] = jnp.zeros(s_ref.shape, F32)

    dot = functools.partial(jnp.dot, preferred_element_type=F32)
    zero = jnp.zeros((), BF16)
    chains = [(d, bb) for d in range(2) for bb in range(gb)]
    g_refs = (gf_ref, gr_ref)
    c_refs = (cf_ref, cr_ref)
    parts = {}
    for d, bb in chains:
        sh, sl = _split2(s_ref[d, bb])
        gh, gl = _split2(g_refs[d][0, bb])
        parts[d, bb] = (jnp.concatenate([c_refs[d][0, bb, :, 0:256], gh], axis=0), gl,
                        jnp.where(bdm, jnp.concatenate([sh] * 4, axis=0), zero),
                        jnp.where(bdm, jnp.concatenate([sl] * 4, axis=0), zero))
    res = {}
    for ch in chains:
        lh, gl, sbh, sbl = parts[ch]
        r = dot(lh, sbh) + dot(lh, sbl)
        res[ch] = (r[0:CHUNK], r[CHUNK:] + dot(gl, sbh))
    for d, y_ref in enumerate((yf_ref, yb_ref)):
        for bb in range(gb):
            y_ref[bb] = res[d, bb][0] + c_refs[d][0, bb, :, 256:512].astype(F32)
            s_ref[d, bb] = res[d, bb][1] + c_refs[d][0, bb, :, 512:768].astype(F32)


def _rwkv_scan(chg, chb, ncc, gb):
    _, B, Ta, _ = chg.shape
    nc = Ta // CHUNK

    def rev_chunk(i):
        return jnp.where(i < ncc, ncc - 1 - i, nc - 1 - (i - ncc))

    fwd = lambda n: pl.BlockSpec((1, gb, CHUNK, n), lambda b, i: (0, b, i, 0))
    rev = lambda n: pl.BlockSpec((1, gb, CHUNK, n), lambda b, i: (1, b, rev_chunk(i), 0))
    return pl.pallas_call(
        _rwkv_scan_kernel,
        grid=(B // gb, nc),
        in_specs=[fwd(256), fwd(768), rev(256), rev(768)],
        out_specs=[pl.BlockSpec((gb, CHUNK, 256), lambda b, i: (b, i, 0)),
                   pl.BlockSpec((gb, CHUNK, 256), lambda b, i: (b, rev_chunk(i), 0))],
        out_shape=[jax.ShapeDtypeStruct((B, Ta, 256), F32),
                   jax.ShapeDtypeStruct((B, Ta, 256), F32)],
        scratch_shapes=[pltpu.VMEM((2, gb, CHUNK, GROUP_W), F32)],
        compiler_params=_cparams(("parallel", "arbitrary")),
        name="rwkv_scan",
    )(chg, chb, chg, chb)


def _out_kernel(x_ref, mod_ref, att_ref, yf_ref, yb_ref, gb_ref, lnw_ref, lnb_ref, w_ref, o_ref, *, j0, C):
    x = x_ref[0]
    tm, D = x.shape
    is_ctx = _token_rows(pl.program_id(0) + j0, tm) < C
    dot = functools.partial(jnp.dot, preferred_element_type=F32)
    ones = _head_ones()
    y = yf_ref[0] + yb_ref[0]
    mean = _head_sum(y, ones) * (1.0 / HEAD_DIM)
    yc = y - mean
    var = _head_sum(yc * yc, ones) * (1.0 / HEAD_DIM)
    yn = yc * lax.rsqrt(var + RWKV_LN_EPS) * lnw_ref[...] + lnb_ref[...]
    rw = ((yn + gb_ref[0, :, 256:512]) * gb_ref[0, :, 0:256]).astype(BF16)
    mix = (dot(att_ref[0, :, 0:512], w_ref[0:512, :]) + dot(rw, w_ref[512:768, :])) + \
        dot(att_ref[0, :, 512:768], w_ref[768:1024, :])
    o_ref[0] = x + _mod_rows(mod_ref, is_ctx, 2, D) * mix


def _out_proj(X, mod2, att, yf, yb, gbn, ln_w, ln_b, w, tm, j0, C):
    B, Ta, D = X.shape
    nt = Ta // tm - j0
    tok = lambda n: pl.BlockSpec((1, tm, n), lambda j, b: (b, j + j0, 0))
    full = _resident
    return pl.pallas_call(
        functools.partial(_out_kernel, j0=j0, C=C),
        grid=(nt, B),
        in_specs=[tok(D), _mod_spec(D), tok(768), tok(256), tok(256), tok(512), full(ln_w), full(ln_b), full(w)],
        out_specs=tok(D),
        out_shape=jax.ShapeDtypeStruct((B, Ta, D), F32),
        compiler_params=_cparams(("parallel", "parallel")),
        name="out_proj",
    )(X, mod2, att, yf, yb, gbn, ln_w, ln_b, w)


def _mlp_kernel(x_ref, xp_ref, xn_ref, mod_ref, g_ref, wa_ref, wb_ref, cw_ref, cb_ref, wd_ref, gf_ref, o_ref,
                *, j0, C, Ta, fc, final):
    j = pl.program_id(0) + j0
    tm = x_ref.shape[1]
    D = x_ref.shape[2]
    dff = wa_ref.shape[1]
    x = x_ref[0]
    ne = tm + 2 * MLP_HALO
    xe = jnp.concatenate([xp_ref[0], x, xn_ref[0]], axis=0)
    pos = _token_rows(j, tm)
    ctx_e = (j * tm - MLP_HALO + lax.broadcasted_iota(jnp.int32, (ne, 1), 0)) < C
    h = (_rms(xe, g_ref[...], NORM_EPS) * (1.0 + _mod_rows(mod_ref, ctx_e, 4, D))
         + _mod_rows(mod_ref, ctx_e, 3, D)).astype(BF16)
    pmask = jnp.where((pos == 0) | (pos == C), 0.0, 1.0)
    nmask = jnp.where((pos == C - 1) | (pos == Ta - 1), 0.0, 1.0)
    dot = functools.partial(jnp.dot, preferred_element_type=F32)
    acc = jnp.zeros((tm, D), F32)
    for c in range(dff // fc):
        cs = slice(c * fc, (c + 1) * fc)
        a = dot(h, wa_ref[:, cs])
        m0 = MLP_HALO
        b = dot(h[m0:m0 + tm], wb_ref[:, cs])
        cv = (cw_ref[0:1, cs] * (a[m0 - 1:m0 - 1 + tm] * pmask) + cw_ref[1:2, cs] * a[m0:m0 + tm]
              + cw_ref[2:3, cs] * (a[m0 + 1:m0 + 1 + tm] * nmask) + cb_ref[:, cs])
        u = cv * _sigmoid(cv) * b
        acc = acc + dot(u.astype(BF16), wd_ref[cs, :])
    y = x + _mod_rows(mod_ref, pos < C, 5, D) * acc
    o_ref[0] = _rms(y, gf_ref[...], NORM_EPS) if final else y


def _mlp(X, mod2, g, wa, wb, cw, cb, wd, gf, tm, j0, C, final):
    B, Ta, D = X.shape
    ntot = Ta // tm
    nt = ntot - j0
    hb = tm // MLP_HALO
    nh = Ta // MLP_HALO
    tok = lambda n: pl.BlockSpec((1, tm, n), lambda j, b: (b, j + j0, 0))
    full = _resident
    kern = functools.partial(_mlp_kernel, j0=j0, C=C, Ta=Ta, fc=wa.shape[1], final=final)
    if final:
        assert j0 * tm == C
        out_spec = pl.BlockSpec((1, tm, D), lambda j, b: (b, j, 0))
        out_shape = jax.ShapeDtypeStruct((B, Ta - C, D), F32)
    else:
        out_spec, out_shape = tok(D), jax.ShapeDtypeStruct((B, Ta, D), F32)
    return pl.pallas_call(
        kern,
        grid=(nt, B),
        in_specs=[tok(D),
                  pl.BlockSpec((1, MLP_HALO, D), lambda j, b: (b, jnp.maximum((j + j0) * hb - 1, j0 * hb), 0)),
                  pl.BlockSpec((1, MLP_HALO, D), lambda j, b: (b, jnp.minimum((j + j0 + 1) * hb, nh - 1), 0)),
                  _mod_spec(D),
                  full(g), full(wa), full(wb), full(cw), full(cb), full(wd), full(gf)],
        out_specs=out_spec,
        out_shape=out_shape,
        compiler_params=_cparams(("parallel", "parallel")),
        name="conv_glu",
    )(X, X, X, mod2, g, wa, wb, cw, cb, wd, gf)


def _rot_cols(w):
    s = w.shape
    x = w.reshape(s[:-1] + (s[-1] // ROPE_DIM, 4, ROPE_DIM // 4))
    r1, r2, c1, c2 = x[..., 0, :], x[..., 1, :], x[..., 2, :], x[..., 3, :]
    return jnp.stack([-r2, r1, -c2, c1], axis=-2).reshape(s)


def _rope_tables(T, C):
    t = np.arange(T)
    rowp = (t // GRID_W).astype(np.float32)
    colp = (t % GRID_W).astype(np.float32)
    half = ROPE_DIM // 2
    freqs = jnp.asarray(ROPE_THETA, F32) ** (-jnp.arange(0, half, 2, dtype=F32) / half)
    ar = jnp.asarray(rowp)[:, None] * freqs[None, :]
    ac = jnp.asarray(colp)[:, None] * freqs[None, :]
    ang = jnp.concatenate([ar, ar, ac, ac], axis=-1)
    cos = jnp.concatenate([jnp.ones((C, ROPE_DIM), F32), jnp.cos(ang)], axis=0)
    sin = jnp.concatenate([jnp.zeros((C, ROPE_DIM), F32), jnp.sin(ang)], axis=0)
    Ta = T + C
    cos256 = jnp.tile(cos, (1, 8))
    sin256 = jnp.tile(sin, (1, 8))
    one = jnp.ones((Ta, HEAD_DIM), F32)
    zero = jnp.zeros((Ta, HEAD_DIM), F32)
    cos512 = jnp.tile(jnp.concatenate([one, cos, one[:, :32]], axis=1), (1, 4))
    sin512 = jnp.tile(jnp.concatenate([zero, sin, zero[:, :32]], axis=1), (1, 4))
    return cos256, sin256, cos512, sin512


def _na_bias_tables(rpb):
    cpos = np.arange(GRID_W)
    cstart = np.clip(cpos - NA_KW // 2, 0, GRID_W - NA_KW)
    col_mask = (cpos[None, :] >= cstart[:, None]) & (cpos[None, :] < cstart[:, None] + NA_KW)
    col_idx = np.clip(cpos[None, :] - cpos[:, None] + NA_KW - 1, 0, 2 * NA_KW - 2)
    onehot = np.zeros((2 * NA_KW - 1, GRID_W, GRID_W), np.float32)
    onehot[col_idx, cpos[:, None], cpos[None, :]] = 1.0
    toep = jnp.einsum('lhrd,dqk->lhrqk', rpb, jnp.asarray(onehot), precision=lax.Precision.HIGHEST)
    toep = jnp.where(jnp.asarray(col_mask)[None, None, None], toep * LOG2E, NEG_INF)
    tabs = []
    for delta in range(NA_KH):
        bias = toep[:, :, NA_KH - 1 - delta:2 * NA_KH - 1 - delta]
        tabs.append(bias.transpose(0, 1, 3, 2, 4).reshape(rpb.shape[0], N_HEADS, GRID_W, NA_KH * GRID_W))
    return jnp.stack(tabs, axis=1)


def kernel(x, c, ctx, c_ctx, norm1_g, norm2_g, ada_w, ada_b, w_in, w_out, na_rpb, mla_q_norm, mla_kv_norm,
           mla_w_uq, mla_w_ukv, rwkv_mu, rwkv_w0, rwkv_w_up, rwkv_a0, rwkv_a_up, rwkv_g_up, rwkv_k_k,
           rwkv_k_a, rwkv_r_k, rwkv_ln_w, rwkv_ln_b, diff_lambda, diff_subln, mlp_w_up, mlp_conv_w,
           mlp_conv_b, mlp_w_down, final_norm_g):
    B, T, D = x.shape
    C = ctx.shape[1]
    L = ada_w.shape[0]
    Ta = T + C
    tm = min(TOKEN_TILE, C)
    rows = T // GRID_W
    assert C % tm == 0 and T % tm == 0 and tm % CHUNK == 0 and rows >= NA_KH and D == 1024
    nct = C // tm
    t_wide = WIDE_TILE if Ta % WIDE_TILE == 0 else tm
    t_mlp = MLP_TILE if Ta % MLP_TILE == 0 else tm
    att_sub = next(n for n in (3, 2, 1) if (Ta // tm) % n == 0)
    dff = mlp_w_down.shape[1]

    wi = w_in
    na_w = wi[:, :, 0:768]
    cq_w, ckv_w, kr_w = wi[:, :, 768:1024], wi[:, :, 1024:1152], wi[:, :, 1152:1184]
    rw_w = wi[:, :, 1184:2080]
    dq_w, dk_w, dv_w = wi[:, :, 2080:2336], wi[:, :, 2336:2592], wi[:, :, 2592:2848]
    w_all = jnp.concatenate([na_w, dq_w, _rot_cols(dq_w), dk_w, _rot_cols(dk_w), dv_w, rw_w,
                             cq_w, ckv_w, kr_w, _rot_cols(kr_w), jnp.zeros((L, D, 64), F32)],
                            axis=-1).astype(BF16)
    w_out_b = w_out.astype(BF16)
    wa_b = mlp_w_up[:, :, :dff].astype(BF16)
    wb_b = mlp_w_up[:, :, dff:].astype(BF16)
    wd_b = mlp_w_down.astype(BF16)

    uq = mla_w_uq.reshape(L, MLA_Q_RANK, N_HEADS, HEAD_DIM + ROPE_DIM)
    pad32 = jnp.zeros((L, MLA_Q_RANK, N_HEADS, 32), F32)
    wqm = jnp.concatenate([uq, pad32], axis=-1).reshape(L, MLA_Q_RANK, 512).astype(BF16)
    wqr = jnp.concatenate([jnp.zeros_like(uq[..., :HEAD_DIM]), _rot_cols(uq[..., HEAD_DIM:]), pad32],
                          axis=-1).reshape(L, MLA_Q_RANK, 512).astype(BF16)
    ukv = mla_w_ukv.reshape(L, MLA_KV_RANK, N_HEADS, 2 * HEAD_DIM)
    wk = jnp.concatenate([ukv[..., :HEAD_DIM], jnp.zeros_like(ukv[..., HEAD_DIM:])],
                         axis=-1).reshape(L, MLA_KV_RANK, 512).astype(BF16)
    wv = ukv[..., HEAD_DIM:].reshape(L, MLA_KV_RANK, 256).astype(BF16)
    pm = np.zeros((128, 1024), np.float32)
    for h in range(N_HEADS):
        for i in range(ROPE_DIM):
            pm[i, h * 128 + HEAD_DIM + i] = 1.0
            pm[ROPE_DIM + i, 512 + h * 128 + HEAD_DIM + i] = 1.0
    pmat = jnp.asarray(pm, BF16)

    zr = lambda n: jnp.zeros((L, 2, n, GROUP_W), F32)
    wup_p = jnp.concatenate([rwkv_w_up, zr(96)], axis=2).astype(BF16)
    aup_p = jnp.concatenate([zr(32), rwkv_a_up, zr(64)], axis=2).astype(BF16)
    gup_p = jnp.concatenate([jnp.zeros((L, 64, GROUP_W), F32), rwkv_g_up], axis=1).astype(BF16)

    cos256, sin256, cos512, sin512 = _rope_tables(T, C)
    na_bias = _na_bias_tables(na_rpb)
    sub256 = jnp.tile(diff_subln, (1, N_HEADS))

    R = ((B + 1 + 7) // 8) * 8
    cc = jnp.concatenate([c, c_ctx[None], jnp.zeros((R - B - 1, D), F32)], axis=0)
    mod = _modulation(cc, ada_w, ada_b)
    mod2 = jnp.stack([jnp.broadcast_to(mod[:, B:B + 1], (L, B, 6 * D)), mod[:, :B]], axis=2)
    mod2 = mod2.reshape(L, 2 * B, 1, 6 * D)

    X = jnp.concatenate([ctx, x], axis=1)
    gb = next(n for n in (8, 4, 2, 1) if B % n == 0)
    for l in range(L):
        need_ctx = l < L - 1
        j0 = 0 if need_ctx else nct
        na_qkv, dq, dk, dv, z_rw, mq, mk, mv = _in_proj(
            X, mod2[l], norm1_g[l][None], w_all[l], cos256, sin256, cos512, sin512,
            mla_q_norm[l][None], mla_kv_norm[l][None], wqm[l], wqr[l], wk[l], wv[l], pmat, t_wide, C)
        lam_init = 0.8 - 0.6 * math.exp(-0.3 * l)
        att = _attention(na_qkv, na_bias[l], mq, mk, mv, dq, dk, dv, diff_lambda[l], sub256[l][None],
                         tm, att_sub, not need_ctx, nct, C, rows, lam_init)
        rvk, gbn, dirp = _rwkv_prep(z_rw, rwkv_mu[l], rwkv_w0[l], wup_p[l], rwkv_a0[l], aup_p[l], gup_p[l],
                                    rwkv_k_k[l][None], rwkv_k_a[l][None], rwkv_r_k[l].reshape(1, GROUP_W),
                                    t_wide, C)
        chg, chb = _rwkv_chunks(rvk, dirp, t_wide)
        yf, yb = _rwkv_scan(chg, chb, C // CHUNK, gb)
        X = _out_proj(X, mod2[l], att, yf, yb, gbn, rwkv_ln_w[l][None], rwkv_ln_b[l][None],
                      w_out_b[l], t_wide if need_ctx else tm, j0, C)
        X = _mlp(X, mod2[l], norm2_g[l][None], wa_b[l], wb_b[l], mlp_conv_w[l], mlp_conv_b[l][None], wd_b[l],
                 final_norm_g[None], t_mlp if need_ctx else tm, j0, C, final=not need_ctx)
    return X
```

```python
import functools
import math

import jax
import jax.numpy as jnp
import numpy as np
from jax import lax
from jax.experimental import pallas as pl
from jax.experimental.pallas import tpu as pltpu

F32 = jnp.float32
BF16 = jnp.bfloat16

GRID_W = 64
GROUP_W = 256
HEAD_DIM = 64
N_HEADS = 4
ROPE_DIM = 32
ROPE_THETA = 10000.0
NORM_EPS = 1e-6
NEG_INF = -1e30
LOG2E = math.log2(math.e)
NA_KH = 8
NA_KW = 16
MLA_Q_RANK = 256
MLA_KV_RANK = 128
RWKV_COLS = 896
RWKV_LN_EPS = 64e-5
DIFF_QK = 32
DIFF_LN_EPS = 1e-5
IN_SPLITS = (768, 1184, 2080, 2848)

W_NA = 0
W_DIFF = 768
W_RWKV = 2048
W_MLA = 2944
W_TOT = 3456

TOKEN_TILE = 256
CHUNK = 64
WIDE_TILE = 768
MLP_TILE = 384
HALO = 8
MLP_HALO = HALO
VMEM_LIMIT = 56 * 1024 * 1024


def _cparams(sem):
    return pltpu.CompilerParams(dimension_semantics=sem, vmem_limit_bytes=VMEM_LIMIT)


def _resident(a):
    return pl.BlockSpec(a.shape, lambda *_: (0,) * a.ndim, pipeline_mode=pl.Buffered(1))


def _dot(a, b):
    return jnp.dot(a.astype(BF16), b.astype(BF16), preferred_element_type=F32)


def _dot_nt(a, b):
    return lax.dot_general(a.astype(BF16), b.astype(BF16), (((1,), (1,)), ((), ())),
                           preferred_element_type=F32)


def _split2(x):
    hi = x.astype(BF16)
    lo = (x - hi.astype(F32)).astype(BF16)
    return hi, lo


def _split3(x):
    h1 = x.astype(BF16)
    r1 = x - h1.astype(F32)
    h2 = r1.astype(BF16)
    h3 = (r1 - h2.astype(F32)).astype(BF16)
    return h1, h2, h3


def _dot3(a, b):
    ah, al = _split2(a)
    bh, bl = _split2(b)
    d = functools.partial(jnp.dot, preferred_element_type=F32)
    return d(ah, bh) + (d(ah, bl) + d(al, bh))


def _sigmoid(x):
    return 1.0 / (1.0 + jnp.exp(-x))


def _rms(x, g, eps):
    return x * lax.rsqrt(jnp.mean(x * x, axis=-1, keepdims=True) + eps) * g


def _token_rows(j, tm):
    return j * tm + lax.broadcasted_iota(jnp.int32, (tm, 1), 0)


def _mod_rows(mod_ref, is_ctx, k, D):
    return jnp.where(is_ctx, mod_ref[0][:, k * D:(k + 1) * D], mod_ref[1][:, k * D:(k + 1) * D])


def _mod_spec(D):
    return pl.BlockSpec((2, 1, 6 * D), lambda j, b: (b, 0, 0))


def _head_ones(n=GROUP_W):
    r = lax.broadcasted_iota(jnp.int32, (n, n), 0) >> 6
    c = lax.broadcasted_iota(jnp.int32, (n, n), 1) >> 6
    return jnp.where(r == c, 1.0, 0.0).astype(BF16)


def _head_sum(x, ones):
    hi, lo = _split2(x)
    d = functools.partial(jnp.dot, preferred_element_type=F32)
    return d(hi, ones) + d(lo, ones)


def _bd_mask():
    r = lax.broadcasted_iota(jnp.int32, (GROUP_W, GROUP_W), 0) >> 6
    c = lax.broadcasted_iota(jnp.int32, (GROUP_W, GROUP_W), 1) >> 6
    return r == c


def _bd(x, mask):
    xb = x.astype(BF16)
    return jnp.where(mask, jnp.concatenate([xb, xb, xb, xb], axis=0), jnp.zeros((), BF16))


def _mod_kernel(s_ref, w_ref, b_ref, o_ref):
    s = s_ref[...]
    s = s * _sigmoid(s)
    o_ref[0] = _dot3(s, w_ref[0]) + b_ref[0]


def _modulation(cc, ada_w, ada_b):
    L, D, N = ada_w.shape
    R = cc.shape[0]
    tn = 1536
    return pl.pallas_call(
        _mod_kernel,
        grid=(L, N // tn),
        in_specs=[pl.BlockSpec((R, D), lambda l, n: (0, 0)),
                  pl.BlockSpec((1, D, tn), lambda l, n: (l, 0, n)),
                  pl.BlockSpec((1, 1, tn), lambda l, n: (l, 0, n))],
        out_specs=pl.BlockSpec((1, R, tn), lambda l, n: (l, 0, n)),
        out_shape=jax.ShapeDtypeStruct((L, R, N), F32),
        compiler_params=_cparams(("parallel", "parallel")),
        name="adaln_mod",
    )(cc, ada_w, ada_b.reshape(L, 1, N))


def _in_kernel(x_ref, mod_ref, g_ref, w_ref, cos_ref, sin_ref, cos5_ref, sin5_ref,
               qn_ref, kvn_ref, wqm_ref, wqr_ref, wk_ref, wv_ref, p_ref,
               na_ref, dq_ref, dk_ref, dv_ref, rw_ref, mq_ref, mk_ref, mv_ref, *, C):
    x = x_ref[0]
    tm, D = x.shape
    is_ctx = _token_rows(pl.program_id(0), tm) < C
    h = _rms(x, g_ref[...], NORM_EPS) * (1.0 + _mod_rows(mod_ref, is_ctx, 1, D)) + _mod_rows(mod_ref, is_ctx, 0, D)
    hb = h.astype(BF16)
    dot = functools.partial(jnp.dot, preferred_element_type=F32)
    zall = dot(hb, w_ref[...])
    na_ref[0, :, 0:256] = (zall[:, 0:256] * (HEAD_DIM ** -0.5 * LOG2E)).astype(BF16)
    na_ref[0, :, 256:768] = zall[:, 256:W_DIFF].astype(BF16)
    d = zall[:, W_DIFF:W_RWKV]
    cos = cos_ref[...]
    sin = sin_ref[...]
    scale = DIFF_QK ** -0.5 * LOG2E
    dq_ref[0] = ((d[:, 0:256] * cos + d[:, 256:512] * sin) * scale).astype(BF16)
    dk_ref[0] = (d[:, 512:768] * cos + d[:, 768:1024] * sin).astype(BF16)
    dv_ref[0] = _with_ones(d[:, 1024:1280])
    rw_ref[0] = zall[:, W_RWKV:W_MLA]
    z = zall[:, W_MLA:W_TOT]
    cos5 = cos5_ref[...]
    sin5 = sin5_ref[...]
    nq = _rms(z[:, 0:256], qn_ref[...], NORM_EPS).astype(BF16)
    mscale = (HEAD_DIM + ROPE_DIM) ** -0.5 * LOG2E
    mq_ref[0] = ((dot(nq, wqm_ref[...]) * cos5 + dot(nq, wqr_ref[...]) * sin5) * mscale).astype(BF16)
    nkv = _rms(z[:, 256:384], kvn_ref[...], NORM_EPS).astype(BF16)
    kr = dot(z[:, 384:512].astype(BF16), p_ref[...])
    mk_ref[0] = (dot(nkv, wk_ref[...]) + kr[:, 0:512] * cos5 + kr[:, 512:1024] * sin5).astype(BF16)
    mv_ref[0] = _with_ones(dot(nkv, wv_ref[...]))


def _in_proj(X, mod2, g, w, cos256, sin256, cos512, sin512, qn, kvn, wqm, wqr, wk, wv, pmat, tm, C):
    B, Ta, D = X.shape
    nt = Ta // tm
    tok = lambda n: pl.BlockSpec((1, tm, n), lambda j, b: (b, j, 0))
    tab = lambda n: pl.BlockSpec((tm, n), lambda j, b: (j, 0))
    full = _resident
    return pl.pallas_call(
        functools.partial(_in_kernel, C=C),
        grid=(nt, B),
        in_specs=[tok(D), _mod_spec(D),
                  full(g), full(w), tab(256), tab(256), tab(512), tab(512),
                  full(qn), full(kvn), full(wqm), full(wqr), full(wk), full(wv), full(pmat)],
        out_specs=[tok(768), tok(256), tok(256), tok(512), tok(RWKV_COLS), tok(512), tok(512), tok(512)],
        out_shape=[jax.ShapeDtypeStruct((B, Ta, 768), BF16),
                   jax.ShapeDtypeStruct((B, Ta, 256), BF16),
                   jax.ShapeDtypeStruct((B, Ta, 256), BF16),
                   jax.ShapeDtypeStruct((B, Ta, 512), BF16),
                   jax.ShapeDtypeStruct((B, Ta, RWKV_COLS), F32),
                   jax.ShapeDtypeStruct((B, Ta, 512), BF16),
                   jax.ShapeDtypeStruct((B, Ta, 512), BF16),
                   jax.ShapeDtypeStruct((B, Ta, 512), BF16)],
        compiler_params=_cparams(("parallel", "parallel")),
        name="in_proj",
    )(X, mod2, g, w, cos256, sin256, cos512, sin512, qn, kvn, wqm, wqr, wk, wv, pmat)


def _softmax_parts(s):
    m = jnp.max(s, axis=-1, keepdims=True)
    e = jnp.exp2(s - m)
    return e, jnp.sum(e, axis=-1, keepdims=True)


def _with_ones(v):
    one = jnp.ones((v.shape[0], 128), F32)
    return jnp.concatenate([v[:, 0:128], one, v[:, 128:256], one], axis=1).astype(BF16)


def _attend(s, vp1):
    sb = s.astype(BF16)
    e = jnp.exp2(sb - jnp.max(sb, axis=-1, keepdims=True))
    r = jnp.dot(e, vp1, preferred_element_type=F32)
    return r[:, 0:128] / r[:, 128:256]


def _mla_attn_body(q, k_ref, v_ref, nk):
    tq = q.shape[0]
    lane = lax.broadcasted_iota(jnp.int32, (tq, 128), 1)
    outs = []
    for pair in range(2):
        vp1 = v_ref[0, 0:nk, pair * 256:(pair + 1) * 256]
        o2 = [_attend(_dot_nt(q[:, h * 128:(h + 1) * 128], k_ref[0, 0:nk, h * 128:(h + 1) * 128]), vp1)
              for h in (2 * pair, 2 * pair + 1)]
        outs.append(jnp.where(lane < 64, o2[0], o2[1]))
    return jnp.concatenate(outs, axis=1)


def _diff_body(q, k_ref, v_ref, nk, lam):
    tq = q.shape[0]
    k = k_ref[0, 0:nk, :]
    sub = lax.broadcasted_iota(jnp.int32, (tq, GROUP_W), 1) >> 5
    lane = lax.broadcasted_iota(jnp.int32, (tq, 128), 1)
    zero = jnp.zeros((), BF16)
    outs = []
    for pair in range(2):
        vp1 = v_ref[0, 0:nk, pair * 256:(pair + 1) * 256]
        o2 = []
        for hh in range(2):
            h = 2 * pair + hh
            o2.append(_attend(_dot_nt(jnp.where(sub == 2 * h, q, zero), k), vp1)
                      - lam * _attend(_dot_nt(jnp.where(sub == 2 * h + 1, q, zero), k), vp1))
        outs.append(jnp.where(lane < 64, o2[0], o2[1]))
    return jnp.concatenate(outs, axis=1)


def _diff_finish(o, lam_init, sub_ref):
    ms = _head_sum(o * o, _head_ones()) * (1.0 / HEAD_DIM)
    return (o * lax.rsqrt(ms + DIFF_LN_EPS) * sub_ref[...]) * (1.0 - lam_init)


def _na_ctx_body(qt_ref, kv_ref, C):
    tq = qt_ref.shape[1]
    zero = jnp.zeros((), BF16)
    lane = lax.broadcasted_iota(jnp.int32, (tq, 128), 1)
    outs = []
    for pair in range(2):
        q2 = qt_ref[0, :, pair * 128:(pair + 1) * 128]
        kc = kv_ref[0, 0:C, 256 + pair * 128:256 + (pair + 1) * 128]
        vc = kv_ref[0, 0:C, 512 + pair * 128:512 + (pair + 1) * 128]
        o2 = []
        for hh in range(2):
            qm = jnp.where((lane < 64) if hh == 0 else (lane >= 64), q2, zero)
            e, l = _softmax_parts(_dot_nt(qm, kc))
            o2.append(jnp.dot(e.astype(BF16), vc, preferred_element_type=F32) * (1.0 / l))
        outs.append(jnp.where(lane < 64, o2[0], o2[1]))
    return jnp.concatenate(outs, axis=1)


def _na_lat_body(qt_ref, kv_ref, bias_ref, first_row, C, rows, R):
    zero = jnp.zeros((), BF16)
    lane = lax.broadcasted_iota(jnp.int32, (GRID_W, 128), 1)
    W = NA_KH * GRID_W
    dot = functools.partial(jnp.dot, preferred_element_type=F32)
    starts, deltas = [], []
    for rr in range(R):
        r = first_row + rr
        rs = jnp.clip(r - NA_KH // 2, 0, rows - NA_KH)
        deltas.append(r - rs)
        starts.append(pl.multiple_of(C + rs * GRID_W, GRID_W))
    units = [(rr, pair) for pair in range(2) for rr in range(R)]
    qs = {}
    for rr, pair in units:
        q2 = qt_ref[0, rr * GRID_W:(rr + 1) * GRID_W, pair * 128:(pair + 1) * 128]
        qs[rr, pair] = jnp.concatenate([jnp.where(lane < 64, q2, zero), jnp.where(lane >= 64, q2, zero)], axis=0)
    s_ctx = {}
    for pair in range(2):
        kc = kv_ref[0, 0:C, 256 + pair * 128:256 + (pair + 1) * 128]
        sc = _dot_nt(jnp.concatenate([qs[rr, pair] for rr in range(R)], axis=0), kc)
        for rr in range(R):
            s_ctx[rr, pair] = sc[rr * 128:(rr + 1) * 128]
    s_lat = {}
    for rr, pair in units:
        kw = kv_ref[0, pl.ds(starts[rr], W), 256 + pair * 128:256 + (pair + 1) * 128]
        bias = jnp.concatenate([bias_ref[deltas[rr], 2 * pair], bias_ref[deltas[rr], 2 * pair + 1]], axis=0)
        s_lat[rr, pair] = _dot_nt(qs[rr, pair], kw) + bias
    e_lat, e_ctx, inv = {}, {}, {}
    for u in units:
        m = jnp.maximum(jnp.max(s_lat[u], axis=-1, keepdims=True), jnp.max(s_ctx[u], axis=-1, keepdims=True))
        e_lat[u] = jnp.exp2(s_lat[u] - m)
        ec = jnp.exp2(s_ctx[u] - m)
        e_ctx[u] = ec.astype(BF16)
        inv[u] = 1.0 / (jnp.sum(e_lat[u], axis=-1, keepdims=True) + jnp.sum(ec, axis=-1, keepdims=True))
    o_ctx = {}
    for pair in range(2):
        vc = kv_ref[0, 0:C, 512 + pair * 128:512 + (pair + 1) * 128]
        oc = dot(jnp.concatenate([e_ctx[rr, pair] for rr in range(R)], axis=0), vc)
        for rr in range(R):
            o_ctx[rr, pair] = oc[rr * 128:(rr + 1) * 128]
    o = {}
    for rr, pair in units:
        vw = kv_ref[0, pl.ds(starts[rr], W), 512 + pair * 128:512 + (pair + 1) * 128]
        ou = (dot(e_lat[rr, pair].astype(BF16), vw) + o_ctx[rr, pair]) * inv[rr, pair]
        o[rr, pair] = jnp.where(lane < 64, ou[0:GRID_W], ou[GRID_W:])
    return jnp.concatenate([jnp.concatenate([o[rr, 0], o[rr, 1]], axis=1) for rr in range(R)], axis=0)


def _attn_kernel(naq_ref, nakv_ref, bias_ref, mq_ref, mk_ref, mv_ref, dq_ref, dk_ref, dv_ref, lam_ref, sub_ref,
                 o_ref, *, nsub, skip_ctx, nct, C, Ta, rows, lam_init):
    tq = naq_ref.shape[1] // nsub
    R = tq // GRID_W
    lp = lam_ref[...]
    lam = (jnp.exp(jnp.sum(lp[0:1] * lp[1:2], axis=-1, keepdims=True))
           - jnp.exp(jnp.sum(lp[2:3] * lp[3:4], axis=-1, keepdims=True)) + lam_init)

    for sub in range(nsub):
        g = pl.program_id(1) * nsub + sub
        rs = pl.ds(sub * tq, tq)
        naq = naq_ref.at[:, rs, :]

        def ctx_tile(rs=rs, naq=naq):
            o_ref[0, rs, :] = jnp.concatenate(
                [_na_ctx_body(naq, nakv_ref, C),
                 _mla_attn_body(mq_ref[0, rs, :], mk_ref, mv_ref, C),
                 _diff_finish(_diff_body(dq_ref[0, rs, :], dk_ref, dv_ref, C, lam), lam_init, sub_ref)],
                axis=1).astype(BF16)

        def lat_tile(rs=rs, naq=naq, g=g):
            o_ref[0, rs, :] = jnp.concatenate(
                [_na_lat_body(naq, nakv_ref, bias_ref, (g - nct) * R, C, rows, R),
                 _mla_attn_body(mq_ref[0, rs, :], mk_ref, mv_ref, Ta),
                 _diff_finish(_diff_body(dq_ref[0, rs, :], dk_ref, dv_ref, Ta, lam), lam_init, sub_ref)],
                axis=1).astype(BF16)

        if sub < nct:
            if not skip_ctx:
                pl.when(g < nct)(ctx_tile)
            pl.when(g >= nct)(lat_tile)
        else:
            lat_tile()


def _attention(na_qkv, bias, mq, mk, mv, dq, dk, dv, lam_p, sub, tq, nsub, skip_ctx, nct, C, rows, lam_init):
    B, Ta, _ = na_qkv.shape
    ts = tq * nsub
    qt = lambda a: pl.BlockSpec((1, ts, a.shape[-1]), lambda b, j: (b, j, 0))
    seq = lambda a: pl.BlockSpec((1,) + a.shape[1:], lambda b, j: (b, 0, 0))
    kern = functools.partial(_attn_kernel, nsub=nsub, skip_ctx=skip_ctx, nct=nct, C=C, Ta=Ta, rows=rows,
                             lam_init=lam_init)
    return pl.pallas_call(
        kern,
        grid=(B, Ta // ts),
        in_specs=[qt(na_qkv), seq(na_qkv), _resident(bias), qt(mq), seq(mk), seq(mv), qt(dq), seq(dk), seq(dv),
                  _resident(lam_p), _resident(sub)],
        out_specs=pl.BlockSpec((1, ts, 768), lambda b, j: (b, j, 0)),
        out_shape=jax.ShapeDtypeStruct((B, Ta, 768), BF16),
        compiler_params=_cparams(("parallel", "arbitrary")),
        name="attention",
    )(na_qkv, na_qkv, bias, mq, mk, mv, dq, dk, dv, lam_p, sub)


def _rwkv_prep_kernel(z_ref, zp_ref, zn_ref, mu_ref, w0_ref, wup_ref, a0_ref, aup_ref, gup_ref,
                      kk_ref, ka_ref, rk_ref, rvk_ref, gb_ref, dir_ref, *, C, Ta):
    tm = z_ref.shape[1]
    z = z_ref[0]
    row = lax.broadcasted_iota(jnp.int32, (tm, 1), 0)
    pos = _token_rows(pl.program_id(0), tm)
    has_prev = jnp.where((pos == 0) | (pos == C), 0.0, 1.0)
    has_next = jnp.where((pos == C - 1) | (pos == Ta - 1), 0.0, 1.0)
    prev = jnp.where(row == 0, zp_ref[0, HALO - 1:HALO, :], pltpu.roll(z, 1, axis=0)) * has_prev
    nxt = jnp.where(row == tm - 1, zn_ref[0, 0:1, :], pltpu.roll(z, tm - 1, axis=0)) * has_next
    zs = z + mu_ref[0:1, :] * (prev - z) + mu_ref[1:2, :] * (nxt - z)
    r = zs[:, 0:256]
    k = zs[:, 256:512]
    v = zs[:, 512:768]
    low = zs[:, 768:896]
    ones = _head_ones()
    kk = k * kk_ref[...]
    kk = kk * lax.rsqrt(jnp.maximum(_head_sum(kk * kk, ones), 1e-24))
    g = _dot(_sigmoid(low), gup_ref[...])
    bonus = _head_sum(r * k * rk_ref[...], ones) * v
    rvk_ref[0, :, 0:256] = r
    rvk_ref[0, :, 256:512] = v
    rvk_ref[0, :, 512:768] = kk
    gb_ref[0, :, 0:256] = g.astype(BF16)
    gb_ref[0, :, 256:512] = bonus.astype(BF16)
    tl = jnp.tanh(low).astype(BF16)
    lb = low.astype(BF16)
    for d in range(2):
        wx = w0_ref[d:d + 1, :] + jnp.dot(tl, wup_ref[d], preferred_element_type=F32)
        a = _sigmoid(a0_ref[d:d + 1, :] + jnp.dot(lb, aup_ref[d], preferred_element_type=F32))
        dir_ref[d, 0, :, 0:256] = -math.exp(-0.5) * _sigmoid(wx)
        dir_ref[d, 0, :, 256:512] = k * (1.0 + (a - 1.0) * ka_ref[...])
        dir_ref[d, 0, :, 512:768] = kk * a


def _rwkv_prep(z, mu, w0, wup_p, a0, aup_p, gup_p, k_k, k_a, r_k, tm, C):
    B, Ta, _ = z.shape
    nt = Ta // tm
    hb = tm // HALO
    nh = Ta // HALO
    tok = lambda n: pl.BlockSpec((1, tm, n), lambda j, b: (b, j, 0))
    full = _resident
    kern = functools.partial(_rwkv_prep_kernel, C=C, Ta=Ta)
    return pl.pallas_call(
        kern,
        grid=(nt, B),
        in_specs=[tok(RWKV_COLS),
                  pl.BlockSpec((1, HALO, RWKV_COLS), lambda j, b: (b, jnp.maximum(j * hb - 1, 0), 0)),
                  pl.BlockSpec((1, HALO, RWKV_COLS), lambda j, b: (b, jnp.minimum((j + 1) * hb, nh - 1), 0)),
                  full(mu), full(w0), full(wup_p), full(a0), full(aup_p), full(gup_p),
                  full(k_k), full(k_a), full(r_k)],
        out_specs=[tok(768), tok(512),
                   pl.BlockSpec((2, 1, tm, 768), lambda j, b: (0, b, j, 0))],
        out_shape=[jax.ShapeDtypeStruct((B, Ta, 768), F32),
                   jax.ShapeDtypeStruct((B, Ta, 512), BF16),
                   jax.ShapeDtypeStruct((2, B, Ta, 768), F32)],
        compiler_params=_cparams(("parallel", "parallel")),
        name="rwkv_prep",
    )(z, z, z, mu, w0, wup_p, a0, aup_p, gup_p, k_k, k_a, r_k)


def _rwkv_chunk_kernel(rvk_ref, dir_ref, g_ref, o_ref):
    d = pl.program_id(0)
    nchunk = rvk_ref.shape[1] // CHUNK
    bdm = _bd_mask()
    row = lax.broadcasted_iota(jnp.int32, (CHUNK, GROUP_W), 0)
    col = lax.broadcasted_iota(jnp.int32, (CHUNK, GROUP_W), 1) & (CHUNK - 1)
    tdiff = jnp.where(d == 0, row - col, col - row)
    incl = tdiff >= 0
    strict = tdiff > 0
    eye = tdiff == 0
    r2 = lax.broadcasted_iota(jnp.int32, (CHUNK, CHUNK), 0)
    c2 = lax.broadcasted_iota(jnp.int32, (CHUNK, CHUNK), 1)
    tri = jnp.where(jnp.where(d == 0, r2 - c2, c2 - r2) >= 0, 1.0, 0.0).astype(BF16)
    dot = functools.partial(jnp.dot, preferred_element_type=F32)

    def hmul(x, y):
        return dot(x.astype(BF16), _bd(y, bdm))

    def hmul_t(x, yt):
        ytb = yt.astype(BF16)
        return dot(x.astype(BF16), jnp.where(bdm, jnp.concatenate([ytb, ytb, ytb, ytb], axis=1),
                                             jnp.zeros((), BF16)))

    def head_t(x):
        xt = x.T
        return jnp.concatenate([xt[0:64], xt[64:128], xt[128:192], xt[192:256]], axis=1)

    cs = range(nchunk)
    sls = [slice(c * CHUNK, (c + 1) * CHUNK) for c in cs]
    r = [rvk_ref[0, sl, 0:256] for sl in sls]
    v = [rvk_ref[0, sl, 256:512] for sl in sls]
    kk = [rvk_ref[0, sl, 512:768] for sl in sls]
    lw = [dir_ref[0, 0, sl, 0:256] for sl in sls]
    kd = [dir_ref[0, 0, sl, 256:512] for sl in sls]
    b = [dir_ref[0, 0, sl, 512:768] for sl in sls]
    cum, tot = [], []
    for c in cs:
        l1, l2, l3 = _split3(lw[c])
        cum.append(dot(tri, l1) + (dot(tri, l2) + dot(tri, l3)))
        tot.append(jnp.sum(lw[c], axis=0, keepdims=True))
    rt = [r[c] * jnp.exp(cum[c]) for c in cs]
    at = [-kk[c] * jnp.exp(cum[c] - lw[c]) for c in cs]
    einv = [jnp.exp(-cum[c]) for c in cs]
    eend = [jnp.exp(tot[c] - cum[c]) for c in cs]
    ar = [jnp.concatenate([at[c], rt[c]], axis=0) for c in cs]
    xb = [hmul_t(ar[c], (b[c] * einv[c]).T) for c in cs]
    xk = [hmul_t(ar[c], (kd[c] * einv[c]).T) for c in cs]
    n = [jnp.where(strict, xb[c][0:CHUNK], 0.0) for c in cs]
    lrb = [jnp.where(incl, xb[c][CHUNK:], 0.0) for c in cs]
    bht = [head_t(b[c] * eend[c]) for c in cs]
    al = [jnp.concatenate([jnp.where(strict, xk[c][0:CHUNK], 0.0),
                           jnp.where(incl, xk[c][CHUNK:], 0.0),
                           head_t(kd[c] * eend[c])], axis=0) for c in cs]
    akv_lrkv = [hmul(al[c], v[c]) for c in cs]
    p = [jnp.where(eye, 1.0, 0.0) + n[c] for c in cs]
    npow = [hmul(n[c], n[c]) for c in cs]
    for _ in range(4):
        sq = [hmul(jnp.concatenate([npow[c], p[c]], axis=0), npow[c]) for c in cs]
        npow = [sq[c][0:CHUNK] for c in cs]
        p = [p[c] + sq[c][CHUNK:] for c in cs]
    t = [p[c] + hmul(p[c], npow[c]) for c in cs]
    w = [hmul(t[c], at[c]) for c in cs]
    u0 = [hmul(t[c], akv_lrkv[c][0:CHUNK]) for c in cs]
    lb = [jnp.concatenate([lrb[c], bht[c]], axis=0) for c in cs]
    xw = [hmul(lb[c], w[c]) for c in cs]
    xu = [hmul(lb[c], u0[c]) for c in cs]
    gs, outs = [], []
    for c in cs:
        gs.append(xw[c][CHUNK:] + jnp.where(eye, jnp.exp(tot[c]), 0.0))
        outs.append(jnp.concatenate([rt[c] + xw[c][0:CHUNK],
                                     xu[c][0:CHUNK] + akv_lrkv[c][CHUNK:2 * CHUNK],
                                     xu[c][CHUNK:] + akv_lrkv[c][2 * CHUNK:]], axis=1))
    g_ref[0, 0] = jnp.concatenate(gs, axis=0)
    o_ref[0, 0] = jnp.concatenate(outs, axis=0).astype(BF16)


def _rwkv_chunks(rvk, dirp, tm):
    B, Ta, _ = rvk.shape
    nt = Ta // tm
    blk = lambda n: pl.BlockSpec((1, 1, tm, n), lambda d, b, j: (d, b, j, 0))
    return pl.pallas_call(
        _rwkv_chunk_kernel,
        grid=(2, B, nt),
        in_specs=[pl.BlockSpec((1, tm, 768), lambda d, b, j: (b, j, 0)), blk(768)],
        out_specs=[blk(256), blk(768)],
        out_shape=[jax.ShapeDtypeStruct((2, B, Ta, 256), F32), jax.ShapeDtypeStruct((2, B, Ta, 768), BF16)],
        compiler_params=_cparams(("parallel", "parallel", "parallel")),
        name="rwkv_chunks",
    )(rvk, dirp)


def _rwkv_scan_kernel(gf_ref, cf_ref, gr_ref, cr_ref, yf_ref, yb_ref, s_ref):
    i = pl.program_id(1)
    gb = gf_ref.shape[1]
    bdm = _bd_mask()

    @pl.when(i == 0)
    def _():
        s_ref[...] = jnp.zeros(s_ref.shape, F32)

    dot = functools.partial(jnp.dot, preferred_element_type=F32)
    zero = jnp.zeros((), BF16)
    chains = [(d, bb) for d in range(2) for bb in range(gb)]
    g_refs = (gf_ref, gr_ref)
    c_refs = (cf_ref, cr_ref)
    parts = {}
    for d, bb in chains:
        sh, sl = _split2(s_ref[d, bb])
        gh, gl = _split2(g_refs[d][0, bb])
        parts[d, bb] = (jnp.concatenate([c_refs[d][0, bb, :, 0:256], gh], axis=0), gl,
                        jnp.where(bdm, jnp.concatenate([sh] * 4, axis=0), zero),
                        jnp.where(bdm, jnp.concatenate([sl] * 4, axis=0), zero))
    res = {}
    for ch in chains:
        lh, gl, sbh, sbl = parts[ch]
        r = dot(lh, sbh) + dot(lh, sbl)
        res[ch] = (r[0:CHUNK], r[CHUNK:] + dot(gl, sbh))
    for d, y_ref in enumerate((yf_ref, yb_ref)):
        for bb in range(gb):
            y_ref[bb] = (res[d, bb][0] + c_refs[d][0, bb, :, 256:512].astype(F32)).astype(BF16)
            s_ref[d, bb] = res[d, bb][1] + c_refs[d][0, bb, :, 512:768].astype(F32)


def _rwkv_scan(chg, chb, ncc, gb):
    _, B, Ta, _ = chg.shape
    nc = Ta // CHUNK

    def rev_chunk(i):
        return jnp.where(i < ncc, ncc - 1 - i, nc - 1 - (i - ncc))

    fwd = lambda n: pl.BlockSpec((1, gb, CHUNK, n), lambda b, i: (0, b, i, 0))
    rev = lambda n: pl.BlockSpec((1, gb, CHUNK, n), lambda b, i: (1, b, rev_chunk(i), 0))
    return pl.pallas_call(
        _rwkv_scan_kernel,
        grid=(B // gb, nc),
        in_specs=[fwd(256), fwd(768), rev(256), rev(768)],
        out_specs=[pl.BlockSpec((gb, CHUNK, 256), lambda b, i: (b, i, 0)),
                   pl.BlockSpec((gb, CHUNK, 256), lambda b, i: (b, rev_chunk(i), 0))],
        out_shape=[jax.ShapeDtypeStruct((B, Ta, 256), BF16),
                   jax.ShapeDtypeStruct((B, Ta, 256), BF16)],
        scratch_shapes=[pltpu.VMEM((2, gb, CHUNK, GROUP_W), F32)],
        compiler_params=_cparams(("parallel", "arbitrary")),
        name="rwkv_scan",
    )(chg, chb, chg, chb)


def _out_kernel(x_ref, mod_ref, att_ref, yf_ref, yb_ref, gb_ref, lnw_ref, lnb_ref, w_ref, o_ref, *, j0, C):
    x = x_ref[0]
    tm, D = x.shape
    is_ctx = _token_rows(pl.program_id(0) + j0, tm) < C
    dot = functools.partial(jnp.dot, preferred_element_type=F32)
    ones = _head_ones()
    y = yf_ref[0].astype(F32) + yb_ref[0].astype(F32)
    mean = _head_sum(y, ones) * (1.0 / HEAD_DIM)
    yc = y - mean
    var = _head_sum(yc * yc, ones) * (1.0 / HEAD_DIM)
    yn = yc * lax.rsqrt(var + RWKV_LN_EPS) * lnw_ref[...] + lnb_ref[...]
    rw = ((yn + gb_ref[0, :, 256:512].astype(F32)) * gb_ref[0, :, 0:256].astype(F32)).astype(BF16)
    mix = (dot(att_ref[0, :, 0:512], w_ref[0:512, :]) + dot(rw, w_ref[512:768, :])) + \
        dot(att_ref[0, :, 512:768], w_ref[768:1024, :])
    o_ref[0] = x + _mod_rows(mod_ref, is_ctx, 2, D) * mix


def _out_proj(X, mod2, att, yf, yb, gbn, ln_w, ln_b, w, tm, j0, C):
    B, Ta, D = X.shape
    nt = Ta // tm - j0
    tok = lambda n: pl.BlockSpec((1, tm, n), lambda j, b: (b, j + j0, 0))
    full = _resident
    return pl.pallas_call(
        functools.partial(_out_kernel, j0=j0, C=C),
        grid=(nt, B),
        in_specs=[tok(D), _mod_spec(D), tok(768), tok(256), tok(256), tok(512), full(ln_w), full(ln_b), full(w)],
        out_specs=tok(D),
        out_shape=jax.ShapeDtypeStruct((B, Ta, D), F32),
        compiler_params=_cparams(("parallel", "parallel")),
        name="out_proj",
    )(X, mod2, att, yf, yb, gbn, ln_w, ln_b, w)


def _mlp_kernel(x_ref, xp_ref, xn_ref, mod_ref, g_ref, wa_ref, wb_ref, cw_ref, cb_ref, wd_ref, gf_ref, o_ref,
                *, j0, C, Ta, fc, final):
    j = pl.program_id(0) + j0
    tm = x_ref.shape[1]
    D = x_ref.shape[2]
    dff = wa_ref.shape[1]
    x = x_ref[0]
    ne = tm + 2 * MLP_HALO
    xe = jnp.concatenate([xp_ref[0], x, xn_ref[0]], axis=0)
    pos = _token_rows(j, tm)
    ctx_e = (j * tm - MLP_HALO + lax.broadcasted_iota(jnp.int32, (ne, 1), 0)) < C
    h = (_rms(xe, g_ref[...], NORM_EPS) * (1.0 + _mod_rows(mod_ref, ctx_e, 4, D))
         + _mod_rows(mod_ref, ctx_e, 3, D)).astype(BF16)
    pmask = jnp.where((pos == 0) | (pos == C), 0.0, 1.0)
    nmask = jnp.where((pos == C - 1) | (pos == Ta - 1), 0.0, 1.0)
    dot = functools.partial(jnp.dot, preferred_element_type=F32)
    acc = jnp.zeros((tm, D), F32)
    for c in range(dff // fc):
        cs = slice(c * fc, (c + 1) * fc)
        a = dot(h, wa_ref[:, cs])
        m0 = MLP_HALO
        b = dot(h[m0:m0 + tm], wb_ref[:, cs])
        cv = (cw_ref[0:1, cs] * (a[m0 - 1:m0 - 1 + tm] * pmask) + cw_ref[1:2, cs] * a[m0:m0 + tm]
              + cw_ref[2:3, cs] * (a[m0 + 1:m0 + 1 + tm] * nmask) + cb_ref[:, cs])
        u = cv * _sigmoid(cv) * b
        acc = acc + dot(u.astype(BF16), wd_ref[cs, :])
    y = x + _mod_rows(mod_ref, pos < C, 5, D) * acc
    o_ref[0] = _rms(y, gf_ref[...], NORM_EPS) if final else y


def _mlp(X, mod2, g, wa, wb, cw, cb, wd, gf, tm, j0, C, final):
    B, Ta, D = X.shape
    ntot = Ta // tm
    nt = ntot - j0
    hb = tm // MLP_HALO
    nh = Ta // MLP_HALO
    tok = lambda n: pl.BlockSpec((1, tm, n), lambda j, b: (b, j + j0, 0))
    full = _resident
    kern = functools.partial(_mlp_kernel, j0=j0, C=C, Ta=Ta, fc=wa.shape[1], final=final)
    if final:
        assert j0 * tm == C
        out_spec = pl.BlockSpec((1, tm, D), lambda j, b: (b, j, 0))
        out_shape = jax.ShapeDtypeStruct((B, Ta - C, D), F32)
    else:
        out_spec, out_shape = tok(D), jax.ShapeDtypeStruct((B, Ta, D), F32)
    return pl.pallas_call(
        kern,
        grid=(nt, B),
        in_specs=[tok(D),
                  pl.BlockSpec((1, MLP_HALO, D), lambda j, b: (b, jnp.maximum((j + j0) * hb - 1, j0 * hb), 0)),
                  pl.BlockSpec((1, MLP_HALO, D), lambda j, b: (b, jnp.minimum((j + j0 + 1) * hb, nh - 1), 0)),
                  _mod_spec(D),
                  full(g), full(wa), full(wb), full(cw), full(cb), full(wd), full(gf)],
        out_specs=out_spec,
        out_shape=out_shape,
        compiler_params=_cparams(("parallel", "parallel")),
        name="conv_glu",
    )(X, X, X, mod2, g, wa, wb, cw, cb, wd, gf)


def _rot_cols(w):
    s = w.shape
    x = w.reshape(s[:-1] + (s[-1] // ROPE_DIM, 4, ROPE_DIM // 4))
    r1, r2, c1, c2 = x[..., 0, :], x[..., 1, :], x[..., 2, :], x[..., 3, :]
    return jnp.stack([-r2, r1, -c2, c1], axis=-2).reshape(s)


def _rope_tables(T, C):
    t = np.arange(T)
    rowp = (t // GRID_W).astype(np.float32)
    colp = (t % GRID_W).astype(np.float32)
    half = ROPE_DIM // 2
    freqs = jnp.asarray(ROPE_THETA, F32) ** (-jnp.arange(0, half, 2, dtype=F32) / half)
    ar = jnp.asarray(rowp)[:, None] * freqs[None, :]
    ac = jnp.asarray(colp)[:, None] * freqs[None, :]
    ang = jnp.concatenate([ar, ar, ac, ac], axis=-1)
    cos = jnp.concatenate([jnp.ones((C, ROPE_DIM), F32), jnp.cos(ang)], axis=0)
    sin = jnp.concatenate([jnp.zeros((C, ROPE_DIM), F32), jnp.sin(ang)], axis=0)
    Ta = T + C
    cos256 = jnp.tile(cos, (1, 8))
    sin256 = jnp.tile(sin, (1, 8))
    one = jnp.ones((Ta, HEAD_DIM), F32)
    zero = jnp.zeros((Ta, HEAD_DIM), F32)
    cos512 = jnp.tile(jnp.concatenate([one, cos, one[:, :32]], axis=1), (1, 4))
    sin512 = jnp.tile(jnp.concatenate([zero, sin, zero[:, :32]], axis=1), (1, 4))
    return cos256, sin256, cos512, sin512


def _na_bias_tables(rpb):
    cpos = np.arange(GRID_W)
    cstart = np.clip(cpos - NA_KW // 2, 0, GRID_W - NA_KW)
    col_mask = (cpos[None, :] >= cstart[:, None]) & (cpos[None, :] < cstart[:, None] + NA_KW)
    col_idx = np.clip(cpos[None, :] - cpos[:, None] + NA_KW - 1, 0, 2 * NA_KW - 2)
    onehot = np.zeros((2 * NA_KW - 1, GRID_W, GRID_W), np.float32)
    onehot[col_idx, cpos[:, None], cpos[None, :]] = 1.0
    toep = jnp.einsum('lhrd,dqk->lhrqk', rpb, jnp.asarray(onehot), precision=lax.Precision.HIGHEST)
    toep = jnp.where(jnp.asarray(col_mask)[None, None, None], toep * LOG2E, NEG_INF)
    tabs = []
    for delta in range(NA_KH):
        bias = toep[:, :, NA_KH - 1 - delta:2 * NA_KH - 1 - delta]
        tabs.append(bias.transpose(0, 1, 3, 2, 4).reshape(rpb.shape[0], N_HEADS, GRID_W, NA_KH * GRID_W))
    return jnp.stack(tabs, axis=1)


def kernel(x, c, ctx, c_ctx, norm1_g, norm2_g, ada_w, ada_b, w_in, w_out, na_rpb, mla_q_norm, mla_kv_norm,
           mla_w_uq, mla_w_ukv, rwkv_mu, rwkv_w0, rwkv_w_up, rwkv_a0, rwkv_a_up, rwkv_g_up, rwkv_k_k,
           rwkv_k_a, rwkv_r_k, rwkv_ln_w, rwkv_ln_b, diff_lambda, diff_subln, mlp_w_up, mlp_conv_w,
           mlp_conv_b, mlp_w_down, final_norm_g):
    B, T, D = x.shape
    C = ctx.shape[1]
    L = ada_w.shape[0]
    Ta = T + C
    tm = min(TOKEN_TILE, C)
    rows = T // GRID_W
    assert C % tm == 0 and T % tm == 0 and tm % CHUNK == 0 and rows >= NA_KH and D == 1024
    nct = C // tm
    t_wide = WIDE_TILE if Ta % WIDE_TILE == 0 else tm
    t_mlp = MLP_TILE if Ta % MLP_TILE == 0 else tm
    att_sub = next(n for n in (3, 2, 1) if (Ta // tm) % n == 0)
    dff = mlp_w_down.shape[1]

    wi = w_in
    na_w = wi[:, :, 0:768]
    cq_w, ckv_w, kr_w = wi[:, :, 768:1024], wi[:, :, 1024:1152], wi[:, :, 1152:1184]
    rw_w = wi[:, :, 1184:2080]
    dq_w, dk_w, dv_w = wi[:, :, 2080:2336], wi[:, :, 2336:2592], wi[:, :, 2592:2848]
    w_all = jnp.concatenate([na_w, dq_w, _rot_cols(dq_w), dk_w, _rot_cols(dk_w), dv_w, rw_w,
                             cq_w, ckv_w, kr_w, _rot_cols(kr_w), jnp.zeros((L, D, 64), F32)],
                            axis=-1).astype(BF16)
    w_out_b = w_out.astype(BF16)
    wa_b = mlp_w_up[:, :, :dff].astype(BF16)
    wb_b = mlp_w_up[:, :, dff:].astype(BF16)
    wd_b = mlp_w_down.astype(BF16)

    uq = mla_w_uq.reshape(L, MLA_Q_RANK, N_HEADS, HEAD_DIM + ROPE_DIM)
    pad32 = jnp.zeros((L, MLA_Q_RANK, N_HEADS, 32), F32)
    wqm = jnp.concatenate([uq, pad32], axis=-1).reshape(L, MLA_Q_RANK, 512).astype(BF16)
    wqr = jnp.concatenate([jnp.zeros_like(uq[..., :HEAD_DIM]), _rot_cols(uq[..., HEAD_DIM:]), pad32],
                          axis=-1).reshape(L, MLA_Q_RANK, 512).astype(BF16)
    ukv = mla_w_ukv.reshape(L, MLA_KV_RANK, N_HEADS, 2 * HEAD_DIM)
    wk = jnp.concatenate([ukv[..., :HEAD_DIM], jnp.zeros_like(ukv[..., HEAD_DIM:])],
                         axis=-1).reshape(L, MLA_KV_RANK, 512).astype(BF16)
    wv = ukv[..., HEAD_DIM:].reshape(L, MLA_KV_RANK, 256).astype(BF16)
    pm = np.zeros((128, 1024), np.float32)
    for h in range(N_HEADS):
        for i in range(ROPE_DIM):
            pm[i, h * 128 + HEAD_DIM + i] = 1.0
            pm[ROPE_DIM + i, 512 + h * 128 + HEAD_DIM + i] = 1.0
    pmat = jnp.asarray(pm, BF16)

    zr = lambda n: jnp.zeros((L, 2, n, GROUP_W), F32)
    wup_p = jnp.concatenate([rwkv_w_up, zr(96)], axis=2).astype(BF16)
    aup_p = jnp.concatenate([zr(32), rwkv_a_up, zr(64)], axis=2).astype(BF16)
    gup_p = jnp.concatenate([jnp.zeros((L, 64, GROUP_W), F32), rwkv_g_up], axis=1).astype(BF16)

    cos256, sin256, cos512, sin512 = _rope_tables(T, C)
    na_bias = _na_bias_tables(na_rpb)
    sub256 = jnp.tile(diff_subln, (1, N_HEADS))

    R = ((B + 1 + 7) // 8) * 8
    cc = jnp.concatenate([c, c_ctx[None], jnp.zeros((R - B - 1, D), F32)], axis=0)
    mod = _modulation(cc, ada_w, ada_b)
    mod2 = jnp.stack([jnp.broadcast_to(mod[:, B:B + 1], (L, B, 6 * D)), mod[:, :B]], axis=2)
    mod2 = mod2.reshape(L, 2 * B, 1, 6 * D)

    X = jnp.concatenate([ctx, x], axis=1)
    gb = next(n for n in (8, 4, 2, 1) if B % n == 0)
    for l in range(L):
        need_ctx = l < L - 1
        j0 = 0 if need_ctx else nct
        na_qkv, dq, dk, dv, z_rw, mq, mk, mv = _in_proj(
            X, mod2[l], norm1_g[l][None], w_all[l], cos256, sin256, cos512, sin512,
            mla_q_norm[l][None], mla_kv_norm[l][None], wqm[l], wqr[l], wk[l], wv[l], pmat, t_wide, C)
        lam_init = 0.8 - 0.6 * math.exp(-0.3 * l)
        att = _attention(na_qkv, na_bias[l], mq, mk, mv, dq, dk, dv, diff_lambda[l], sub256[l][None],
                         tm, att_sub, not need_ctx, nct, C, rows, lam_init)
        rvk, gbn, dirp = _rwkv_prep(z_rw, rwkv_mu[l], rwkv_w0[l], wup_p[l], rwkv_a0[l], aup_p[l], gup_p[l],
                                    rwkv_k_k[l][None], rwkv_k_a[l][None], rwkv_r_k[l].reshape(1, GROUP_W),
                                    t_wide, C)
        chg, chb = _rwkv_chunks(rvk, dirp, t_wide)
        yf, yb = _rwkv_scan(chg, chb, C // CHUNK, gb)
        X = _out_proj(X, mod2[l], att, yf, yb, gbn, rwkv_ln_w[l][None], rwkv_ln_b[l][None],
                      w_out_b[l], t_wide if need_ctx else tm, j0, C)
        X = _mlp(X, mod2[l], norm2_g[l][None], wa_b[l], wb_b[l], mlp_conv_w[l], mlp_conv_b[l][None], wd_b[l],
                 final_norm_g[None], t_mlp if need_ctx else tm, j0, C, final=not need_ctx)
    return X
```
